```python
import jax
import jax.numpy as jnp
from jax import lax
import numpy as np


D_MODEL = 2048
BATCH = 16
SEQ = 2048
DEPTH = 2

HEAD_DIM = 128
RET_HEADS = 8
RET_W = RET_HEADS * HEAD_DIM
RET_CHUNK = 128
ROPE_BASE = 10000.0
NSA_HEADS = 8
NSA_KV_HEADS = 2
NSA_Q_W = NSA_HEADS * HEAD_DIM
NSA_KV_W = NSA_KV_HEADS * HEAD_DIM
NSA_N_BRANCH = 3
CMP_LEN = 32
CMP_STRIDE = 16
SEL_BLOCK = 64
SEL_TOPK = 16
NSA_WINDOW = 512
NSA_Q_BLOCK = 16
BAND_BLOCK = 128
CONV_W = 1024
CONV_K = 31
DIL_PATTERNS = ((128, 1), (512, 4), (2048, 16))
DIL_GROUP_HEADS = 4
DIL_HEADS = DIL_GROUP_HEADS * len(DIL_PATTERNS)
DIL_QKV_W = DIL_HEADS * HEAD_DIM
DIL_OUT_W = DIL_GROUP_HEADS * HEAD_DIM
EVEN_SPLITS = (RET_W, RET_W, RET_W, RET_W, NSA_Q_W, NSA_KV_W, NSA_KV_W, NSA_KV_W, NSA_KV_W, NSA_KV_W, NSA_KV_W, NSA_N_BRANCH * NSA_HEADS, NSA_Q_W)
EVEN_IN = sum(EVEN_SPLITS)
EVEN_MIX_W = RET_W + NSA_Q_W
ODD_SPLITS = (CONV_W, CONV_W, CONV_W, DIL_QKV_W, DIL_QKV_W, DIL_QKV_W, DIL_OUT_W)
ODD_IN = sum(ODD_SPLITS)
ODD_MIX_W = CONV_W + DIL_OUT_W
N_EVEN = (DEPTH + 1) // 2
N_ODD = DEPTH // 2
NORM_EPS = 1e-6
NEG_INF = -1e30

kernel_name = 'hybrid_retnet_nsa_conformer_dilated'


def rms_norm(x, gain):
    xf = x.astype(jnp.float32)
    y = xf * lax.rsqrt(jnp.mean(xf * xf, axis=-1, keepdims=True) + NORM_EPS)
    return (y * gain.astype(jnp.float32)).astype(x.dtype)


def split_cols(h, sizes):
    offs = np.cumsum(np.array(sizes))[:-1].tolist()
    return jnp.split(h, offs, axis=-1)


def heads(t, n):
    B, S, _ = t.shape
    return t.reshape(B, S, n, -1).transpose(0, 2, 1, 3)


def merge(t):
    B, n, S, d = t.shape
    return t.transpose(0, 2, 1, 3).reshape(B, S, n * d)


def rotary(t, pos):
    d = t.shape[-1]
    half = d // 2
    inv = ROPE_BASE ** (-jnp.arange(half, dtype=jnp.float32) / half)
    ang = pos.astype(jnp.float32)[:, None] * inv[None, :]
    cos, sin = jnp.cos(ang), jnp.sin(ang)
    tf = t.astype(jnp.float32)
    t1, t2 = tf[..., :half], tf[..., half:]
    return jnp.concatenate([t1 * cos - t2 * sin, t2 * cos + t1 * sin], axis=-1).astype(t.dtype)


def retention_chunkwise(q, k, v):
    B, H, S, d = q.shape
    C = RET_CHUNK
    N = S // C
    log_g = jnp.log(1.0 - 2.0 ** (-5.0 - jnp.arange(H, dtype=jnp.float32)))
    idx = jnp.arange(C, dtype=jnp.float32)
    diff = idx[:, None] - idx[None, :]
    dmask = jnp.where(diff >= 0, jnp.exp(jnp.maximum(diff, 0.0)[None] * log_g[:, None, None]), 0.0)
    qc = q.astype(jnp.float32).reshape(B, H, N, C, d)
    kc = (k.astype(jnp.float32) * d ** -0.5).reshape(B, H, N, C, d)
    vc = v.astype(jnp.float32).reshape(B, H, N, C, d)
    scores = jnp.einsum('bhncd,bhnmd->bhncm', qc, kc) * dmask[:, None]
    o_inner = jnp.einsum('bhncm,bhnme->bhnce', scores, vc)
    k_dec = jnp.exp((C - 1 - idx)[None, :] * log_g[:, None])
    kv = jnp.einsum('bhnmd,bhnme->nbhde', kc * k_dec[:, None, :, None], vc)
    chunk_dec = jnp.exp(C * log_g)[:, None, None]

    def step(state, kv_i):
        return chunk_dec * state + kv_i, state

    _, state_prev = lax.scan(step, jnp.zeros((B, H, d, d), jnp.float32), kv)
    q_dec = jnp.exp((idx + 1.0)[None, :] * log_g[:, None])
    o_cross = jnp.einsum('bhncd,nbhde->bhnce', qc * q_dec[:, None, :, None], state_prev)
    return (o_inner + o_cross).reshape(B, H, S, d)


def banded_attention(q, k, v, max_dist, block):
    B, G, R, L, d = q.shape
    nb = -(-L // block)
    Lp = nb * block
    pad_end = Lp - L
    qp = jnp.pad(q, ((0, 0), (0, 0), (0, 0), (0, pad_end), (0, 0)))
    kp = jnp.pad(k, ((0, 0), (0, 0), (max_dist, pad_end), (0, 0)))
    vp = jnp.pad(v, ((0, 0), (0, 0), (max_dist, pad_end), (0, 0)))
    span = block + max_dist
    kidx = np.arange(nb)[:, None] * block + np.arange(span)[None, :]
    kb = kp[:, :, kidx]
    vb = vp[:, :, kidx]
    qb = qp.reshape(B, G, R, nb, block, d)
    s = jnp.einsum('bgrnqd,bgnkd->bgrnqk', qb, kb).astype(jnp.float32) * d ** -0.5
    qpos = np.arange(nb)[:, None] * block + np.arange(block)[None, :]
    kpos = kidx - max_dist
    rel = qpos[:, :, None] - kpos[:, None, :]
    mask = (rel >= 0) & (rel <= max_dist) & (kpos[:, None, :] >= 0)
    s = jnp.where(mask, s, NEG_INF)
    lse = jax.nn.logsumexp(s, axis=-1)
    p = jnp.exp(s - lse[..., None])
    o = jnp.einsum('bgrnqk,bgnkd->bgrnqd', p, vb)
    o = o.reshape(B, G, R, Lp, d)[:, :, :, :L]
    lse = lse.reshape(B, G, R, Lp)[:, :, :, :L]
    return o, lse


def compress(t, pos_emb, w1, w2):
    B, G, S, d = t.shape
    n_cmp = (S - CMP_LEN) // CMP_STRIDE + 1
    idx = np.arange(n_cmp)[:, None] * CMP_STRIDE + np.arange(CMP_LEN)[None, :]
    blk = (t[:, :, idx] + pos_emb).reshape(B, G, n_cmp, CMP_LEN * d)
    return jax.nn.gelu(blk @ w1) @ w2


def compressed_attention(q, kcmp, vcmp):
    B, G, R, S, d = q.shape
    n_cmp = kcmp.shape[2]
    s = jnp.einsum('bgrsd,bgnd->bgrsn', q, kcmp).astype(jnp.float32) * d ** -0.5
    ends = np.arange(n_cmp) * CMP_STRIDE + CMP_LEN - 1
    valid = ends[None, :] <= np.arange(S)[:, None]
    any_valid = valid.any(axis=-1).astype(np.float32)[:, None]
    s = jnp.where(valid, s, NEG_INF)
    p = jax.nn.softmax(s, axis=-1) * any_valid
    o = jnp.einsum('bgrsn,bgnd->bgrsd', p, vcmp)
    return o, p


def select_blocks(p_cmp, S):
    n_cmp = p_cmp.shape[-1]
    n_sel = S // SEL_BLOCK
    cs = np.arange(n_cmp) * CMP_STRIDE
    ss = np.arange(n_sel) * SEL_BLOCK
    overlap = ((cs[:, None] < ss[None, :] + SEL_BLOCK) & (cs[:, None] + CMP_LEN > ss[None, :])).astype(np.float32)
    imp = jnp.einsum('bgrsn,nj->bgsj', p_cmp, jnp.asarray(overlap))
    cur = np.arange(S) // SEL_BLOCK
    j = np.arange(n_sel)
    forced = (j[None, :] == 0) | (j[None, :] == cur[:, None]) | (j[None, :] == cur[:, None] - 1)
    future = j[None, :] > cur[:, None]
    imp = jnp.where(forced, -NEG_INF, jnp.where(future, NEG_INF, imp))
    _, idx = lax.top_k(imp, min(SEL_TOPK, n_sel))
    return idx


def selected_attention(q, k, v, sel_idx):
    B, G, R, S, d = q.shape
    K = sel_idx.shape[-1]
    n_sel = S // SEL_BLOCK
    kb = k.reshape(B, G, n_sel, SEL_BLOCK, d)
    vb = v.reshape(B, G, n_sel, SEL_BLOCK, d)
    nq = S // NSA_Q_BLOCK
    q_c = q.reshape(B, G, R, nq, NSA_Q_BLOCK, d).transpose(3, 0, 1, 2, 4, 5)
    i_c = sel_idx.reshape(B, G, nq, NSA_Q_BLOCK, K).transpose(2, 0, 1, 3, 4)
    t_c = jnp.arange(S).reshape(nq, NSA_Q_BLOCK)
    bi = jnp.arange(B)[:, None, None, None]
    gi = jnp.arange(G)[None, :, None, None]
    offs = jnp.arange(SEL_BLOCK)

    def one(args):
        qb, ib, tb = args
        kg = kb[bi, gi, ib]
        vg = vb[bi, gi, ib]
        s = jnp.einsum('bgrqd,bgqkld->bgrqkl', qb, kg).astype(jnp.float32) * d ** -0.5
        kpos = ib[..., None] * SEL_BLOCK + offs
        ok = kpos <= tb[:, None, None]
        s = jnp.where(ok[:, :, None], s, NEG_INF)
        shp = s.shape
        p = jax.nn.softmax(s.reshape(shp[0], shp[1], shp[2], shp[3], -1), axis=-1).reshape(shp)
        return jnp.einsum('bgrqkl,bgqkld->bgrqd', p, vg)

    o = lax.map(one, (q_c, i_c, t_c))
    return o.transpose(1, 2, 3, 0, 4, 5).reshape(B, G, R, S, d)


def nsa_attention(q, kc, vc, ks, vs, kw, vw, gate_logits, q_g, k_g, pos_k, pos_v, k_w1, k_w2, v_w1, v_w2):
    B, S, _ = q.shape
    G, R = NSA_KV_HEADS, NSA_HEADS // NSA_KV_HEADS
    qh = rms_norm(heads(q, NSA_HEADS), q_g).reshape(B, G, R, S, HEAD_DIM)
    kcmp = rms_norm(compress(heads(kc, G), pos_k, k_w1, k_w2), k_g[0])
    vcmp = compress(heads(vc, G), pos_v, v_w1, v_w2)
    o_cmp, p_cmp = compressed_attention(qh, kcmp, vcmp)
    sel_idx = select_blocks(p_cmp, S)
    o_sel = selected_attention(qh, rms_norm(heads(ks, G), k_g[1]), heads(vs, G), sel_idx)
    o_win, _ = banded_attention(qh, rms_norm(heads(kw, G), k_g[2]), heads(vw, G), NSA_WINDOW - 1, BAND_BLOCK)
    gates = jax.nn.sigmoid(gate_logits.astype(jnp.float32)).reshape(B, S, NSA_N_BRANCH, G, R).transpose(2, 0, 3, 4, 1)[..., None]
    o = gates[0] * o_cmp + gates[1] * o_sel + gates[2] * o_win
    return merge(o.reshape(B, NSA_HEADS, S, HEAD_DIM)).astype(q.dtype)


def conformer_conv(u, g, dw_w, dw_b, ln_g, ln_b):
    a = u * jax.nn.sigmoid(g)
    C = a.shape[-1]
    a = lax.conv_general_dilated(a, dw_w[:, None, :].astype(a.dtype), (1,), [(CONV_K - 1, 0)], dimension_numbers=('NWC', 'WIO', 'NWC'), feature_group_count=C) + dw_b
    af = a.astype(jnp.float32)
    mu = jnp.mean(af, axis=-1, keepdims=True)
    var = jnp.mean(jnp.square(af - mu), axis=-1, keepdims=True)
    a = ((af - mu) * lax.rsqrt(var + NORM_EPS) * ln_g + ln_b).astype(u.dtype)
    return jax.nn.silu(a)


def strided_window_attention(q, k, v, window, dil):
    B, H, S, d = q.shape
    L = S // dil

    def to_sub(t):
        return t.reshape(B, H, L, dil, d).transpose(0, 1, 3, 2, 4).reshape(B, H * dil, L, d)

    o, lse = banded_attention(to_sub(q)[:, :, None], to_sub(k), to_sub(v), window // dil, BAND_BLOCK)
    o = o[:, :, 0].reshape(B, H, dil, L, d).transpose(0, 1, 3, 2, 4).reshape(B, H, S, d)
    lse = lse[:, :, 0].reshape(B, H, dil, L).transpose(0, 1, 3, 2).reshape(B, H, S)
    return o, lse


def dilated_attention(q, k, v, q_g, k_g):
    qh = rms_norm(heads(q, DIL_HEADS), q_g)
    kh = rms_norm(heads(k, DIL_HEADS), k_g)
    vh = heads(v, DIL_HEADS)
    outs, lses = [], []
    for gi, (window, dil) in enumerate(DIL_PATTERNS):
        sl = slice(gi * DIL_GROUP_HEADS, (gi + 1) * DIL_GROUP_HEADS)
        o, lse = strided_window_attention(qh[:, sl], kh[:, sl], vh[:, sl], window, dil)
        outs.append(o)
        lses.append(lse)
    w = jax.nn.softmax(jnp.stack(lses), axis=0)
    o = jnp.sum(w[..., None] * jnp.stack(outs), axis=0)
    return merge(o).astype(q.dtype)


def even_layer(x, norm_g, w_in, w_out, ret_g, nsa_qn, nsa_kn, pos_k, pos_v, k_w1, k_w2, v_w1, v_w2):
    B, S, _ = x.shape
    h = rms_norm(x, norm_g) @ w_in
    rq, rk, rv, rz, nq, kc, vc, ks, vs, kw, vw, ng, nz = split_cols(h, EVEN_SPLITS)
    pos = jnp.arange(S)
    o = retention_chunkwise(rotary(heads(rq, RET_HEADS), pos), rotary(heads(rk, RET_HEADS), pos), heads(rv, RET_HEADS))
    o = rms_norm(o, ret_g[:, None, :]).astype(x.dtype)
    a_out = merge(o) * jax.nn.silu(rz)
    b_out = nsa_attention(nq, kc, vc, ks, vs, kw, vw, ng, nsa_qn, nsa_kn, pos_k, pos_v, k_w1, k_w2, v_w1, v_w2) * jax.nn.silu(nz)
    return x + jnp.concatenate([a_out, b_out], axis=-1) @ w_out


def odd_layer(x, norm_g, w_in, w_out, dw_w, dw_b, cn_g, cn_b, dil_qn, dil_kn):
    h = rms_norm(x, norm_g) @ w_in
    cu, cg, cz, dq, dk, dv, dz = split_cols(h, ODD_SPLITS)
    c_out = conformer_conv(cu, cg, dw_w, dw_b, cn_g, cn_b) * jax.nn.silu(cz)
    d_out = dilated_attention(dq, dk, dv, dil_qn, dil_kn) * jax.nn.silu(dz)
    return x + jnp.concatenate([c_out, d_out], axis=-1) @ w_out


def setup_inputs(seed: int = 0) -> dict:
    key = jax.random.key(seed)
    ks = jax.random.split(key, 24)

    def nrm(k, shape, scale):
        return jax.random.normal(k, shape, jnp.float32) * scale

    def gain(k, shape):
        return 1.0 + 0.02 * jax.random.normal(k, shape, jnp.float32)

    flat = CMP_LEN * HEAD_DIM
    return {
        'x': nrm(ks[0], (BATCH, SEQ, D_MODEL), 1.0),
        'ev_norm': gain(ks[1], (N_EVEN, D_MODEL)),
        'ev_w_in': nrm(ks[2], (N_EVEN, D_MODEL, EVEN_IN), D_MODEL ** -0.5),
        'ev_w_out': nrm(ks[3], (N_EVEN, EVEN_MIX_W, D_MODEL), EVEN_MIX_W ** -0.5),
        'ev_ret_norm': gain(ks[4], (N_EVEN, RET_HEADS, HEAD_DIM)),
        'ev_nsa_q_norm': gain(ks[5], (N_EVEN, HEAD_DIM)),
        'ev_nsa_k_norm': gain(ks[6], (N_EVEN, NSA_N_BRANCH, HEAD_DIM)),
        'ev_cmp_pos_k': nrm(ks[7], (N_EVEN, CMP_LEN, HEAD_DIM), 0.02),
        'ev_cmp_pos_v': nrm(ks[8], (N_EVEN, CMP_LEN, HEAD_DIM), 0.02),
        'ev_cmp_k_w1': nrm(ks[9], (N_EVEN, flat, HEAD_DIM), flat ** -0.5),
        'ev_cmp_k_w2': nrm(ks[10], (N_EVEN, HEAD_DIM, HEAD_DIM), HEAD_DIM ** -0.5),
        'ev_cmp_v_w1': nrm(ks[11], (N_EVEN, flat, HEAD_DIM), flat ** -0.5),
        'ev_cmp_v_w2': nrm(ks[12], (N_EVEN, HEAD_DIM, HEAD_DIM), HEAD_DIM ** -0.5),
        'od_norm': gain(ks[13], (N_ODD, D_MODEL)),
        'od_w_in': nrm(ks[14], (N_ODD, D_MODEL, ODD_IN), D_MODEL ** -0.5),
        'od_w_out': nrm(ks[15], (N_ODD, ODD_MIX_W, D_MODEL), ODD_MIX_W ** -0.5),
        'od_dw_w': nrm(ks[16], (N_ODD, CONV_K, CONV_W), CONV_K ** -0.5),
        'od_dw_b': nrm(ks[17], (N_ODD, CONV_W), 0.01),
        'od_conv_norm_g': gain(ks[18], (N_ODD, CONV_W)),
        'od_conv_norm_b': nrm(ks[19], (N_ODD, CONV_W), 0.01),
        'od_dil_q_norm': gain(ks[20], (N_ODD, HEAD_DIM)),
        'od_dil_k_norm': gain(ks[21], (N_ODD, HEAD_DIM)),
    }


def reference(x, ev_norm, ev_w_in, ev_w_out, ev_ret_norm, ev_nsa_q_norm, ev_nsa_k_norm, ev_cmp_pos_k, ev_cmp_pos_v, ev_cmp_k_w1, ev_cmp_k_w2, ev_cmp_v_w1, ev_cmp_v_w2, od_norm, od_w_in, od_w_out, od_dw_w, od_dw_b, od_conv_norm_g, od_conv_norm_b, od_dil_q_norm, od_dil_k_norm):
    for layer in range(DEPTH):
        i = layer // 2
        if layer % 2 == 0:
            x = even_layer(x, ev_norm[i], ev_w_in[i], ev_w_out[i], ev_ret_norm[i], ev_nsa_q_norm[i], ev_nsa_k_norm[i], ev_cmp_pos_k[i], ev_cmp_pos_v[i], ev_cmp_k_w1[i], ev_cmp_k_w2[i], ev_cmp_v_w1[i], ev_cmp_v_w2[i])
        else:
            x = odd_layer(x, od_norm[i], od_w_in[i], od_w_out[i], od_dw_w[i], od_dw_b[i], od_conv_norm_g[i], od_conv_norm_b[i], od_dil_q_norm[i], od_dil_k_norm[i])
    return x
```

```python
import functools

import jax
import jax.numpy as jnp
import numpy as np
from jax import lax
from jax.experimental import pallas as pl
from jax.experimental.pallas import tpu as pltpu

f32 = jnp.float32
bf16 = jnp.bfloat16

D_MODEL = 2048
SEQ = 2048
HEAD_DIM = 128
SCALE = HEAD_DIM ** -0.5
NORM_EPS = 1e-6
NEG_INF = -1e30
LANE = 128
SUBLANE = 8

RET_HEADS = 8
RET_W = RET_HEADS * HEAD_DIM
RET_CHUNK = 128
ROPE_BASE = 10000.0
NSA_HEADS = 8
NSA_KV_HEADS = 2
NSA_REP = NSA_HEADS // NSA_KV_HEADS
NSA_N_BRANCH = 3
CMP_LEN = 32
CMP_STRIDE = 16
N_CMP = (SEQ - CMP_LEN) // CMP_STRIDE + 1
SEL_BLOCK = 64
N_SEL = SEQ // SEL_BLOCK
SEL_TOPK = 16
NSA_WINDOW = 512
NSA_TQ = 128
SEL_TK = 512
WIN_SPAN = NSA_WINDOW + NSA_TQ
CONV_W = 1024
CONV_K = 31
CONV_T = 512
CONV_HALO = 32
CONV_RB = 32
DIL_PATTERNS = ((128, 1), (512, 4), (2048, 16))
DIL_GROUP_HEADS = 4
DIL_HEADS = DIL_GROUP_HEADS * len(DIL_PATTERNS)
DIL_BLOCK = 128

EV_RQ, EV_RK, EV_RV, EV_RZ = 0, 8, 16, 24
EV_NQ = 32
EV_KC, EV_VC, EV_KS, EV_VS, EV_KW, EV_VW = 40, 42, 44, 46, 48, 50
EV_NZ = 52
EV_NG = 60
EV_COLS = 8192
OD_CU, OD_CG, OD_CZ = 0, 8, 16
OD_DQ, OD_DK, OD_DV = 24, 36, 48
OD_DZ = 60
OD_COLS = 8192

VMEM_LIMIT = 56 * 1024 * 1024


def _cparams(*sem):
    return pltpu.CompilerParams(dimension_semantics=sem, vmem_limit_bytes=VMEM_LIMIT)


def _rms(x, gain):
    ms = jnp.mean(x * x, axis=-1, keepdims=True)
    return x * lax.rsqrt(ms + NORM_EPS) * gain


def _silu(z):
    return z * jax.nn.sigmoid(z)


def _dot(a, b):
    return jnp.dot(a, b, preferred_element_type=f32)


def _dot_nt(a, b):
    return lax.dot_general(a, b, (((1,), (1,)), ((), ())), preferred_element_type=f32)


def _dot_tn(a, b):
    return lax.dot_general(a, b, (((0,), (0,)), ((), ())), preferred_element_type=f32)


PROJ_TM = 1024
PROJ_TN = 1024
NORM_ROWS = 256


def _proj_in_body(x_ref, g_ref, w_ref, o_ref, xn_ref):
    @pl.when(pl.program_id(1) == 0)
    def _():
        for c in range(PROJ_TM // NORM_ROWS):
            rows = slice(c * NORM_ROWS, (c + 1) * NORM_ROWS)
            xn_ref[rows, :] = _rms(x_ref[rows, :], g_ref[...]).astype(bf16)

    o_ref[...] = _dot(xn_ref[...], w_ref[...])


def _proj_in(x2, gain, w):
    m, d = x2.shape
    n = w.shape[1]
    return pl.pallas_call(
        _proj_in_body,
        grid=(m // PROJ_TM, n // PROJ_TN),
        in_specs=[
            pl.BlockSpec((PROJ_TM, d), lambda i, j: (i, 0)),
            pl.BlockSpec((1, d), lambda i, j: (0, 0)),
            pl.BlockSpec((d, PROJ_TN), lambda i, j: (0, j)),
        ],
        out_specs=pl.BlockSpec((PROJ_TM, PROJ_TN), lambda i, j: (i, j)),
        out_shape=jax.ShapeDtypeStruct((m, n), f32),
        scratch_shapes=[pltpu.VMEM((PROJ_TM, d), bf16)],
        compiler_params=_cparams("parallel", "arbitrary"),
        name="proj_in",
    )(x2, gain.reshape(1, d), w)


OUT_TM = 512


def _proj_out_body(x_ref, m0_ref, m1_ref, w0_ref, w1_ref, o_ref):
    o_ref[...] = x_ref[...] + _dot(m0_ref[...], w0_ref[...]) + _dot(m1_ref[...], w1_ref[...])


def _proj_out(x2, m0, m1, w0, w1):
    m, d = x2.shape
    k0, k1 = m0.shape[1], m1.shape[1]
    return pl.pallas_call(
        _proj_out_body,
        grid=(m // OUT_TM,),
        in_specs=[
            pl.BlockSpec((OUT_TM, d), lambda i: (i, 0)),
            pl.BlockSpec((OUT_TM, k0), lambda i: (i, 0)),
            pl.BlockSpec((OUT_TM, k1), lambda i: (i, 0)),
            pl.BlockSpec((k0, d), lambda i: (0, 0)),
            pl.BlockSpec((k1, d), lambda i: (0, 0)),
        ],
        out_specs=pl.BlockSpec((OUT_TM, d), lambda i: (i, 0)),
        out_shape=jax.ShapeDtypeStruct((m, d), f32),
        compiler_params=_cparams("parallel"),
        name="proj_out",
    )(x2, m0, m1, w0, w1)


def _ret_body(q_ref, k_ref, v_ref, z_ref, cos_ref, sin_ref, dmask_ref, kdec_ref, qdec_ref, cdec_ref, g_ref, o_ref):
    c_len = RET_CHUNK

    def rot(t, cs, sn):
        return t * cs + pltpu.roll(t, HEAD_DIM // 2, 1) * sn

    def body(n, state):
        rows = pl.ds(pl.multiple_of(n * c_len, c_len), c_len)
        cs = cos_ref[rows, :]
        sn = sin_ref[rows, :]
        q = rot(q_ref[rows, :], cs, sn)
        k = rot(k_ref[rows, :], cs, sn) * SCALE
        vb = v_ref[rows, :].astype(bf16)
        s = _dot_nt(q.astype(bf16), k.astype(bf16)) * dmask_ref[...]
        o = _dot(s.astype(bf16), vb)
        o = o + _dot((q * qdec_ref[...]).astype(bf16), state.astype(bf16))
        kv = _dot_tn((k * kdec_ref[...]).astype(bf16), vb)
        state = cdec_ref[...] * state + kv
        y = _rms(o, g_ref[...])
        o_ref[rows, :] = (y * _silu(z_ref[rows, :])).astype(o_ref.dtype)
        return state

    lax.fori_loop(0, SEQ // c_len, body, jnp.zeros((HEAD_DIM, HEAD_DIM), f32))


def _retention(h, ret_g):
    b = h.shape[0]
    c_len = RET_CHUNK
    half = HEAD_DIM // 2
    pos = jnp.arange(SEQ)
    inv = ROPE_BASE ** (-jnp.arange(half, dtype=f32) / half)
    ang = pos.astype(f32)[:, None] * inv[None, :]
    cos, sin = jnp.cos(ang), jnp.sin(ang)
    cos_t = jnp.concatenate([cos, cos], axis=-1)
    sin_t = jnp.concatenate([-sin, sin], axis=-1)
    log_g = jnp.log(1.0 - 2.0 ** (-5.0 - jnp.arange(RET_HEADS, dtype=f32)))
    idx = jnp.arange(c_len, dtype=f32)
    diff = idx[:, None] - idx[None, :]
    dmask = jnp.where(diff >= 0, jnp.exp(jnp.maximum(diff, 0.0)[None] * log_g[:, None, None]), 0.0)
    ones = jnp.ones((1, 1, HEAD_DIM), f32)
    k_dec = jnp.exp((c_len - 1 - idx)[None, :] * log_g[:, None])[:, :, None] * ones
    q_dec = jnp.exp((idx + 1.0)[None, :] * log_g[:, None])[:, :, None] * ones
    chunk_dec = jnp.exp(c_len * log_g)[:, None, None] * ones

    def col(off):
        return pl.BlockSpec((None, SEQ, HEAD_DIM), lambda bi, hi: (bi, 0, off + hi))

    table = pl.BlockSpec((SEQ, HEAD_DIM), lambda bi, hi: (0, 0))

    def per_head(rows):
        return pl.BlockSpec((None, rows, HEAD_DIM), lambda bi, hi: (hi, 0, 0))

    return pl.pallas_call(
        _ret_body,
        grid=(b, RET_HEADS),
        in_specs=[col(EV_RQ), col(EV_RK), col(EV_RV), col(EV_RZ), table, table,
                  per_head(c_len), per_head(c_len), per_head(c_len), per_head(1), per_head(1)],
        out_specs=pl.BlockSpec((None, SEQ, HEAD_DIM), lambda bi, hi: (bi, 0, hi)),
        out_shape=jax.ShapeDtypeStruct((b, SEQ, RET_W), bf16),
        compiler_params=_cparams("parallel", "parallel"),
        name="retention",
    )(h, h, h, h, cos_t, sin_t, dmask, k_dec, q_dec, chunk_dec, ret_g.reshape(RET_HEADS, 1, HEAD_DIM))


def _nsa_body(q_ref, kc_ref, vc_ref, ks_ref, vs_ref, kw_ref, vw_ref, nz_ref, ng_ref,
              qg_ref, kg_ref, posk_ref, posv_ref, w1k_ref, w2k_ref, w1v_ref, w2v_ref, ovl_ref, exp_ref,
              o_ref, kcmp_s, vcmp_s, ksn_s, kwn_s, vsb_s, vwb_s):
    tq = NSA_TQ
    rep = NSA_REP
    half_blk = CMP_LEN // 2

    def compress(t_ref, pos_ref, w1_ref, w2_ref):
        first = jnp.zeros((SEQ // CMP_STRIDE, HEAD_DIM), f32)
        second = jnp.zeros((SEQ // CMP_STRIDE, HEAD_DIM), f32)
        for i in range(half_blk):
            ti = t_ref[pl.ds(i, SEQ // CMP_STRIDE, stride=CMP_STRIDE), :]
            first = first + _dot((ti + pos_ref[i:i + 1, :]).astype(bf16), w1_ref[i * HEAD_DIM:(i + 1) * HEAD_DIM, :])
            j = half_blk + i
            second = second + _dot((ti + pos_ref[j:j + 1, :]).astype(bf16), w1_ref[j * HEAD_DIM:(j + 1) * HEAD_DIM, :])
        pre = first + pltpu.roll(second, SEQ // CMP_STRIDE - 1, 0)
        return _dot(jax.nn.gelu(pre).astype(bf16), w2_ref[...])

    kcmp_s[...] = _rms(compress(kc_ref, posk_ref, w1k_ref, w2k_ref), kg_ref[0:1, :]).astype(bf16)
    vcmp_s[...] = compress(vc_ref, posv_ref, w1v_ref, w2v_ref).astype(bf16)
    for c in range(SEQ // 256):
        rows = slice(c * 256, (c + 1) * 256)
        ksn_s[rows, :] = _rms(ks_ref[rows, :], kg_ref[1:2, :]).astype(bf16)
        kwn_s[rows, :] = _rms(kw_ref[rows, :], kg_ref[2:3, :]).astype(bf16)
        vsb_s[rows, :] = vs_ref[rows, :].astype(bf16)
        vwb_s[rows, :] = vw_ref[rows, :].astype(bf16)

    def tile(i, carry):
        t0 = i * tq
        rows = pl.ds(pl.multiple_of(t0, tq), tq)
        qs = [_rms(q_ref[rows, r * HEAD_DIM:(r + 1) * HEAD_DIM], qg_ref[...]).astype(bf16) for r in range(rep)]
        qst = jnp.concatenate(qs, axis=0)
        tpos = t0 + lax.broadcasted_iota(jnp.int32, (tq, 1), 0)

        n_idx = lax.broadcasted_iota(jnp.int32, (1, LANE), 1)
        valid = (n_idx * CMP_STRIDE + (CMP_LEN - 1) <= tpos) & (n_idx < N_CMP)
        cbias = jnp.where(valid, 0.0, NEG_INF)
        sc = (_dot_nt(qst, kcmp_s[...]) * SCALE).reshape(rep, tq, LANE) + cbias[None]
        mc = jnp.max(sc, axis=-1, keepdims=True)
        ec = jnp.exp(sc - mc)
        any_valid = jnp.where(tpos >= CMP_LEN - 1, 1.0, 0.0)
        pc = ec / jnp.sum(ec, axis=-1, keepdims=True) * any_valid[None]
        o_cmp = _dot(pc.reshape(rep * tq, LANE).astype(bf16), vcmp_s[...])
        psum = pc[0]
        for r in range(1, rep):
            psum = psum + pc[r]

        imp = lax.dot_general(ovl_ref[...], psum, (((1,), (1,)), ((), ())),
                              preferred_element_type=f32, precision=lax.Precision.HIGHEST)[:N_SEL, :]
        jj = lax.broadcasted_iota(jnp.int32, (N_SEL, 1), 0)
        cur = (t0 + lax.broadcasted_iota(jnp.int32, (1, tq), 1)) // SEL_BLOCK
        forced = (jj == 0) | (jj == cur) | (jj == cur - 1)
        imp = jnp.where(forced, -NEG_INF, jnp.where(jj > cur, NEG_INF, imp))
        rank = jnp.zeros((N_SEL, tq), f32)
        for j2 in range(N_SEL):
            row = imp[j2:j2 + 1, :]
            beats = (row > imp) | ((row == imp) & (j2 < jj))
            rank = rank + jnp.where(beats, 1.0, 0.0)
        sel_t = jnp.where(rank < SEL_TOPK, 1.0, 0.0).astype(bf16)

        def sel_tile(kt, st):
            m, l, acc = st
            krows = pl.ds(pl.multiple_of(kt * SEL_TK, SEL_TK), SEL_TK)
            picked = _dot_tn(sel_t, exp_ref[kt])
            kpos = kt * SEL_TK + lax.broadcasted_iota(jnp.int32, (1, SEL_TK), 1)
            sbias = jnp.where((kpos <= tpos) & (picked > 0.5), 0.0, NEG_INF)
            s = _dot_nt(qst, ksn_s[krows, :]).reshape(rep, tq, SEL_TK) + sbias[None]
            m_new = jnp.maximum(m, jnp.max(s, axis=-1, keepdims=True))
            alpha = jnp.exp((m - m_new) * SCALE)
            e = jnp.exp((s - m_new) * SCALE)
            l = alpha * l + jnp.sum(e, axis=-1, keepdims=True)
            pv = _dot(e.reshape(rep * tq, SEL_TK).astype(bf16), vsb_s[krows, :]).reshape(rep, tq, HEAD_DIM)
            return m_new, l, alpha * acc + pv

        n_kt = (t0 + tq - 1) // SEL_TK + 1
        m0 = jnp.full((rep, tq, 1), NEG_INF, f32)
        _, l_s, acc_s = lax.fori_loop(0, n_kt, sel_tile, (m0, jnp.zeros((rep, tq, 1), f32), jnp.zeros((rep, tq, HEAD_DIM), f32)))
        o_sel = acc_s / l_s

        w0 = jnp.maximum(t0 + tq - WIN_SPAN, 0)
        wrows = pl.ds(pl.multiple_of(w0, tq), WIN_SPAN)
        rel = tpos - (w0 + lax.broadcasted_iota(jnp.int32, (1, WIN_SPAN), 1))
        wbias = jnp.where((rel >= 0) & (rel <= NSA_WINDOW - 1), 0.0, NEG_INF)
        sw = _dot_nt(qst, kwn_s[wrows, :]).reshape(rep, tq, WIN_SPAN) + wbias[None]
        mw = jnp.max(sw, axis=-1, keepdims=True)
        ew = jnp.exp((sw - mw) * SCALE)
        o_win = _dot(ew.reshape(rep * tq, WIN_SPAN).astype(bf16), vwb_s[wrows, :]).reshape(rep, tq, HEAD_DIM)
        o_win = o_win / jnp.sum(ew, axis=-1, keepdims=True)

        gates = jax.nn.sigmoid(ng_ref[rows, :])
        o_cmp = o_cmp.reshape(rep, tq, HEAD_DIM)
        for r in range(rep):
            g_c = gates[:, r:r + 1]
            g_s = gates[:, rep + r:rep + r + 1]
            g_w = gates[:, 2 * rep + r:2 * rep + r + 1]
            o = g_c * o_cmp[r] + g_s * o_sel[r] + g_w * o_win[r]
            cols = slice(r * HEAD_DIM, (r + 1) * HEAD_DIM)
            o_ref[rows, cols] = (o * _silu(nz_ref[rows, cols])).astype(o_ref.dtype)
        return carry

    lax.fori_loop(0, SEQ // tq, tile, 0)


def _nsa(h, q_g, k_g, pos_k, pos_v, k_w1, k_w2, v_w1, v_w2):
    b = h.shape[0]
    grp_w = NSA_REP * HEAD_DIM
    cs = np.arange(LANE) * CMP_STRIDE
    ss = np.arange(LANE) * SEL_BLOCK
    ovl = ((cs[None, :] < ss[:, None] + SEL_BLOCK) & (cs[None, :] + CMP_LEN > ss[:, None])
           & (np.arange(LANE)[None, :] < N_CMP) & (np.arange(LANE)[:, None] < N_SEL))
    ovl_t = jnp.asarray(ovl.astype(np.float32))
    key_blk = np.arange(SEQ) // SEL_BLOCK
    expand = (np.arange(N_SEL)[:, None] == key_blk[None, :]).astype(np.float32)
    expand = jnp.asarray(expand.reshape(N_SEL, SEQ // SEL_TK, SEL_TK).transpose(1, 0, 2), dtype=bf16)

    def wide(off):
        return pl.BlockSpec((None, SEQ, grp_w), lambda bi, gi: (bi, 0, off // NSA_REP + gi))

    def col(off):
        return pl.BlockSpec((None, SEQ, HEAD_DIM), lambda bi, gi: (bi, 0, off + gi))

    def full(shape):
        return pl.BlockSpec(shape, lambda bi, gi: (0,) * len(shape))

    flat = CMP_LEN * HEAD_DIM
    return pl.pallas_call(
        _nsa_body,
        grid=(b, NSA_KV_HEADS),
        in_specs=[wide(EV_NQ), col(EV_KC), col(EV_VC), col(EV_KS), col(EV_VS), col(EV_KW), col(EV_VW),
                  wide(EV_NZ), col(EV_NG),
                  full((1, HEAD_DIM)), full((NSA_N_BRANCH, HEAD_DIM)), full((CMP_LEN, HEAD_DIM)), full((CMP_LEN, HEAD_DIM)),
                  full((flat, HEAD_DIM)), full((HEAD_DIM, HEAD_DIM)), full((flat, HEAD_DIM)), full((HEAD_DIM, HEAD_DIM)),
                  full((LANE, LANE)), full((SEQ // SEL_TK, N_SEL, SEL_TK))],
        out_specs=pl.BlockSpec((None, SEQ, grp_w), lambda bi, gi: (bi, 0, gi)),
        out_shape=jax.ShapeDtypeStruct((b, SEQ, NSA_HEADS * HEAD_DIM), bf16),
        scratch_shapes=[pltpu.VMEM((SEQ // CMP_STRIDE, HEAD_DIM), bf16), pltpu.VMEM((SEQ // CMP_STRIDE, HEAD_DIM), bf16),
                        pltpu.VMEM((SEQ, HEAD_DIM), bf16), pltpu.VMEM((SEQ, HEAD_DIM), bf16),
                        pltpu.VMEM((SEQ, HEAD_DIM), bf16), pltpu.VMEM((SEQ, HEAD_DIM), bf16)],
        compiler_params=_cparams("parallel", "parallel"),
        name="nsa",
    )(h, h, h, h, h, h, h, h, h,
      q_g.reshape(1, HEAD_DIM), k_g, pos_k, pos_v,
      k_w1.astype(bf16), k_w2.astype(bf16), v_w1.astype(bf16), v_w2.astype(bf16), ovl_t, expand)


def _conv_body(u_ref, g_ref, z_ref, w_ref, b_ref, lg_ref, lb_ref, o_ref, a_ext, a_sh):
    ext = CONV_HALO + CONV_T

    @pl.when(pl.program_id(1) == 0)
    def _():
        a_ext[0:CONV_HALO, :] = jnp.zeros((CONV_HALO, CONV_W), f32)
        a_ext[ext:ext + SUBLANE, :] = jnp.zeros((SUBLANE, CONV_W), f32)

    for c in range(CONV_T // 128):
        rows = slice(c * 128, (c + 1) * 128)
        a_ext[CONV_HALO + c * 128:CONV_HALO + (c + 1) * 128, :] = u_ref[rows, :] * jax.nn.sigmoid(g_ref[rows, :])

    for r in range(1, SUBLANE):
        for c in range(ext // CONV_RB):
            a_sh[r - 1, c * CONV_RB:(c + 1) * CONV_RB, :] = a_ext[c * CONV_RB + r:(c + 1) * CONV_RB + r, :]

    lead = CONV_HALO - (CONV_K - 1)

    def block(rb, carry):
        r0 = pl.multiple_of(rb * CONV_RB, CONV_RB)
        accs = []
        for cb in range(CONV_W // LANE):
            lanes = slice(cb * LANE, (cb + 1) * LANE)
            acc = jnp.zeros((CONV_RB, LANE), f32)
            for k in range(CONV_K):
                shift, base = (lead + k) % SUBLANE, (lead + k) // SUBLANE * SUBLANE
                rows = pl.ds(r0 + base, CONV_RB)
                tap = a_ext[rows, lanes] if shift == 0 else a_sh[shift - 1, rows, lanes]
                acc = acc + tap * w_ref[k:k + 1, lanes]
            accs.append(acc + b_ref[:, lanes])
        y = jnp.concatenate(accs, axis=1)
        mu = jnp.mean(y, axis=-1, keepdims=True)
        var = jnp.mean(jnp.square(y - mu), axis=-1, keepdims=True)
        yn = (y - mu) * lax.rsqrt(var + NORM_EPS) * lg_ref[...] + lb_ref[...]
        rows = pl.ds(r0, CONV_RB)
        o_ref[rows, :] = (_silu(yn) * _silu(z_ref[rows, :])).astype(o_ref.dtype)
        return carry

    lax.fori_loop(0, CONV_T // CONV_RB, block, 0)
    a_ext[0:CONV_HALO, :] = a_ext[CONV_T:CONV_T + CONV_HALO, :]


def _conv(h, dw_w, dw_b, ln_g, ln_b):
    b = h.shape[0]

    def col(off):
        return pl.BlockSpec((None, CONV_T, CONV_W), lambda bi, ti: (bi, ti, off * LANE // CONV_W))

    def full(rows):
        return pl.BlockSpec((rows, CONV_W), lambda bi, ti: (0, 0))

    return pl.pallas_call(
        _conv_body,
        grid=(b, SEQ // CONV_T),
        in_specs=[col(OD_CU), col(OD_CG), col(OD_CZ), full(CONV_K), full(1), full(1), full(1)],
        out_specs=pl.BlockSpec((None, CONV_T, CONV_W), lambda bi, ti: (bi, ti, 0)),
        out_shape=jax.ShapeDtypeStruct((b, SEQ, CONV_W), bf16),
        scratch_shapes=[pltpu.VMEM((CONV_HALO + CONV_T + SUBLANE, CONV_W), f32),
                        pltpu.VMEM((SUBLANE - 1, CONV_HALO + CONV_T, CONV_W), f32)],
        compiler_params=_cparams("parallel", "arbitrary"),
        name="conformer_conv",
    )(h, h, h, dw_w, dw_b.reshape(1, CONV_W), ln_g.reshape(1, CONV_W), ln_b.reshape(1, CONV_W))


def _dil_body(q0_ref, k0_ref, v0_ref, q1_ref, k1_ref, v1_ref, q2_ref, k2_ref, v2_ref, z_ref, qg_ref, kg_ref,
              o_ref, qn_s, kn_s, vb_s, o0_s, o1_s, o2_s, l0_s, l1_s, l2_s):
    blk = DIL_BLOCK
    n_tiles = SEQ // blk
    groups = ((q0_ref, k0_ref, v0_ref, o0_s, l0_s), (q1_ref, k1_ref, v1_ref, o1_s, l1_s), (q2_ref, k2_ref, v2_ref, o2_s, l2_s))
    row_i = lax.broadcasted_iota(jnp.int32, (blk, blk), 0)
    col_i = lax.broadcasted_iota(jnp.int32, (blk, blk), 1)

    for (q_ref, k_ref, v_ref, o_s, l_s), (window, dil) in zip(groups, DIL_PATTERNS):
        assert window // dil == blk
        nb = SEQ // dil // blk

        def natural_rows(u, dil=dil, nb=nb):
            start = u // nb + (u % nb) * (blk * dil)
            return pl.ds(start, blk, stride=dil) if dil > 1 else pl.ds(pl.multiple_of(start, blk), blk)

        def prep(u, carry, q_ref=q_ref, k_ref=k_ref, v_ref=v_ref, natural_rows=natural_rows):
            src = natural_rows(u)
            dst = pl.ds(pl.multiple_of(u * blk, blk), blk)
            qn_s[dst, :] = _rms(q_ref[src, :], qg_ref[...]).astype(bf16)
            kn_s[dst, :] = _rms(k_ref[src, :], kg_ref[...]).astype(bf16)
            vb_s[dst, :] = v_ref[src, :].astype(bf16)
            return carry

        lax.fori_loop(0, n_tiles, prep, 0)

        def att(u, carry, o_s=o_s, l_s=l_s, nb=nb, natural_rows=natural_rows):
            cur = pl.ds(pl.multiple_of(u * blk, blk), blk)
            prev = pl.ds(pl.multiple_of(jnp.maximum(u - 1, 0) * blk, blk), blk)
            no_prev = jnp.where((u % nb) > 0, 0, blk)
            q = qn_s[cur, :]
            s_c = jnp.where(col_i <= row_i, _dot_nt(q, kn_s[cur, :]) * SCALE, NEG_INF)
            s_p = jnp.where(col_i >= row_i + no_prev, _dot_nt(q, kn_s[prev, :]) * SCALE, NEG_INF)
            m = jnp.maximum(jnp.max(s_c, axis=-1, keepdims=True), jnp.max(s_p, axis=-1, keepdims=True))
            e_c = jnp.exp(s_c - m)
            e_p = jnp.exp(s_p - m)
            l = jnp.sum(e_c, axis=-1, keepdims=True) + jnp.sum(e_p, axis=-1, keepdims=True)
            o = (_dot(e_c.astype(bf16), vb_s[cur, :]) + _dot(e_p.astype(bf16), vb_s[prev, :])) / l
            dst = natural_rows(u)
            o_s[dst, :] = o
            l_s[dst, :] = jnp.broadcast_to(m + jnp.log(l), (blk, HEAD_DIM))
            return carry

        lax.fori_loop(0, n_tiles, att, 0)

    def mix(c, carry):
        rows = pl.ds(pl.multiple_of(c * blk, blk), blk)
        l0, l1, l2 = l0_s[rows, :], l1_s[rows, :], l2_s[rows, :]
        m = jnp.maximum(jnp.maximum(l0, l1), l2)
        w0, w1, w2 = jnp.exp(l0 - m), jnp.exp(l1 - m), jnp.exp(l2 - m)
        o = (w0 * o0_s[rows, :] + w1 * o1_s[rows, :] + w2 * o2_s[rows, :]) / (w0 + w1 + w2)
        o_ref[rows, :] = (o * _silu(z_ref[rows, :])).astype(o_ref.dtype)
        return carry

    lax.fori_loop(0, n_tiles, mix, 0)


def _dilated(h, q_g, k_g):
    b = h.shape[0]

    def col(off):
        return pl.BlockSpec((None, SEQ, HEAD_DIM), lambda bi, hi: (bi, 0, off + hi))

    gain = pl.BlockSpec((1, HEAD_DIM), lambda bi, hi: (0, 0))
    specs = []
    for gi in range(len(DIL_PATTERNS)):
        specs += [col(OD_DQ + gi * DIL_GROUP_HEADS), col(OD_DK + gi * DIL_GROUP_HEADS), col(OD_DV + gi * DIL_GROUP_HEADS)]
    return pl.pallas_call(
        _dil_body,
        grid=(b, DIL_GROUP_HEADS),
        in_specs=specs + [col(OD_DZ), gain, gain],
        out_specs=pl.BlockSpec((None, SEQ, HEAD_DIM), lambda bi, hi: (bi, 0, hi)),
        out_shape=jax.ShapeDtypeStruct((b, SEQ, DIL_GROUP_HEADS * HEAD_DIM), bf16),
        scratch_shapes=[pltpu.VMEM((SEQ, HEAD_DIM), bf16)] * 3 + [pltpu.VMEM((SEQ, HEAD_DIM), f32)] * 6,
        compiler_params=_cparams("parallel", "parallel"),
        name="dilated",
    )(*([h] * 10), q_g.reshape(1, HEAD_DIM), k_g.reshape(1, HEAD_DIM))


def _even_w_in(w):
    ng0 = EV_NZ * LANE
    nz0 = ng0 + NSA_N_BRANCH * NSA_HEADS
    ng = w[:, ng0:nz0].reshape(-1, NSA_N_BRANCH, NSA_KV_HEADS, NSA_REP)
    ng = ng.transpose(0, 2, 1, 3).reshape(-1, NSA_KV_HEADS, NSA_N_BRANCH * NSA_REP)
    ng = jnp.pad(ng, ((0, 0), (0, 0), (0, LANE - NSA_N_BRANCH * NSA_REP))).reshape(-1, NSA_KV_HEADS * LANE)
    out = jnp.concatenate([w[:, :ng0], w[:, nz0:], ng], axis=1)
    return jnp.pad(out, ((0, 0), (0, EV_COLS - out.shape[1]))).astype(bf16)


def kernel(x, ev_norm, ev_w_in, ev_w_out, ev_ret_norm, ev_nsa_q_norm, ev_nsa_k_norm, ev_cmp_pos_k, ev_cmp_pos_v, ev_cmp_k_w1, ev_cmp_k_w2, ev_cmp_v_w1, ev_cmp_v_w2, od_norm, od_w_in, od_w_out, od_dw_w, od_dw_b, od_conv_norm_g, od_conv_norm_b, od_dil_q_norm, od_dil_k_norm):
    b, s, d = x.shape
    assert (s, d) == (SEQ, D_MODEL)
    x0 = x.reshape(b * s, d)

    h0 = _proj_in(x0, ev_norm[0], _even_w_in(ev_w_in[0])).reshape(b, s, EV_COLS)
    a_out = _retention(h0, ev_ret_norm[0])
    b_out = _nsa(h0, ev_nsa_q_norm[0], ev_nsa_k_norm[0], ev_cmp_pos_k[0], ev_cmp_pos_v[0],
                 ev_cmp_k_w1[0], ev_cmp_k_w2[0], ev_cmp_v_w1[0], ev_cmp_v_w2[0])
    wo = ev_w_out[0].astype(bf16)
    x1 = _proj_out(x0, a_out.reshape(b * s, -1), b_out.reshape(b * s, -1), wo[:RET_W], wo[RET_W:])

    h1 = _proj_in(x1, od_norm[0], od_w_in[0].astype(bf16)).reshape(b, s, OD_COLS)
    c_out = _conv(h1, od_dw_w[0], od_dw_b[0], od_conv_norm_g[0], od_conv_norm_b[0])
    d_out = _dilated(h1, od_dil_q_norm[0], od_dil_k_norm[0])
    wo = od_w_out[0].astype(bf16)
    x2 = _proj_out(x1, c_out.reshape(b * s, -1), d_out.reshape(b * s, -1), wo[:CONV_W], wo[CONV_W:])
    return x2.reshape(b, s, d)
```

```python
import jax
import jax.numpy as jnp
import numpy as np
from jax import lax
from jax.experimental import pallas as pl
from jax.experimental.pallas import tpu as pltpu

f32 = jnp.float32
bf16 = jnp.bfloat16

D_MODEL = 2048
SEQ = 2048
HEAD_DIM = 128
SCALE = HEAD_DIM ** -0.5
NORM_EPS = 1e-6
NEG_INF = -1e30
LANE = 128
SUBLANE = 8

RET_HEADS = 8
RET_W = RET_HEADS * HEAD_DIM
RET_CHUNK = 128
RET_UNROLL = 8
ROPE_BASE = 10000.0
NSA_HEADS = 8
NSA_KV_HEADS = 2
NSA_REP = NSA_HEADS // NSA_KV_HEADS
NSA_N_BRANCH = 3
CMP_LEN = 32
CMP_STRIDE = 16
N_CMP = (SEQ - CMP_LEN) // CMP_STRIDE + 1
SEL_BLOCK = 64
N_SEL = SEQ // SEL_BLOCK
SEL_TOPK = 16
NSA_WINDOW = 512
NSA_TQ = 128
SEL_TK = 512
NSA_SELECT_UNROLL = 4
WIN_SPAN = NSA_WINDOW + NSA_TQ
CONV_W = 1024
CONV_K = 31
CONV_T = 512
CONV_HALO = 32
CONV_RB = 64
DIL_PATTERNS = ((128, 1), (512, 4), (2048, 16))
DIL_GROUP_HEADS = 4
DIL_HEADS = DIL_GROUP_HEADS * len(DIL_PATTERNS)
DIL_BLOCK = 128
DIL_UNROLL = 4

EV_RQ, EV_RK, EV_RV, EV_RZ = 0, 8, 16, 24
EV_NQ = 32
EV_KC, EV_VC, EV_KS, EV_VS, EV_KW, EV_VW = 40, 42, 44, 46, 48, 50
EV_NZ = 52
EV_NG = 60
EV_COLS = 8192
OD_CU, OD_CG, OD_CZ = 0, 8, 16
OD_DQ, OD_DK, OD_DV = 24, 36, 48
OD_DZ = 60
OD_COLS = 8192

VMEM_LIMIT = 56 * 1024 * 1024


def _cparams(*sem):
    return pltpu.CompilerParams(dimension_semantics=sem, vmem_limit_bytes=VMEM_LIMIT)


def _rms(x, gain):
    ms = jnp.mean(x * x, axis=-1, keepdims=True)
    return x * lax.rsqrt(ms + NORM_EPS) * gain


def _silu(z):
    return z * jax.nn.sigmoid(z)


def _dot(a, b):
    return jnp.dot(a, b, preferred_element_type=f32)


def _dot_nt(a, b):
    return lax.dot_general(a, b, (((1,), (1,)), ((), ())), preferred_element_type=f32)


def _dot_tn(a, b):
    return lax.dot_general(a, b, (((0,), (0,)), ((), ())), preferred_element_type=f32)


PROJ_TM = 1024
PROJ_TN = 1024
NORM_ROWS = 256


def _proj_in_body(x_ref, g_ref, w_ref, o_ref, xn_ref):
    @pl.when(pl.program_id(1) == 0)
    def _():
        for c in range(PROJ_TM // NORM_ROWS):
            rows = slice(c * NORM_ROWS, (c + 1) * NORM_ROWS)
            xn_ref[rows, :] = _rms(x_ref[rows, :], g_ref[...]).astype(bf16)

    o_ref[...] = _dot(xn_ref[...], w_ref[...])


def _proj_in(x2, gain, w):
    m, d = x2.shape
    n = w.shape[1]
    return pl.pallas_call(
        _proj_in_body,
        grid=(m // PROJ_TM, n // PROJ_TN),
        in_specs=[
            pl.BlockSpec((PROJ_TM, d), lambda i, j: (i, 0)),
            pl.BlockSpec((1, d), lambda i, j: (0, 0)),
            pl.BlockSpec((d, PROJ_TN), lambda i, j: (0, j)),
        ],
        out_specs=pl.BlockSpec((PROJ_TM, PROJ_TN), lambda i, j: (i, j)),
        out_shape=jax.ShapeDtypeStruct((m, n), f32),
        scratch_shapes=[pltpu.VMEM((PROJ_TM, d), bf16)],
        compiler_params=_cparams("parallel", "arbitrary"),
        name="proj_in",
    )(x2, gain.reshape(1, d), w)


OUT_TM = 512


def _proj_out_body(x_ref, m0_ref, m1_ref, w0_ref, w1_ref, o_ref):
    o_ref[...] = x_ref[...] + _dot(m0_ref[...], w0_ref[...]) + _dot(m1_ref[...], w1_ref[...])


def _proj_out(x2, m0, m1, w0, w1):
    m, d = x2.shape
    k0, k1 = m0.shape[1], m1.shape[1]
    return pl.pallas_call(
        _proj_out_body,
        grid=(m // OUT_TM,),
        in_specs=[
            pl.BlockSpec((OUT_TM, d), lambda i: (i, 0)),
            pl.BlockSpec((OUT_TM, k0), lambda i: (i, 0)),
            pl.BlockSpec((OUT_TM, k1), lambda i: (i, 0)),
            pl.BlockSpec((k0, d), lambda i: (0, 0)),
            pl.BlockSpec((k1, d), lambda i: (0, 0)),
        ],
        out_specs=pl.BlockSpec((OUT_TM, d), lambda i: (i, 0)),
        out_shape=jax.ShapeDtypeStruct((m, d), f32),
        compiler_params=_cparams("parallel"),
        name="proj_out",
    )(x2, m0, m1, w0, w1)


def _ret_body(q_ref, k_ref, v_ref, z_ref, cos_ref, sin_ref, dmask_ref, kdec_ref, qdec_ref, cdec_ref, g_ref, o_ref):
    c_len = RET_CHUNK

    def rot(t, cs, sn):
        return t * cs + pltpu.roll(t, HEAD_DIM // 2, 1) * sn

    def body(n, state):
        rows = pl.ds(pl.multiple_of(n * c_len, c_len), c_len)
        cs = cos_ref[rows, :]
        sn = sin_ref[rows, :]
        q = rot(q_ref[rows, :], cs, sn)
        k = rot(k_ref[rows, :], cs, sn) * SCALE
        vb = v_ref[rows, :].astype(bf16)
        s = _dot_nt(q.astype(bf16), k.astype(bf16)) * dmask_ref[...]
        o = _dot(s.astype(bf16), vb)
        o = o + _dot((q * qdec_ref[...]).astype(bf16), state.astype(bf16))
        kv = _dot_tn((k * kdec_ref[...]).astype(bf16), vb)
        state = cdec_ref[...] * state + kv
        y = _rms(o, g_ref[...])
        o_ref[rows, :] = (y * _silu(z_ref[rows, :])).astype(o_ref.dtype)
        return state

    lax.fori_loop(0, SEQ // c_len, body, jnp.zeros((HEAD_DIM, HEAD_DIM), f32), unroll=RET_UNROLL)


def _retention(h, ret_g):
    b = h.shape[0]
    c_len = RET_CHUNK
    half = HEAD_DIM // 2
    pos = jnp.arange(SEQ)
    inv = ROPE_BASE ** (-jnp.arange(half, dtype=f32) / half)
    ang = pos.astype(f32)[:, None] * inv[None, :]
    cos, sin = jnp.cos(ang), jnp.sin(ang)
    cos_t = jnp.concatenate([cos, cos], axis=-1)
    sin_t = jnp.concatenate([-sin, sin], axis=-1)
    log_g = jnp.log(1.0 - 2.0 ** (-5.0 - jnp.arange(RET_HEADS, dtype=f32)))
    idx = jnp.arange(c_len, dtype=f32)
    diff = idx[:, None] - idx[None, :]
    dmask = jnp.where(diff >= 0, jnp.exp(jnp.maximum(diff, 0.0)[None] * log_g[:, None, None]), 0.0)
    ones = jnp.ones((1, 1, HEAD_DIM), f32)
    k_dec = jnp.exp((c_len - 1 - idx)[None, :] * log_g[:, None])[:, :, None] * ones
    q_dec = jnp.exp((idx + 1.0)[None, :] * log_g[:, None])[:, :, None] * ones
    chunk_dec = jnp.exp(c_len * log_g)[:, None, None] * ones

    def col(off):
        return pl.BlockSpec((None, SEQ, HEAD_DIM), lambda bi, hi: (bi, 0, off + hi))

    table = pl.BlockSpec((SEQ, HEAD_DIM), lambda bi, hi: (0, 0))

    def per_head(rows):
        return pl.BlockSpec((None, rows, HEAD_DIM), lambda bi, hi: (hi, 0, 0))

    return pl.pallas_call(
        _ret_body,
        grid=(b, RET_HEADS),
        in_specs=[col(EV_RQ), col(EV_RK), col(EV_RV), col(EV_RZ), table, table,
                  per_head(c_len), per_head(c_len), per_head(c_len), per_head(1), per_head(1)],
        out_specs=pl.BlockSpec((None, SEQ, HEAD_DIM), lambda bi, hi: (bi, 0, hi)),
        out_shape=jax.ShapeDtypeStruct((b, SEQ, RET_W), bf16),
        compiler_params=_cparams("parallel", "parallel"),
        name="retention",
    )(h, h, h, h, cos_t, sin_t, dmask, k_dec, q_dec, chunk_dec, ret_g.reshape(RET_HEADS, 1, HEAD_DIM))


def _nsa_body(q_ref, kc_ref, vc_ref, ks_ref, vs_ref, kw_ref, vw_ref, nz_ref, ng_ref,
              qg_ref, kg_ref, posk_ref, posv_ref, w1k_ref, w2k_ref, w1v_ref, w2v_ref, ovl_ref, gsel_ref, eye_ref,
              o_ref, kcmp_s, vcmp_s, ksa_s, vsa_s, kwa_s, vwa_s, qn_s, not_sel_s):
    tq = NSA_TQ
    rep = NSA_REP
    half_blk = CMP_LEN // 2
    d = HEAD_DIM
    exp2_scale = SCALE * float(np.log2(np.e))

    def compress(t_ref, pos_ref, w1_ref, w2_ref):
        first = jnp.zeros((SEQ // CMP_STRIDE, d), f32)
        second = jnp.zeros((SEQ // CMP_STRIDE, d), f32)
        for i in range(half_blk):
            ti = t_ref[pl.ds(i, SEQ // CMP_STRIDE, stride=CMP_STRIDE), :]
            first = first + _dot((ti + pos_ref[i:i + 1, :]).astype(bf16), w1_ref[i * d:(i + 1) * d, :])
            j = half_blk + i
            second = second + _dot((ti + pos_ref[j:j + 1, :]).astype(bf16), w1_ref[j * d:(j + 1) * d, :])
        pre = first + pltpu.roll(second, SEQ // CMP_STRIDE - 1, 0)
        return _dot(jax.nn.gelu(pre).astype(bf16), w2_ref[...])

    kcmp_s[...] = _rms(compress(kc_ref, posk_ref, w1k_ref, w2k_ref), kg_ref[0:1, :]).astype(bf16)
    vcmp_s[...] = compress(vc_ref, posv_ref, w1v_ref, w2v_ref).astype(bf16)
    lane0 = lax.broadcasted_iota(jnp.int32, (tq, d), 1) == 0
    ones = jnp.ones((256, d), bf16)
    zeros = jnp.zeros((256, d), bf16)
    for c in range(NSA_WINDOW // 256):
        rows = slice(c * 256, (c + 1) * 256)
        kwa_s[rows, 0:d] = zeros
        kwa_s[rows, d:2 * d] = jnp.where(lax.broadcasted_iota(jnp.int32, (256, d), 1) == 0, NEG_INF, 0.0).astype(bf16)
        vwa_s[rows, 0:d] = zeros
        vwa_s[rows, d:2 * d] = zeros
    for c in range(SEQ // 256):
        rows = slice(c * 256, (c + 1) * 256)
        wrows = slice(NSA_WINDOW + c * 256, NSA_WINDOW + (c + 1) * 256)
        ksa_s[rows, 0:d] = _rms(ks_ref[rows, :], kg_ref[1:2, :]).astype(bf16)
        ksa_s[rows, d:2 * d] = gsel_ref[rows, :]
        vsa_s[rows, 0:d] = vs_ref[rows, :].astype(bf16)
        vsa_s[rows, d:2 * d] = ones
        kwa_s[wrows, 0:d] = _rms(kw_ref[rows, :], kg_ref[2:3, :]).astype(bf16)
        kwa_s[wrows, d:2 * d] = zeros
        vwa_s[wrows, 0:d] = vw_ref[rows, :].astype(bf16)
        vwa_s[wrows, d:2 * d] = ones

    row_i = lax.broadcasted_iota(jnp.int32, (tq, tq), 0)
    col_i = lax.broadcasted_iota(jnp.int32, (tq, tq), 1)
    band_first = jnp.where(col_i > row_i, 0.0, NEG_INF)
    band_last = jnp.where(col_i <= row_i, 0.0, NEG_INF)
    pad_flag = jnp.where(lane0, 1.0, 0.0).astype(bf16)

    def softmax_update(s, m, acc, v):
        n = s.shape[-1]
        m_new = jnp.maximum(m, jnp.max(s, axis=-1, keepdims=True))
        alpha = jnp.exp2((m - m_new) * exp2_scale)
        e = jnp.exp2((s - m_new) * exp2_scale)
        pv = _dot(e.reshape(rep * tq, n).astype(bf16), v).reshape(rep, tq, 2 * d)
        return m_new, alpha * acc + pv

    def select_tile(i, carry):
        t0 = i * tq
        rows = pl.ds(pl.multiple_of(t0, tq), tq)
        qs = [_rms(q_ref[rows, r * d:(r + 1) * d], qg_ref[...]).astype(bf16) for r in range(rep)]
        for r in range(rep):
            qn_s[r, rows, :] = qs[r]
        tpos = t0 + lax.broadcasted_iota(jnp.int32, (tq, 1), 0)
        qst = jnp.concatenate(qs, axis=0)
        n_idx = lax.broadcasted_iota(jnp.int32, (1, LANE), 1)
        valid = (n_idx * CMP_STRIDE + (CMP_LEN - 1) <= tpos) & (n_idx < N_CMP)
        cbias = jnp.where(valid, 0.0, NEG_INF)
        sc = (_dot_nt(qst, kcmp_s[...]) * SCALE).reshape(rep, tq, LANE) + cbias[None]
        mc = jnp.max(sc, axis=-1, keepdims=True)
        ec = jnp.exp(sc - mc)
        any_valid = jnp.where(tpos >= CMP_LEN - 1, 1.0, 0.0)
        pc = ec / jnp.sum(ec, axis=-1, keepdims=True) * any_valid[None]
        o_cmp = _dot(pc.reshape(rep * tq, LANE).astype(bf16), vcmp_s[...]).reshape(rep, tq, d)
        gates = jax.nn.sigmoid(ng_ref[rows, :])
        for r in range(rep):
            o_ref[rows, r * d:(r + 1) * d] = (gates[:, r:r + 1] * o_cmp[r]).astype(o_ref.dtype)
        psum = pc[0]
        for r in range(1, rep):
            psum = psum + pc[r]

        imp = lax.dot_general(ovl_ref[...], psum, (((1,), (1,)), ((), ())),
                              preferred_element_type=f32, precision=lax.Precision.HIGHEST)[:N_SEL, :]
        jj = lax.broadcasted_iota(jnp.int32, (N_SEL, 1), 0)
        cur = (t0 + lax.broadcasted_iota(jnp.int32, (1, tq), 1)) // SEL_BLOCK
        forced = (jj == 0) | (jj == cur) | (jj == cur - 1)
        imp = jnp.where(forced, -NEG_INF, jnp.where(jj > cur, NEG_INF, imp))
        rank = jnp.zeros((N_SEL, tq), f32)
        for j2 in range(N_SEL):
            row = imp[j2:j2 + 1, :]
            beats = (row > imp) | ((row == imp) & (j2 < jj))
            rank = rank + jnp.where(beats, 1.0, 0.0)
        not_sel_t = jnp.where(rank < SEL_TOPK, 0.0, 1.0).astype(bf16)
        not_sel_s[rows, :] = _dot_tn(not_sel_t, eye_ref[...]).astype(bf16)
        return carry

    lax.fori_loop(0, SEQ // tq, select_tile, 0, unroll=NSA_SELECT_UNROLL)

    def tile(i, carry):
        t0 = i * tq
        rows = pl.ds(pl.multiple_of(t0, tq), tq)
        qs = [qn_s[r, rows, :] for r in range(rep)]
        tpos = t0 + lax.broadcasted_iota(jnp.int32, (tq, 1), 0)
        m_init = jnp.full((rep, tq, 1), NEG_INF, f32)
        acc_init = jnp.zeros((rep, tq, 2 * d), f32)

        qw = jnp.concatenate([jnp.concatenate([q, pad_flag], axis=1) for q in qs], axis=0)
        wrows = pl.ds(pl.multiple_of(t0, tq), WIN_SPAN)
        sw = _dot_nt(qw, kwa_s[wrows, :]).reshape(rep, tq, WIN_SPAN)
        sw = jnp.concatenate([sw[:, :, :tq] + band_first[None], sw[:, :, tq:NSA_WINDOW],
                              sw[:, :, NSA_WINDOW:] + band_last[None]], axis=2)
        _, acc_w = softmax_update(sw, m_init, acc_init, vwa_s[wrows, :])
        o_win = acc_w[:, :, :d] / acc_w[:, :, d:]

        not_sel = not_sel_s[rows, :]
        qsel = jnp.concatenate([jnp.concatenate([q, not_sel], axis=1) for q in qs], axis=0)

        def scores(kt):
            return _dot_nt(qsel, ksa_s[pl.ds(pl.multiple_of(kt * SEL_TK, SEL_TK), SEL_TK), :]).reshape(rep, tq, SEL_TK)

        def full_tile(kt, st):
            s, m, acc = st
            s_next = scores(kt + 1)
            m, acc = softmax_update(s, m, acc, vsa_s[pl.ds(pl.multiple_of(kt * SEL_TK, SEL_TK), SEL_TK), :])
            return s_next, m, acc

        n_full = t0 // SEL_TK
        s_tail, m_s, acc_s = lax.fori_loop(0, n_full, full_tile, (scores(0), m_init, acc_init))
        k0 = pl.multiple_of(n_full * SEL_TK, SEL_TK)
        causal = jnp.where(k0 + lax.broadcasted_iota(jnp.int32, (1, SEL_TK), 1) <= tpos, 0.0, NEG_INF)
        _, acc_s = softmax_update(s_tail + causal[None], m_s, acc_s, vsa_s[pl.ds(k0, SEL_TK), :])
        o_sel = acc_s[:, :, :d] / acc_s[:, :, d:]

        gates = jax.nn.sigmoid(ng_ref[rows, :])
        for r in range(rep):
            g_s = gates[:, rep + r:rep + r + 1]
            g_w = gates[:, 2 * rep + r:2 * rep + r + 1]
            cols = slice(r * d, (r + 1) * d)
            o = o_ref[rows, cols].astype(f32) + g_s * o_sel[r] + g_w * o_win[r]
            o_ref[rows, cols] = (o * _silu(nz_ref[rows, cols])).astype(o_ref.dtype)
        return carry

    lax.fori_loop(0, SEQ // tq, tile, 0)


def _nsa(h, q_g, k_g, pos_k, pos_v, k_w1, k_w2, v_w1, v_w2):
    b = h.shape[0]
    grp_w = NSA_REP * HEAD_DIM
    cs = np.arange(LANE) * CMP_STRIDE
    ss = np.arange(LANE) * SEL_BLOCK
    ovl = ((cs[None, :] < ss[:, None] + SEL_BLOCK) & (cs[None, :] + CMP_LEN > ss[:, None])
           & (np.arange(LANE)[None, :] < N_CMP) & (np.arange(LANE)[:, None] < N_SEL))
    ovl_t = jnp.asarray(ovl.astype(np.float32))
    key_blk = np.arange(SEQ) // SEL_BLOCK
    gsel = jnp.asarray(np.where(key_blk[:, None] == np.arange(LANE)[None, :], NEG_INF, 0.0), dtype=bf16)
    eye = jnp.asarray(np.eye(N_SEL, LANE), dtype=bf16)

    def wide(off):
        return pl.BlockSpec((None, SEQ, grp_w), lambda bi, gi: (bi, 0, off // NSA_REP + gi))

    def col(off):
        return pl.BlockSpec((None, SEQ, HEAD_DIM), lambda bi, gi: (bi, 0, off + gi))

    def full(shape):
        return pl.BlockSpec(shape, lambda bi, gi: (0,) * len(shape))

    flat = CMP_LEN * HEAD_DIM
    return pl.pallas_call(
        _nsa_body,
        grid=(b, NSA_KV_HEADS),
        in_specs=[wide(EV_NQ), col(EV_KC), col(EV_VC), col(EV_KS), col(EV_VS), col(EV_KW), col(EV_VW),
                  wide(EV_NZ), col(EV_NG),
                  full((1, HEAD_DIM)), full((NSA_N_BRANCH, HEAD_DIM)), full((CMP_LEN, HEAD_DIM)), full((CMP_LEN, HEAD_DIM)),
                  full((flat, HEAD_DIM)), full((HEAD_DIM, HEAD_DIM)), full((flat, HEAD_DIM)), full((HEAD_DIM, HEAD_DIM)),
                  full((LANE, LANE)), full((SEQ, LANE)), full((N_SEL, LANE))],
        out_specs=pl.BlockSpec((None, SEQ, grp_w), lambda bi, gi: (bi, 0, gi)),
        out_shape=jax.ShapeDtypeStruct((b, SEQ, NSA_HEADS * HEAD_DIM), bf16),
        scratch_shapes=[pltpu.VMEM((SEQ // CMP_STRIDE, HEAD_DIM), bf16), pltpu.VMEM((SEQ // CMP_STRIDE, HEAD_DIM), bf16),
                        pltpu.VMEM((SEQ, 2 * HEAD_DIM), bf16), pltpu.VMEM((SEQ, 2 * HEAD_DIM), bf16),
                        pltpu.VMEM((SEQ + NSA_WINDOW, 2 * HEAD_DIM), bf16), pltpu.VMEM((SEQ + NSA_WINDOW, 2 * HEAD_DIM), bf16),
                        pltpu.VMEM((NSA_REP, SEQ, HEAD_DIM), bf16), pltpu.VMEM((SEQ, LANE), bf16)],
        compiler_params=_cparams("parallel", "parallel"),
        name="nsa",
    )(h, h, h, h, h, h, h, h, h,
      q_g.reshape(1, HEAD_DIM), k_g, pos_k, pos_v,
      k_w1.astype(bf16), k_w2.astype(bf16), v_w1.astype(bf16), v_w2.astype(bf16), ovl_t, gsel, eye)


def _conv_body(u_ref, g_ref, z_ref, w_ref, b_ref, lg_ref, lb_ref, o_ref, a_ext, a_sh, y_s):
    ext = CONV_HALO + CONV_T
    n_cb = CONV_W // LANE
    lead = CONV_HALO - (CONV_K - 1)

    @pl.when(pl.program_id(1) == 0)
    def _():
        for cb in range(n_cb):
            a_ext[cb, 0:CONV_HALO, :] = jnp.zeros((CONV_HALO, LANE), f32)
            a_ext[cb, ext:ext + SUBLANE, :] = jnp.zeros((SUBLANE, LANE), f32)

    for cb in range(n_cb):
        lanes = slice(cb * LANE, (cb + 1) * LANE)
        for c in range(CONV_T // 128):
            rows = slice(c * 128, (c + 1) * 128)
            a_ext[cb, CONV_HALO + c * 128:CONV_HALO + (c + 1) * 128, :] = u_ref[rows, lanes] * jax.nn.sigmoid(g_ref[rows, lanes])
        for r in range(1, SUBLANE):
            for c in range(ext // CONV_HALO):
                a_sh[r - 1, cb, c * CONV_HALO:(c + 1) * CONV_HALO, :] = a_ext[cb, c * CONV_HALO + r:(c + 1) * CONV_HALO + r, :]

        taps_w = [jnp.broadcast_to(w_ref[k:k + 1, lanes], (CONV_RB, LANE)) for k in range(CONV_K)]
        bias = jnp.broadcast_to(b_ref[:, lanes], (CONV_RB, LANE))

        def conv_rows(rb, carry, cb=cb, taps_w=taps_w, bias=bias):
            r0 = pl.multiple_of(rb * CONV_RB, CONV_RB)
            parts = [bias, None]
            for k in range(CONV_K):
                shift, base = (lead + k) % SUBLANE, (lead + k) // SUBLANE * SUBLANE
                rows = pl.ds(r0 + base, CONV_RB)
                tap = a_ext[cb, rows, :] if shift == 0 else a_sh[shift - 1, cb, rows, :]
                term = tap * taps_w[k]
                parts[k % 2] = term if parts[k % 2] is None else parts[k % 2] + term
            y_s[cb, pl.ds(r0, CONV_RB), :] = parts[0] + parts[1]
            return carry

        lax.fori_loop(0, CONV_T // CONV_RB, conv_rows, 0)
        a_ext[cb, 0:CONV_HALO, :] = a_ext[cb, CONV_T:CONV_T + CONV_HALO, :]

    def norm_rows(rb, carry):
        rows = pl.ds(pl.multiple_of(rb * CONV_RB, CONV_RB), CONV_RB)
        y = jnp.concatenate([y_s[cb, rows, :] for cb in range(n_cb)], axis=1)
        mu = jnp.mean(y, axis=-1, keepdims=True)
        var = jnp.mean(jnp.square(y - mu), axis=-1, keepdims=True)
        yn = (y - mu) * lax.rsqrt(var + NORM_EPS) * lg_ref[...] + lb_ref[...]
        o_ref[rows, :] = (_silu(yn) * _silu(z_ref[rows, :])).astype(o_ref.dtype)
        return carry

    lax.fori_loop(0, CONV_T // CONV_RB, norm_rows, 0, unroll=2)


def _conv(h, dw_w, dw_b, ln_g, ln_b):
    b = h.shape[0]

    def col(off):
        return pl.BlockSpec((None, CONV_T, CONV_W), lambda bi, ti: (bi, ti, off * LANE // CONV_W))

    def full(rows):
        return pl.BlockSpec((rows, CONV_W), lambda bi, ti: (0, 0))

    return pl.pallas_call(
        _conv_body,
        grid=(b, SEQ // CONV_T),
        in_specs=[col(OD_CU), col(OD_CG), col(OD_CZ), full(CONV_K), full(1), full(1), full(1)],
        out_specs=pl.BlockSpec((None, CONV_T, CONV_W), lambda bi, ti: (bi, ti, 0)),
        out_shape=jax.ShapeDtypeStruct((b, SEQ, CONV_W), bf16),
        scratch_shapes=[pltpu.VMEM((CONV_W // LANE, CONV_HALO + CONV_T + SUBLANE, LANE), f32),
                        pltpu.VMEM((SUBLANE - 1, CONV_W // LANE, CONV_HALO + CONV_T, LANE), f32),
                        pltpu.VMEM((CONV_W // LANE, CONV_T, LANE), f32)],
        compiler_params=_cparams("parallel", "arbitrary"),
        name="conformer_conv",
    )(h, h, h, dw_w, dw_b.reshape(1, CONV_W), ln_g.reshape(1, CONV_W), ln_b.reshape(1, CONV_W))


def _dil_body(q0_ref, k0_ref, v0_ref, q1_ref, k1_ref, v1_ref, q2_ref, k2_ref, v2_ref, z_ref, qg_ref, kg_ref,
              o_ref, qn_s, kn_s, va_s, o0_s, o1_s, o2_s, l0_s, l1_s, l2_s):
    blk = DIL_BLOCK
    n_tiles = SEQ // blk
    groups = ((q0_ref, k0_ref, v0_ref, o0_s, l0_s), (q1_ref, k1_ref, v1_ref, o1_s, l1_s), (q2_ref, k2_ref, v2_ref, o2_s, l2_s))
    kn_s[0:blk, :] = jnp.zeros((blk, HEAD_DIM), bf16)
    va_s[0:blk, :] = jnp.zeros((blk, 2 * HEAD_DIM), bf16)
    ones = jnp.ones((blk, HEAD_DIM), bf16)
    row2 = lax.broadcasted_iota(jnp.int32, (blk, 2 * blk), 0)
    col2 = lax.broadcasted_iota(jnp.int32, (blk, 2 * blk), 1)
    band_prev = jnp.where((col2 >= row2) & (col2 <= row2 + blk), 0.0, NEG_INF)
    band_first = jnp.where((col2 >= blk) & (col2 <= row2 + blk), 0.0, NEG_INF)
    band_cur = band_first[:, blk:]

    for (q_ref, k_ref, v_ref, o_s, l_s), (window, dil) in zip(groups, DIL_PATTERNS):
        assert window // dil == blk
        nb = SEQ // dil // blk

        def natural_rows(u, dil=dil, nb=nb):
            start = u // nb + (u % nb) * (blk * dil)
            return pl.ds(start, blk, stride=dil) if dil > 1 else pl.ds(pl.multiple_of(start, blk), blk)

        def prep(u, carry, q_ref=q_ref, k_ref=k_ref, v_ref=v_ref, natural_rows=natural_rows):
            src = natural_rows(u)
            qn_s[pl.ds(pl.multiple_of(u * blk, blk), blk), :] = _rms(q_ref[src, :], qg_ref[...]).astype(bf16)
            dst = pl.ds(pl.multiple_of((u + 1) * blk, blk), blk)
            kn_s[dst, :] = _rms(k_ref[src, :], kg_ref[...]).astype(bf16)
            va_s[dst, :] = jnp.concatenate([v_ref[src, :].astype(bf16), ones], axis=1)
            return carry

        lax.fori_loop(0, n_tiles, prep, 0, unroll=DIL_UNROLL)

        def att(u, carry, o_s=o_s, l_s=l_s, nb=nb, natural_rows=natural_rows):
            q = qn_s[pl.ds(pl.multiple_of(u * blk, blk), blk), :]
            if nb == 1:
                keys = pl.ds(pl.multiple_of((u + 1) * blk, blk), blk)
                band = band_cur
            else:
                keys = pl.ds(pl.multiple_of(u * blk, blk), 2 * blk)
                band = jnp.where((u % nb) > 0, band_prev, band_first)
            s = _dot_nt(q, kn_s[keys, :]) * SCALE + band
            m = jnp.max(s, axis=-1, keepdims=True)
            pv = _dot(jnp.exp(s - m).astype(bf16), va_s[keys, :])
            den = pv[:, HEAD_DIM:]
            dst = natural_rows(u)
            o_s[dst, :] = pv[:, :HEAD_DIM] / den
            l_s[dst, :] = m + jnp.log(den)
            return carry

        lax.fori_loop(0, n_tiles, att, 0, unroll=DIL_UNROLL)

    def mix(c, carry):
        rows = pl.ds(pl.multiple_of(c * blk, blk), blk)
        l0, l1, l2 = l0_s[rows, :], l1_s[rows, :], l2_s[rows, :]
        m = jnp.maximum(jnp.maximum(l0, l1), l2)
        w0, w1, w2 = jnp.exp(l0 - m), jnp.exp(l1 - m), jnp.exp(l2 - m)
        o = (w0 * o0_s[rows, :] + w1 * o1_s[rows, :] + w2 * o2_s[rows, :]) / (w0 + w1 + w2)
        o_ref[rows, :] = (o * _silu(z_ref[rows, :])).astype(o_ref.dtype)
        return carry

    lax.fori_loop(0, n_tiles, mix, 0, unroll=DIL_UNROLL)


def _dilated(h, q_g, k_g):
    b = h.shape[0]

    def col(off):
        return pl.BlockSpec((None, SEQ, HEAD_DIM), lambda bi, hi: (bi, 0, off + hi))

    gain = pl.BlockSpec((1, HEAD_DIM), lambda bi, hi: (0, 0))
    specs = []
    for gi in range(len(DIL_PATTERNS)):
        specs += [col(OD_DQ + gi * DIL_GROUP_HEADS), col(OD_DK + gi * DIL_GROUP_HEADS), col(OD_DV + gi * DIL_GROUP_HEADS)]
    return pl.pallas_call(
        _dil_body,
        grid=(b, DIL_GROUP_HEADS),
        in_specs=specs + [col(OD_DZ), gain, gain],
        out_specs=pl.BlockSpec((None, SEQ, HEAD_DIM), lambda bi, hi: (bi, 0, hi)),
        out_shape=jax.ShapeDtypeStruct((b, SEQ, DIL_GROUP_HEADS * HEAD_DIM), bf16),
        scratch_shapes=[pltpu.VMEM((SEQ, HEAD_DIM), bf16), pltpu.VMEM((SEQ + DIL_BLOCK, HEAD_DIM), bf16),
                        pltpu.VMEM((SEQ + DIL_BLOCK, 2 * HEAD_DIM), bf16)] + [pltpu.VMEM((SEQ, HEAD_DIM), f32)] * 6,
        compiler_params=_cparams("parallel", "parallel"),
        name="dilated",
    )(*([h] * 10), q_g.reshape(1, HEAD_DIM), k_g.reshape(1, HEAD_DIM))


def _even_w_in(w):
    ng0 = EV_NZ * LANE
    nz0 = ng0 + NSA_N_BRANCH * NSA_HEADS
    ng = w[:, ng0:nz0].reshape(-1, NSA_N_BRANCH, NSA_KV_HEADS, NSA_REP)
    ng = ng.transpose(0, 2, 1, 3).reshape(-1, NSA_KV_HEADS, NSA_N_BRANCH * NSA_REP)
    ng = jnp.pad(ng, ((0, 0), (0, 0), (0, LANE - NSA_N_BRANCH * NSA_REP))).reshape(-1, NSA_KV_HEADS * LANE)
    out = jnp.concatenate([w[:, :ng0], w[:, nz0:], ng], axis=1)
    return jnp.pad(out, ((0, 0), (0, EV_COLS - out.shape[1]))).astype(bf16)


def kernel(x, ev_norm, ev_w_in, ev_w_out, ev_ret_norm, ev_nsa_q_norm, ev_nsa_k_norm, ev_cmp_pos_k, ev_cmp_pos_v, ev_cmp_k_w1, ev_cmp_k_w2, ev_cmp_v_w1, ev_cmp_v_w2, od_norm, od_w_in, od_w_out, od_dw_w, od_dw_b, od_conv_norm_g, od_conv_norm_b, od_dil_q_norm, od_dil_k_norm):
    b, s, d = x.shape
    assert (s, d) == (SEQ, D_MODEL)
    x0 = x.reshape(b * s, d)

    h0 = _proj_in(x0, ev_norm[0], _even_w_in(ev_w_in[0])).reshape(b, s, EV_COLS)
    a_out = _retention(h0, ev_ret_norm[0])
    b_out = _nsa(h0, ev_nsa_q_norm[0], ev_nsa_k_norm[0], ev_cmp_pos_k[0], ev_cmp_pos_v[0],
                 ev_cmp_k_w1[0], ev_cmp_k_w2[0], ev_cmp_v_w1[0], ev_cmp_v_w2[0])
    wo = ev_w_out[0].astype(bf16)
    x1 = _proj_out(x0, a_out.reshape(b * s, -1), b_out.reshape(b * s, -1), wo[:RET_W], wo[RET_W:])

    h1 = _proj_in(x1, od_norm[0], od_w_in[0].astype(bf16)).reshape(b, s, OD_COLS)
    c_out = _conv(h1, od_dw_w[0], od_dw_b[0], od_conv_norm_g[0], od_conv_norm_b[0])
    d_out = _dilated(h1, od_dil_q_norm[0], od_dil_k_norm[0])
    wo = od_w_out[0].astype(bf16)
    x2 = _proj_out(x1, c_out.reshape(b * s, -1), d_out.reshape(b * s, -1), wo[:CONV_W], wo[CONV_W:])
    return x2.reshape(b, s, d)
```

```python
import jax
import jax.numpy as jnp
import numpy as np
from jax import lax
from jax.experimental import pallas as pl
from jax.experimental.pallas import tpu as pltpu

f32 = jnp.float32
bf16 = jnp.bfloat16

D_MODEL = 2048
SEQ = 2048
HEAD_DIM = 128
SCALE = HEAD_DIM ** -0.5
NORM_EPS = 1e-6
NEG_INF = -1e30
LANE = 128
SUBLANE = 8

RET_HEADS = 8
RET_W = RET_HEADS * HEAD_DIM
RET_CHUNK = 256
RET_UNROLL = 8
ROPE_BASE = 10000.0
NSA_HEADS = 8
NSA_KV_HEADS = 2
NSA_REP = NSA_HEADS // NSA_KV_HEADS
NSA_N_BRANCH = 3
CMP_LEN = 32
CMP_STRIDE = 16
N_CMP = (SEQ - CMP_LEN) // CMP_STRIDE + 1
SEL_BLOCK = 64
N_SEL = SEQ // SEL_BLOCK
SEL_TOPK = 16
NSA_WINDOW = 512
NSA_TQ = 128
SEL_TK = 512
NSA_SELECT_TQ = 512
WIN_SPAN = NSA_WINDOW + NSA_TQ
CONV_W = 1024
CONV_K = 31
CONV_T = 512
CONV_HALO = 32
CONV_RB = 64
DIL_PATTERNS = ((128, 1), (512, 4), (2048, 16))
DIL_GROUP_HEADS = 4
DIL_HEADS = DIL_GROUP_HEADS * len(DIL_PATTERNS)
DIL_BLOCK = 128
DIL_UNROLL = 8

EV_RQ, EV_RK, EV_RV, EV_RZ = 0, 8, 16, 24
EV_NQ = 32
EV_KC, EV_VC, EV_KS, EV_VS, EV_KW, EV_VW = 40, 42, 44, 46, 48, 50
EV_NZ = 52
EV_NG = 60
EV_COLS = 8192
OD_CU, OD_CG, OD_CZ = 0, 8, 16
OD_DQ, OD_DK, OD_DV = 24, 36, 48
OD_DZ = 60
OD_COLS = 8192

VMEM_LIMIT = 56 * 1024 * 1024


def _cparams(*sem):
    return pltpu.CompilerParams(dimension_semantics=sem, vmem_limit_bytes=VMEM_LIMIT)


def _rms(x, gain):
    ms = jnp.mean(x * x, axis=-1, keepdims=True)
    return x * lax.rsqrt(ms + NORM_EPS) * gain


def _silu(z):
    return z * jax.nn.sigmoid(z)


def _dot(a, b):
    return jnp.dot(a, b, preferred_element_type=f32)


def _dot_nt(a, b):
    return lax.dot_general(a, b, (((1,), (1,)), ((), ())), preferred_element_type=f32)


def _dot_tn(a, b):
    return lax.dot_general(a, b, (((0,), (0,)), ((), ())), preferred_element_type=f32)


PROJ_TM = 1024
PROJ_TN = 1024
NORM_ROWS = 256


def _proj_in_body(x_ref, g_ref, w_ref, o_ref, xn_ref):
    @pl.when(pl.program_id(1) == 0)
    def _():
        for c in range(PROJ_TM // NORM_ROWS):
            rows = slice(c * NORM_ROWS, (c + 1) * NORM_ROWS)
            xn_ref[rows, :] = _rms(x_ref[rows, :], g_ref[...]).astype(bf16)

    o_ref[...] = _dot(xn_ref[...], w_ref[...])


def _proj_in(x2, gain, w):
    m, d = x2.shape
    n = w.shape[1]
    return pl.pallas_call(
        _proj_in_body,
        grid=(m // PROJ_TM, n // PROJ_TN),
        in_specs=[
            pl.BlockSpec((PROJ_TM, d), lambda i, j: (i, 0)),
            pl.BlockSpec((1, d), lambda i, j: (0, 0)),
            pl.BlockSpec((d, PROJ_TN), lambda i, j: (0, j)),
        ],
        out_specs=pl.BlockSpec((PROJ_TM, PROJ_TN), lambda i, j: (i, j)),
        out_shape=jax.ShapeDtypeStruct((m, n), f32),
        scratch_shapes=[pltpu.VMEM((PROJ_TM, d), bf16)],
        compiler_params=_cparams("parallel", "arbitrary"),
        name="proj_in",
    )(x2, gain.reshape(1, d), w)


OUT_TM = 512


def _proj_out_body(x_ref, m0_ref, m1_ref, w0_ref, w1_ref, o_ref):
    o_ref[...] = x_ref[...] + _dot(m0_ref[...], w0_ref[...]) + _dot(m1_ref[...], w1_ref[...])


def _proj_out(x2, m0, m1, w0, w1):
    m, d = x2.shape
    k0, k1 = m0.shape[1], m1.shape[1]
    return pl.pallas_call(
        _proj_out_body,
        grid=(m // OUT_TM,),
        in_specs=[
            pl.BlockSpec((OUT_TM, d), lambda i: (i, 0)),
            pl.BlockSpec((OUT_TM, k0), lambda i: (i, 0)),
            pl.BlockSpec((OUT_TM, k1), lambda i: (i, 0)),
            pl.BlockSpec((k0, d), lambda i: (0, 0)),
            pl.BlockSpec((k1, d), lambda i: (0, 0)),
        ],
        out_specs=pl.BlockSpec((OUT_TM, d), lambda i: (i, 0)),
        out_shape=jax.ShapeDtypeStruct((m, d), f32),
        compiler_params=_cparams("parallel"),
        name="proj_out",
    )(x2, m0, m1, w0, w1)


def _ret_body(q_ref, k_ref, v_ref, z_ref, cos_ref, sin_ref, dmask_ref, kdec_ref, qdec_ref, cdec_ref, g_ref, o_ref):
    c_len = RET_CHUNK

    def rot(t, cs, sn):
        return t * cs + pltpu.roll(t, HEAD_DIM // 2, 1) * sn

    def body(n, state):
        rows = pl.ds(pl.multiple_of(n * c_len, c_len), c_len)
        cs = cos_ref[rows, :]
        sn = sin_ref[rows, :]
        q = rot(q_ref[rows, :], cs, sn)
        k = rot(k_ref[rows, :], cs, sn) * SCALE
        vb = v_ref[rows, :].astype(bf16)
        s = _dot_nt(q.astype(bf16), k.astype(bf16)) * dmask_ref[...]
        o = _dot(s.astype(bf16), vb)
        o = o + _dot((q * qdec_ref[...]).astype(bf16), state.astype(bf16))
        kv = _dot_tn((k * kdec_ref[...]).astype(bf16), vb)
        state = cdec_ref[...] * state + kv
        y = _rms(o, g_ref[...])
        o_ref[rows, :] = (y * _silu(z_ref[rows, :])).astype(o_ref.dtype)
        return state

    lax.fori_loop(0, SEQ // c_len, body, jnp.zeros((HEAD_DIM, HEAD_DIM), f32), unroll=RET_UNROLL)


def _retention(h, ret_g):
    b = h.shape[0]
    c_len = RET_CHUNK
    half = HEAD_DIM // 2
    pos = jnp.arange(SEQ)
    inv = ROPE_BASE ** (-jnp.arange(half, dtype=f32) / half)
    ang = pos.astype(f32)[:, None] * inv[None, :]
    cos, sin = jnp.cos(ang), jnp.sin(ang)
    cos_t = jnp.concatenate([cos, cos], axis=-1)
    sin_t = jnp.concatenate([-sin, sin], axis=-1)
    log_g = jnp.log(1.0 - 2.0 ** (-5.0 - jnp.arange(RET_HEADS, dtype=f32)))
    idx = jnp.arange(c_len, dtype=f32)
    diff = idx[:, None] - idx[None, :]
    dmask = jnp.where(diff >= 0, jnp.exp(jnp.maximum(diff, 0.0)[None] * log_g[:, None, None]), 0.0)
    ones = jnp.ones((1, 1, HEAD_DIM), f32)
    k_dec = jnp.exp((c_len - 1 - idx)[None, :] * log_g[:, None])[:, :, None] * ones
    q_dec = jnp.exp((idx + 1.0)[None, :] * log_g[:, None])[:, :, None] * ones
    chunk_dec = jnp.exp(c_len * log_g)[:, None, None] * ones

    def col(off):
        return pl.BlockSpec((None, SEQ, HEAD_DIM), lambda bi, hi: (bi, 0, off + hi))

    table = pl.BlockSpec((SEQ, HEAD_DIM), lambda bi, hi: (0, 0))

    def per_head(rows, cols=HEAD_DIM):
        return pl.BlockSpec((None, rows, cols), lambda bi, hi: (hi, 0, 0))

    return pl.pallas_call(
        _ret_body,
        grid=(b, RET_HEADS),
        in_specs=[col(EV_RQ), col(EV_RK), col(EV_RV), col(EV_RZ), table, table,
                  per_head(c_len, c_len), per_head(c_len), per_head(c_len), per_head(1), per_head(1)],
        out_specs=pl.BlockSpec((None, SEQ, HEAD_DIM), lambda bi, hi: (bi, 0, hi)),
        out_shape=jax.ShapeDtypeStruct((b, SEQ, RET_W), bf16),
        compiler_params=_cparams("parallel", "parallel"),
        name="retention",
    )(h, h, h, h, cos_t, sin_t, dmask, k_dec, q_dec, chunk_dec, ret_g.reshape(RET_HEADS, 1, HEAD_DIM))


def _nsa_body(q_ref, kc_ref, vc_ref, ks_ref, vs_ref, kw_ref, vw_ref, nz_ref, ng_ref,
              qg_ref, kg_ref, posk_ref, posv_ref, w1k_ref, w2k_ref, w1v_ref, w2v_ref, ovl_ref, gsel_ref, eye_ref,
              o_ref, kcmp_s, vcmp_s, ksa_s, vsa_s, kwa_s, vwa_s, qn_s, not_sel_s):
    tq = NSA_TQ
    sq = NSA_SELECT_TQ
    rep = NSA_REP
    half_blk = CMP_LEN // 2
    d = HEAD_DIM
    exp2_scale = SCALE * float(np.log2(np.e))

    def compress(t_ref, pos_ref, w1_ref, w2_ref):
        first = jnp.zeros((SEQ // CMP_STRIDE, d), f32)
        second = jnp.zeros((SEQ // CMP_STRIDE, d), f32)
        for i in range(half_blk):
            ti = t_ref[pl.ds(i, SEQ // CMP_STRIDE, stride=CMP_STRIDE), :]
            first = first + _dot((ti + pos_ref[i:i + 1, :]).astype(bf16), w1_ref[i * d:(i + 1) * d, :])
            j = half_blk + i
            second = second + _dot((ti + pos_ref[j:j + 1, :]).astype(bf16), w1_ref[j * d:(j + 1) * d, :])
        pre = first + pltpu.roll(second, SEQ // CMP_STRIDE - 1, 0)
        return _dot(jax.nn.gelu(pre).astype(bf16), w2_ref[...])

    kcmp_s[...] = _rms(compress(kc_ref, posk_ref, w1k_ref, w2k_ref), kg_ref[0:1, :]).astype(bf16)
    vcmp_s[...] = compress(vc_ref, posv_ref, w1v_ref, w2v_ref).astype(bf16)
    lane0 = lax.broadcasted_iota(jnp.int32, (tq, d), 1) == 0
    ones = jnp.ones((256, d), bf16)
    zeros = jnp.zeros((256, d), bf16)
    for c in range(NSA_WINDOW // 256):
        rows = slice(c * 256, (c + 1) * 256)
        kwa_s[rows, 0:d] = zeros
        kwa_s[rows, d:2 * d] = jnp.where(lax.broadcasted_iota(jnp.int32, (256, d), 1) == 0, NEG_INF, 0.0).astype(bf16)
        vwa_s[rows, 0:d] = zeros
        vwa_s[rows, d:2 * d] = zeros
    for c in range(SEQ // 256):
        rows = slice(c * 256, (c + 1) * 256)
        wrows = slice(NSA_WINDOW + c * 256, NSA_WINDOW + (c + 1) * 256)
        ksa_s[rows, 0:d] = _rms(ks_ref[rows, :], kg_ref[1:2, :]).astype(bf16)
        ksa_s[rows, d:2 * d] = gsel_ref[rows, :]
        vsa_s[rows, 0:d] = vs_ref[rows, :].astype(bf16)
        vsa_s[rows, d:2 * d] = ones
        kwa_s[wrows, 0:d] = _rms(kw_ref[rows, :], kg_ref[2:3, :]).astype(bf16)
        kwa_s[wrows, d:2 * d] = zeros
        vwa_s[wrows, 0:d] = vw_ref[rows, :].astype(bf16)
        vwa_s[wrows, d:2 * d] = ones

    row_i = lax.broadcasted_iota(jnp.int32, (tq, tq), 0)
    col_i = lax.broadcasted_iota(jnp.int32, (tq, tq), 1)
    band_first = jnp.where(col_i > row_i, 0.0, NEG_INF)
    band_last = jnp.where(col_i <= row_i, 0.0, NEG_INF)
    pad_flag = jnp.where(lane0, 1.0, 0.0).astype(bf16)

    def softmax_update(s, m, acc, v):
        n = s.shape[-1]
        m_new = jnp.maximum(m, jnp.max(s, axis=-1, keepdims=True))
        alpha = jnp.exp2((m - m_new) * exp2_scale)
        e = jnp.exp2((s - m_new) * exp2_scale)
        pv = _dot(e.reshape(rep * tq, n).astype(bf16), v).reshape(rep, tq, 2 * d)
        return m_new, alpha * acc + pv

    def select_tile(i, carry):
        t0 = i * sq
        rows = pl.ds(pl.multiple_of(t0, sq), sq)
        qs = [_rms(q_ref[rows, r * d:(r + 1) * d], qg_ref[...]).astype(bf16) for r in range(rep)]
        for r in range(rep):
            qn_s[r, rows, :] = qs[r]
        tpos = t0 + lax.broadcasted_iota(jnp.int32, (sq, 1), 0)
        qst = jnp.concatenate(qs, axis=0)
        n_idx = lax.broadcasted_iota(jnp.int32, (1, LANE), 1)
        valid = (n_idx * CMP_STRIDE + (CMP_LEN - 1) <= tpos) & (n_idx < N_CMP)
        cbias = jnp.where(valid, 0.0, NEG_INF)
        sc = (_dot_nt(qst, kcmp_s[...]) * SCALE).reshape(rep, sq, LANE) + cbias[None]
        mc = jnp.max(sc, axis=-1, keepdims=True)
        ec = jnp.exp(sc - mc)
        any_valid = jnp.where(tpos >= CMP_LEN - 1, 1.0, 0.0)
        pc = ec / jnp.sum(ec, axis=-1, keepdims=True) * any_valid[None]
        o_cmp = _dot(pc.reshape(rep * sq, LANE).astype(bf16), vcmp_s[...]).reshape(rep, sq, d)
        gates = jax.nn.sigmoid(ng_ref[rows, :])
        for r in range(rep):
            o_ref[rows, r * d:(r + 1) * d] = (gates[:, r:r + 1] * o_cmp[r]).astype(o_ref.dtype)
        psum = pc[0]
        for r in range(1, rep):
            psum = psum + pc[r]

        imp = lax.dot_general(ovl_ref[...], psum, (((1,), (1,)), ((), ())),
                              preferred_element_type=f32, precision=lax.Precision.HIGHEST)[:N_SEL, :]
        jj = lax.broadcasted_iota(jnp.int32, (N_SEL, 1), 0)
        cur = (t0 + lax.broadcasted_iota(jnp.int32, (1, sq), 1)) // SEL_BLOCK
        forced = (jj == 0) | (jj == cur) | (jj == cur - 1)
        imp = jnp.where(forced, -NEG_INF, jnp.where(jj > cur, NEG_INF, imp))
        rank = jnp.zeros((N_SEL, sq), f32)
        for j2 in range(N_SEL):
            row = imp[j2:j2 + 1, :]
            beats = (row > imp) | ((row == imp) & (j2 < jj))
            rank = rank + jnp.where(beats, 1.0, 0.0)
        not_sel_t = jnp.where(rank < SEL_TOPK, 0.0, 1.0).astype(bf16)
        not_sel_s[rows, :] = _dot_tn(not_sel_t, eye_ref[...]).astype(bf16)
        return carry

    lax.fori_loop(0, SEQ // sq, select_tile, 0)

    def make_tile(n_full):
        def tile(i, carry):
            t0 = i * tq
            rows = pl.ds(pl.multiple_of(t0, tq), tq)
            qs = [qn_s[r, rows, :] for r in range(rep)]
            tpos = t0 + lax.broadcasted_iota(jnp.int32, (tq, 1), 0)
            m_init = jnp.full((rep, tq, 1), NEG_INF, f32)
            acc_init = jnp.zeros((rep, tq, 2 * d), f32)

            qw = jnp.concatenate([jnp.concatenate([q, pad_flag], axis=1) for q in qs], axis=0)
            wrows = pl.ds(pl.multiple_of(t0, tq), WIN_SPAN)
            sw = _dot_nt(qw, kwa_s[wrows, :]).reshape(rep, tq, WIN_SPAN)
            sw = jnp.concatenate([sw[:, :, :tq] + band_first[None], sw[:, :, tq:NSA_WINDOW],
                                  sw[:, :, NSA_WINDOW:] + band_last[None]], axis=2)
            _, acc_w = softmax_update(sw, m_init, acc_init, vwa_s[wrows, :])
            o_win = acc_w[:, :, :d] / acc_w[:, :, d:]

            not_sel = not_sel_s[rows, :]
            qsel = jnp.concatenate([jnp.concatenate([q, not_sel], axis=1) for q in qs], axis=0)
            m_s, acc_s = m_init, acc_init
            for kt in range(n_full + 1):
                krows = slice(kt * SEL_TK, (kt + 1) * SEL_TK)
                s = _dot_nt(qsel, ksa_s[krows, :]).reshape(rep, tq, SEL_TK)
                if kt == n_full:
                    kpos = kt * SEL_TK + lax.broadcasted_iota(jnp.int32, (1, SEL_TK), 1)
                    s = s + jnp.where(kpos <= tpos, 0.0, NEG_INF)[None]
                m_s, acc_s = softmax_update(s, m_s, acc_s, vsa_s[krows, :])
            o_sel = acc_s[:, :, :d] / acc_s[:, :, d:]

            gates = jax.nn.sigmoid(ng_ref[rows, :])
            for r in range(rep):
                g_s = gates[:, rep + r:rep + r + 1]
                g_w = gates[:, 2 * rep + r:2 * rep + r + 1]
                cols = slice(r * d, (r + 1) * d)
                o = o_ref[rows, cols].astype(f32) + g_s * o_sel[r] + g_w * o_win[r]
                o_ref[rows, cols] = (o * _silu(nz_ref[rows, cols])).astype(o_ref.dtype)
            return carry
        return tile

    tiles_per_key_tile = SEL_TK // tq
    for n_full in range(SEQ // SEL_TK):
        lax.fori_loop(n_full * tiles_per_key_tile, (n_full + 1) * tiles_per_key_tile, make_tile(n_full), 0)


def _nsa(h, q_g, k_g, pos_k, pos_v, k_w1, k_w2, v_w1, v_w2):
    b = h.shape[0]
    grp_w = NSA_REP * HEAD_DIM
    cs = np.arange(LANE) * CMP_STRIDE
    ss = np.arange(LANE) * SEL_BLOCK
    ovl = ((cs[None, :] < ss[:, None] + SEL_BLOCK) & (cs[None, :] + CMP_LEN > ss[:, None])
           & (np.arange(LANE)[None, :] < N_CMP) & (np.arange(LANE)[:, None] < N_SEL))
    ovl_t = jnp.asarray(ovl.astype(np.float32))
    key_blk = np.arange(SEQ) // SEL_BLOCK
    gsel = jnp.asarray(np.where(key_blk[:, None] == np.arange(LANE)[None, :], NEG_INF, 0.0), dtype=bf16)
    eye = jnp.asarray(np.eye(N_SEL, LANE), dtype=bf16)

    def wide(off):
        return pl.BlockSpec((None, SEQ, grp_w), lambda bi, gi: (bi, 0, off // NSA_REP + gi))

    def col(off):
        return pl.BlockSpec((None, SEQ, HEAD_DIM), lambda bi, gi: (bi, 0, off + gi))

    def full(shape):
        return pl.BlockSpec(shape, lambda bi, gi: (0,) * len(shape))

    flat = CMP_LEN * HEAD_DIM
    return pl.pallas_call(
        _nsa_body,
        grid=(b, NSA_KV_HEADS),
        in_specs=[wide(EV_NQ), col(EV_KC), col(EV_VC), col(EV_KS), col(EV_VS), col(EV_KW), col(EV_VW),
                  wide(EV_NZ), col(EV_NG),
                  full((1, HEAD_DIM)), full((NSA_N_BRANCH, HEAD_DIM)), full((CMP_LEN, HEAD_DIM)), full((CMP_LEN, HEAD_DIM)),
                  full((flat, HEAD_DIM)), full((HEAD_DIM, HEAD_DIM)), full((flat, HEAD_DIM)), full((HEAD_DIM, HEAD_DIM)),
                  full((LANE, LANE)), full((SEQ, LANE)), full((N_SEL, LANE))],
        out_specs=pl.BlockSpec((None, SEQ, grp_w), lambda bi, gi: (bi, 0, gi)),
        out_shape=jax.ShapeDtypeStruct((b, SEQ, NSA_HEADS * HEAD_DIM), bf16),
        scratch_shapes=[pltpu.VMEM((SEQ // CMP_STRIDE, HEAD_DIM), bf16), pltpu.VMEM((SEQ // CMP_STRIDE, HEAD_DIM), bf16),
                        pltpu.VMEM((SEQ, 2 * HEAD_DIM), bf16), pltpu.VMEM((SEQ, 2 * HEAD_DIM), bf16),
                        pltpu.VMEM((SEQ + NSA_WINDOW, 2 * HEAD_DIM), bf16), pltpu.VMEM((SEQ + NSA_WINDOW, 2 * HEAD_DIM), bf16),
                        pltpu.VMEM((NSA_REP, SEQ, HEAD_DIM), bf16), pltpu.VMEM((SEQ, LANE), bf16)],
        compiler_params=_cparams("parallel", "parallel"),
        name="nsa",
    )(h, h, h, h, h, h, h, h, h,
      q_g.reshape(1, HEAD_DIM), k_g, pos_k, pos_v,
      k_w1.astype(bf16), k_w2.astype(bf16), v_w1.astype(bf16), v_w2.astype(bf16), ovl_t, gsel, eye)


def _conv_body(u_ref, g_ref, z_ref, w_ref, b_ref, lg_ref, lb_ref, o_ref, a_ext, a_sh, y_s):
    ext = CONV_HALO + CONV_T
    n_cb = CONV_W // LANE
    lead = CONV_HALO - (CONV_K - 1)

    @pl.when(pl.program_id(1) == 0)
    def _():
        for cb in range(n_cb):
            a_ext[cb, 0:CONV_HALO, :] = jnp.zeros((CONV_HALO, LANE), f32)
            a_ext[cb, ext:ext + SUBLANE, :] = jnp.zeros((SUBLANE, LANE), f32)

    for cb in range(n_cb):
        lanes = slice(cb * LANE, (cb + 1) * LANE)
        for c in range(CONV_T // 128):
            rows = slice(c * 128, (c + 1) * 128)
            a_ext[cb, CONV_HALO + c * 128:CONV_HALO + (c + 1) * 128, :] = u_ref[rows, lanes] * jax.nn.sigmoid(g_ref[rows, lanes])
        for r in range(1, SUBLANE):
            for c in range(ext // CONV_HALO):
                a_sh[r - 1, cb, c * CONV_HALO:(c + 1) * CONV_HALO, :] = a_ext[cb, c * CONV_HALO + r:(c + 1) * CONV_HALO + r, :]

        taps_w = [jnp.broadcast_to(w_ref[k:k + 1, lanes], (CONV_RB, LANE)) for k in range(CONV_K)]
        bias = jnp.broadcast_to(b_ref[:, lanes], (CONV_RB, LANE))

        def conv_rows(rb, carry, cb=cb, taps_w=taps_w, bias=bias):
            r0 = pl.multiple_of(rb * CONV_RB, CONV_RB)
            parts = [bias, None]
            for k in range(CONV_K):
                shift, base = (lead + k) % SUBLANE, (lead + k) // SUBLANE * SUBLANE
                rows = pl.ds(r0 + base, CONV_RB)
                tap = a_ext[cb, rows, :] if shift == 0 else a_sh[shift - 1, cb, rows, :]
                term = tap * taps_w[k]
                parts[k % 2] = term if parts[k % 2] is None else parts[k % 2] + term
            y_s[cb, pl.ds(r0, CONV_RB), :] = parts[0] + parts[1]
            return carry

        lax.fori_loop(0, CONV_T // CONV_RB, conv_rows, 0)
        a_ext[cb, 0:CONV_HALO, :] = a_ext[cb, CONV_T:CONV_T + CONV_HALO, :]

    def norm_rows(rb, carry):
        rows = pl.ds(pl.multiple_of(rb * CONV_RB, CONV_RB), CONV_RB)
        y = jnp.concatenate([y_s[cb, rows, :] for cb in range(n_cb)], axis=1)
        mu = jnp.mean(y, axis=-1, keepdims=True)
        var = jnp.mean(jnp.square(y - mu), axis=-1, keepdims=True)
        yn = (y - mu) * lax.rsqrt(var + NORM_EPS) * lg_ref[...] + lb_ref[...]
        o_ref[rows, :] = (_silu(yn) * _silu(z_ref[rows, :])).astype(o_ref.dtype)
        return carry

    lax.fori_loop(0, CONV_T // CONV_RB, norm_rows, 0, unroll=2)


def _conv(h, dw_w, dw_b, ln_g, ln_b):
    b = h.shape[0]

    def col(off):
        return pl.BlockSpec((None, CONV_T, CONV_W), lambda bi, ti: (bi, ti, off * LANE // CONV_W))

    def full(rows):
        return pl.BlockSpec((rows, CONV_W), lambda bi, ti: (0, 0))

    return pl.pallas_call(
        _conv_body,
        grid=(b, SEQ // CONV_T),
        in_specs=[col(OD_CU), col(OD_CG), col(OD_CZ), full(CONV_K), full(1), full(1), full(1)],
        out_specs=pl.BlockSpec((None, CONV_T, CONV_W), lambda bi, ti: (bi, ti, 0)),
        out_shape=jax.ShapeDtypeStruct((b, SEQ, CONV_W), bf16),
        scratch_shapes=[pltpu.VMEM((CONV_W // LANE, CONV_HALO + CONV_T + SUBLANE, LANE), f32),
                        pltpu.VMEM((SUBLANE - 1, CONV_W // LANE, CONV_HALO + CONV_T, LANE), f32),
                        pltpu.VMEM((CONV_W // LANE, CONV_T, LANE), f32)],
        compiler_params=_cparams("parallel", "arbitrary"),
        name="conformer_conv",
    )(h, h, h, dw_w, dw_b.reshape(1, CONV_W), ln_g.reshape(1, CONV_W), ln_b.reshape(1, CONV_W))


def _dil_body(q0_ref, k0_ref, v0_ref, q1_ref, k1_ref, v1_ref, q2_ref, k2_ref, v2_ref, z_ref, qg_ref, kg_ref,
              o_ref, qn_s, kn_s, va_s, o0_s, o1_s, o2_s, l0_s, l1_s, l2_s):
    blk = DIL_BLOCK
    n_tiles = SEQ // blk
    groups = ((q0_ref, k0_ref, v0_ref, o0_s, l0_s), (q1_ref, k1_ref, v1_ref, o1_s, l1_s), (q2_ref, k2_ref, v2_ref, o2_s, l2_s))
    kn_s[0:blk, :] = jnp.zeros((blk, HEAD_DIM), bf16)
    va_s[0:blk, :] = jnp.zeros((blk, 2 * HEAD_DIM), bf16)
    ones = jnp.ones((blk, HEAD_DIM), bf16)
    row2 = lax.broadcasted_iota(jnp.int32, (blk, 2 * blk), 0)
    col2 = lax.broadcasted_iota(jnp.int32, (blk, 2 * blk), 1)
    band_prev = jnp.where((col2 >= row2) & (col2 <= row2 + blk), 0.0, NEG_INF)
    band_first = jnp.where((col2 >= blk) & (col2 <= row2 + blk), 0.0, NEG_INF)

    for (q_ref, k_ref, v_ref, o_s, l_s), (window, dil) in zip(groups, DIL_PATTERNS):
        assert window // dil == blk
        nb = SEQ // dil // blk

        def natural_rows(u, dil=dil, nb=nb):
            start = u // nb + (u % nb) * (blk * dil)
            return pl.ds(start, blk, stride=dil) if dil > 1 else pl.ds(pl.multiple_of(start, blk), blk)

        def prep(u, carry, q_ref=q_ref, k_ref=k_ref, v_ref=v_ref, natural_rows=natural_rows):
            src = natural_rows(u)
            qn_s[pl.ds(pl.multiple_of(u * blk, blk), blk), :] = _rms(q_ref[src, :], qg_ref[...]).astype(bf16)
            dst = pl.ds(pl.multiple_of((u + 1) * blk, blk), blk)
            kn_s[dst, :] = _rms(k_ref[src, :], kg_ref[...]).astype(bf16)
            va_s[dst, :] = jnp.concatenate([v_ref[src, :].astype(bf16), ones], axis=1)
            return carry

        lax.fori_loop(0, n_tiles, prep, 0, unroll=DIL_UNROLL)

        def att(u, carry, o_s=o_s, l_s=l_s, nb=nb, natural_rows=natural_rows):
            q = qn_s[pl.ds(pl.multiple_of(u * blk, blk), blk), :]
            keys = pl.ds(pl.multiple_of(u * blk, blk), 2 * blk)
            band = jnp.where((u % nb) > 0, band_prev, band_first)
            s = _dot_nt(q, kn_s[keys, :]) * SCALE + band
            m = jnp.max(s, axis=-1, keepdims=True)
            pv = _dot(jnp.exp(s - m).astype(bf16), va_s[keys, :])
            den = pv[:, HEAD_DIM:]
            dst = natural_rows(u)
            o_s[dst, :] = pv[:, :HEAD_DIM] / den
            l_s[dst, :] = m + jnp.log(den)
            return carry

        lax.fori_loop(0, n_tiles, att, 0, unroll=DIL_UNROLL)

    def mix(c, carry):
        rows = pl.ds(pl.multiple_of(c * blk, blk), blk)
        l0, l1, l2 = l0_s[rows, :], l1_s[rows, :], l2_s[rows, :]
        m = jnp.maximum(jnp.maximum(l0, l1), l2)
        w0, w1, w2 = jnp.exp(l0 - m), jnp.exp(l1 - m), jnp.exp(l2 - m)
        o = (w0 * o0_s[rows, :] + w1 * o1_s[rows, :] + w2 * o2_s[rows, :]) / (w0 + w1 + w2)
        o_ref[rows, :] = (o * _silu(z_ref[rows, :])).astype(o_ref.dtype)
        return carry

    lax.fori_loop(0, n_tiles, mix, 0, unroll=DIL_UNROLL)


def _dilated(h, q_g, k_g):
    b = h.shape[0]

    def col(off):
        return pl.BlockSpec((None, SEQ, HEAD_DIM), lambda bi, hi: (bi, 0, off + hi))

    gain = pl.BlockSpec((1, HEAD_DIM), lambda bi, hi: (0, 0))
    specs = []
    for gi in range(len(DIL_PATTERNS)):
        specs += [col(OD_DQ + gi * DIL_GROUP_HEADS), col(OD_DK + gi * DIL_GROUP_HEADS), col(OD_DV + gi * DIL_GROUP_HEADS)]
    return pl.pallas_call(
        _dil_body,
        grid=(b, DIL_GROUP_HEADS),
        in_specs=specs + [col(OD_DZ), gain, gain],
        out_specs=pl.BlockSpec((None, SEQ, HEAD_DIM), lambda bi, hi: (bi, 0, hi)),
        out_shape=jax.ShapeDtypeStruct((b, SEQ, DIL_GROUP_HEADS * HEAD_DIM), bf16),
        scratch_shapes=[pltpu.VMEM((SEQ, HEAD_DIM), bf16), pltpu.VMEM((SEQ + DIL_BLOCK, HEAD_DIM), bf16),
                        pltpu.VMEM((SEQ + DIL_BLOCK, 2 * HEAD_DIM), bf16)] + [pltpu.VMEM((SEQ, HEAD_DIM), f32)] * 6,
        compiler_params=_cparams("parallel", "parallel"),
        name="dilated",
    )(*([h] * 10), q_g.reshape(1, HEAD_DIM), k_g.reshape(1, HEAD_DIM))


def _even_w_in(w):
    ng0 = EV_NZ * LANE
    nz0 = ng0 + NSA_N_BRANCH * NSA_HEADS
    ng = w[:, ng0:nz0].reshape(-1, NSA_N_BRANCH, NSA_KV_HEADS, NSA_REP)
    ng = ng.transpose(0, 2, 1, 3).reshape(-1, NSA_KV_HEADS, NSA_N_BRANCH * NSA_REP)
    ng = jnp.pad(ng, ((0, 0), (0, 0), (0, LANE - NSA_N_BRANCH * NSA_REP))).reshape(-1, NSA_KV_HEADS * LANE)
    out = jnp.concatenate([w[:, :ng0], w[:, nz0:], ng], axis=1)
    return jnp.pad(out, ((0, 0), (0, EV_COLS - out.shape[1]))).astype(bf16)


def kernel(x, ev_norm, ev_w_in, ev_w_out, ev_ret_norm, ev_nsa_q_norm, ev_nsa_k_norm, ev_cmp_pos_k, ev_cmp_pos_v, ev_cmp_k_w1, ev_cmp_k_w2, ev_cmp_v_w1, ev_cmp_v_w2, od_norm, od_w_in, od_w_out, od_dw_w, od_dw_b, od_conv_norm_g, od_conv_norm_b, od_dil_q_norm, od_dil_k_norm):
    b, s, d = x.shape
    assert (s, d) == (SEQ, D_MODEL)
    x0 = x.reshape(b * s, d)

    h0 = _proj_in(x0, ev_norm[0], _even_w_in(ev_w_in[0])).reshape(b, s, EV_COLS)
    a_out = _retention(h0, ev_ret_norm[0])
    b_out = _nsa(h0, ev_nsa_q_norm[0], ev_nsa_k_norm[0], ev_cmp_pos_k[0], ev_cmp_pos_v[0],
                 ev_cmp_k_w1[0], ev_cmp_k_w2[0], ev_cmp_v_w1[0], ev_cmp_v_w2[0])
    wo = ev_w_out[0].astype(bf16)
    x1 = _proj_out(x0, a_out.reshape(b * s, -1), b_out.reshape(b * s, -1), wo[:RET_W], wo[RET_W:])

    h1 = _proj_in(x1, od_norm[0], od_w_in[0].astype(bf16)).reshape(b, s, OD_COLS)
    c_out = _conv(h1, od_dw_w[0], od_dw_b[0], od_conv_norm_g[0], od_conv_norm_b[0])
    d_out = _dilated(h1, od_dil_q_norm[0], od_dil_k_norm[0])
    wo = od_w_out[0].astype(bf16)
    x2 = _proj_out(x1, c_out.reshape(b * s, -1), d_out.reshape(b * s, -1), wo[:CONV_W], wo[CONV_W:])
    return x2.reshape(b, s, d)
```

```python
import functools

import jax
import jax.numpy as jnp
import numpy as np
from jax import lax
from jax.experimental import pallas as pl
from jax.experimental.pallas import tpu as pltpu

f32 = jnp.float32
bf16 = jnp.bfloat16

D_MODEL = 2048
SEQ = 2048
HEAD_DIM = 128
SCALE = HEAD_DIM ** -0.5
NORM_EPS = 1e-6
NEG_INF = -1e30
LANE = 128
SUBLANE = 8

RET_HEADS = 8
RET_W = RET_HEADS * HEAD_DIM
RET_CHUNK = 256
RET_UNROLL = 8
ROPE_BASE = 10000.0
NSA_HEADS = 8
NSA_KV_HEADS = 2
NSA_REP = NSA_HEADS // NSA_KV_HEADS
NSA_N_BRANCH = 3
CMP_LEN = 32
CMP_STRIDE = 16
N_CMP = (SEQ - CMP_LEN) // CMP_STRIDE + 1
SEL_BLOCK = 64
N_SEL = SEQ // SEL_BLOCK
SEL_TOPK = 16
NSA_WINDOW = 512
NSA_TQ = 128
SEL_TK = 512
NSA_PREP_ROWS = 256
NSA_SELECT_TQ = 512
WIN_SPAN = NSA_WINDOW + NSA_TQ
CONV_W = 1024
CONV_K = 31
CONV_T = 512
CONV_HALO = 32
CONV_RB = 64
DIL_PATTERNS = ((128, 1), (512, 4), (2048, 16))
DIL_GROUP_HEADS = 4
DIL_HEADS = DIL_GROUP_HEADS * len(DIL_PATTERNS)
DIL_BLOCK = 128
DIL_UNROLL = 8

EV_RQ, EV_RK, EV_RV, EV_RZ = 0, 8, 16, 24
EV_NQ = 32
EV_KC, EV_VC, EV_KS, EV_VS, EV_KW, EV_VW = 40, 42, 44, 46, 48, 50
EV_NZ = 52
EV_NG = 60
EV_COLS = 8192
EV_F32_TILE = 5
EV_F32_KC, EV_F32_VC = 0, 2
OD_CU, OD_CG, OD_CZ = 0, 8, 16
OD_DQ0, OD_DK0, OD_DV0 = 24, 28, 32
OD_DZ = 36
OD_F32_TILE, OD_F32_TILES = 5, 3
OD_F32_Q, OD_F32_K, OD_F32_V = 0, 8, 16
OD_COLS = 8192

VMEM_LIMIT = 56 * 1024 * 1024


def _cparams(*sem):
    return pltpu.CompilerParams(dimension_semantics=sem, vmem_limit_bytes=VMEM_LIMIT)


def _rms(x, gain):
    ms = jnp.mean(x * x, axis=-1, keepdims=True)
    return x * lax.rsqrt(ms + NORM_EPS) * gain


def _silu(z):
    return z * jax.nn.sigmoid(z)


def _dot(a, b):
    return jnp.dot(a, b, preferred_element_type=f32)


def _dot_nt(a, b):
    return lax.dot_general(a, b, (((1,), (1,)), ((), ())), preferred_element_type=f32)


def _dot_tn(a, b):
    return lax.dot_general(a, b, (((0,), (0,)), ((), ())), preferred_element_type=f32)


PROJ_TM = 1024
PROJ_TN = 1024
NORM_ROWS = 256


def _proj_in_body(x_ref, g_ref, w_ref, o_ref, of_ref, xn_ref, *, f32_tile, f32_tiles):
    j = pl.program_id(1)

    @pl.when(j == 0)
    def _():
        for c in range(PROJ_TM // NORM_ROWS):
            rows = slice(c * NORM_ROWS, (c + 1) * NORM_ROWS)
            xn_ref[rows, :] = _rms(x_ref[rows, :], g_ref[...]).astype(bf16)

    acc = _dot(xn_ref[...], w_ref[...])
    o_ref[...] = acc.astype(o_ref.dtype)

    @pl.when((j >= f32_tile) & (j < f32_tile + f32_tiles))
    def _():
        of_ref[...] = acc


def _proj_in(x2, gain, w, f32_tile, f32_tiles):
    m, d = x2.shape
    n = w.shape[1]
    return pl.pallas_call(
        functools.partial(_proj_in_body, f32_tile=f32_tile, f32_tiles=f32_tiles),
        grid=(m // PROJ_TM, n // PROJ_TN),
        in_specs=[
            pl.BlockSpec((PROJ_TM, d), lambda i, j: (i, 0)),
            pl.BlockSpec((1, d), lambda i, j: (0, 0)),
            pl.BlockSpec((d, PROJ_TN), lambda i, j: (0, j)),
        ],
        out_specs=[pl.BlockSpec((PROJ_TM, PROJ_TN), lambda i, j: (i, j)),
                   pl.BlockSpec((PROJ_TM, PROJ_TN), lambda i, j: (i, jnp.clip(j - f32_tile, 0, f32_tiles - 1)))],
        out_shape=[jax.ShapeDtypeStruct((m, n), bf16), jax.ShapeDtypeStruct((m, f32_tiles * PROJ_TN), f32)],
        scratch_shapes=[pltpu.VMEM((PROJ_TM, d), bf16)],
        compiler_params=_cparams("parallel", "arbitrary"),
        name="proj_in",
    )(x2, gain.reshape(1, d), w)


OUT_TM = 512


def _proj_out_body(x_ref, m0_ref, m1_ref, w0_ref, w1_ref, o_ref):
    o_ref[...] = x_ref[...] + _dot(m0_ref[...], w0_ref[...]) + _dot(m1_ref[...], w1_ref[...])


def _proj_out(x2, m0, m1, w):
    m, d = x2.shape
    k0, k1 = m0.shape[1], m1.shape[1]
    assert w.shape == (k0 + k1, d) and k0 % k1 == 0
    return pl.pallas_call(
        _proj_out_body,
        grid=(m // OUT_TM,),
        in_specs=[
            pl.BlockSpec((OUT_TM, d), lambda i: (i, 0)),
            pl.BlockSpec((OUT_TM, k0), lambda i: (i, 0)),
            pl.BlockSpec((OUT_TM, k1), lambda i: (i, 0)),
            pl.BlockSpec((k0, d), lambda i: (0, 0)),
            pl.BlockSpec((k1, d), lambda i: (k0 // k1, 0)),
        ],
        out_specs=pl.BlockSpec((OUT_TM, d), lambda i: (i, 0)),
        out_shape=jax.ShapeDtypeStruct((m, d), f32),
        compiler_params=_cparams("parallel"),
        name="proj_out",
    )(x2, m0, m1, w, w)


def _ret_body(q_ref, k_ref, v_ref, z_ref, cos_ref, sin_ref, dmask_ref, kdec_ref, qdec_ref, cdec_ref, g_ref, o_ref):
    c_len = RET_CHUNK

    def rot(t, cs, sn):
        return t * cs + pltpu.roll(t, HEAD_DIM // 2, 1) * sn

    def body(n, state):
        rows = pl.ds(pl.multiple_of(n * c_len, c_len), c_len)
        cs = cos_ref[rows, :]
        sn = sin_ref[rows, :]
        q = rot(q_ref[rows, :].astype(f32), cs, sn)
        k = rot(k_ref[rows, :].astype(f32), cs, sn) * SCALE
        vb = v_ref[rows, :]
        s = _dot_nt(q.astype(bf16), k.astype(bf16)) * dmask_ref[...]
        o = _dot(s.astype(bf16), vb)
        o = o + _dot((q * qdec_ref[...]).astype(bf16), state.astype(bf16))
        kv = _dot_tn((k * kdec_ref[...]).astype(bf16), vb)
        state = cdec_ref[...] * state + kv
        y = _rms(o, g_ref[...])
        o_ref[rows, :] = (y * _silu(z_ref[rows, :].astype(f32))).astype(o_ref.dtype)
        return state

    lax.fori_loop(0, SEQ // c_len, body, jnp.zeros((HEAD_DIM, HEAD_DIM), f32), unroll=RET_UNROLL)


def _retention(h, ret_g):
    b = h.shape[0]
    c_len = RET_CHUNK
    half = HEAD_DIM // 2
    pos = jnp.arange(SEQ)
    inv = ROPE_BASE ** (-jnp.arange(half, dtype=f32) / half)
    ang = pos.astype(f32)[:, None] * inv[None, :]
    cos, sin = jnp.cos(ang), jnp.sin(ang)
    cos_t = jnp.concatenate([cos, cos], axis=-1)
    sin_t = jnp.concatenate([-sin, sin], axis=-1)
    log_g = jnp.log(1.0 - 2.0 ** (-5.0 - jnp.arange(RET_HEADS, dtype=f32)))
    idx = jnp.arange(c_len, dtype=f32)
    diff = idx[:, None] - idx[None, :]
    dmask = jnp.where(diff >= 0, jnp.exp(jnp.maximum(diff, 0.0)[None] * log_g[:, None, None]), 0.0)
    ones = jnp.ones((1, 1, HEAD_DIM), f32)
    k_dec = jnp.exp((c_len - 1 - idx)[None, :] * log_g[:, None])[:, :, None] * ones
    q_dec = jnp.exp((idx + 1.0)[None, :] * log_g[:, None])[:, :, None] * ones
    chunk_dec = jnp.exp(c_len * log_g)[:, None, None] * ones

    def col(off):
        return pl.BlockSpec((None, SEQ, HEAD_DIM), lambda bi, hi: (bi, 0, off + hi))

    table = pl.BlockSpec((SEQ, HEAD_DIM), lambda bi, hi: (0, 0))

    def per_head(rows, cols=HEAD_DIM):
        return pl.BlockSpec((None, rows, cols), lambda bi, hi: (hi, 0, 0))

    return pl.pallas_call(
        _ret_body,
        grid=(b, RET_HEADS),
        in_specs=[col(EV_RQ), col(EV_RK), col(EV_RV), col(EV_RZ), table, table,
                  per_head(c_len, c_len), per_head(c_len), per_head(c_len), per_head(1), per_head(1)],
        out_specs=pl.BlockSpec((None, SEQ, HEAD_DIM), lambda bi, hi: (bi, 0, hi)),
        out_shape=jax.ShapeDtypeStruct((b, SEQ, RET_W), bf16),
        compiler_params=_cparams("parallel", "parallel"),
        name="retention",
    )(h, h, h, h, cos_t, sin_t, dmask, k_dec, q_dec, chunk_dec, ret_g.reshape(RET_HEADS, 1, HEAD_DIM))


def _nsa_body(q_ref, kc_ref, vc_ref, ks_ref, vs_ref, kw_ref, vw_ref, nz_ref, ng_ref,
              qg_ref, kg_ref, posk_ref, posv_ref, w1k_ref, w2k_ref, w1v_ref, w2v_ref, ovl_ref, gsel_ref, eye_ref,
              o_ref, kcmp_s, vcmp_s, ksa_s, vsa_s, kwa_s, vwa_s, qn_s, not_sel_s):
    tq = NSA_TQ
    sq = NSA_SELECT_TQ
    rep = NSA_REP
    half_blk = CMP_LEN // 2
    d = HEAD_DIM
    exp2_scale = SCALE * float(np.log2(np.e))

    def compress(t_ref, pos_ref, w1_ref, w2_ref):
        first = jnp.zeros((SEQ // CMP_STRIDE, d), f32)
        second = jnp.zeros((SEQ // CMP_STRIDE, d), f32)
        for i in range(half_blk):
            ti = t_ref[pl.ds(i, SEQ // CMP_STRIDE, stride=CMP_STRIDE), :]
            first = first + _dot((ti + pos_ref[i:i + 1, :]).astype(bf16), w1_ref[i * d:(i + 1) * d, :])
            j = half_blk + i
            second = second + _dot((ti + pos_ref[j:j + 1, :]).astype(bf16), w1_ref[j * d:(j + 1) * d, :])
        pre = first + pltpu.roll(second, SEQ // CMP_STRIDE - 1, 0)
        return _dot(jax.nn.gelu(pre).astype(bf16), w2_ref[...])

    kcmp_s[...] = _rms(compress(kc_ref, posk_ref, w1k_ref, w2k_ref), kg_ref[0:1, :]).astype(bf16)
    vcmp_s[...] = compress(vc_ref, posv_ref, w1v_ref, w2v_ref).astype(bf16)
    lane0 = lax.broadcasted_iota(jnp.int32, (tq, d), 1) == 0
    pc_rows = NSA_PREP_ROWS
    ones = jnp.ones((pc_rows, d), bf16)
    zeros = jnp.zeros((pc_rows, d), bf16)
    for c in range(NSA_WINDOW // pc_rows):
        rows = slice(c * pc_rows, (c + 1) * pc_rows)
        kwa_s[rows, 0:d] = zeros
        kwa_s[rows, d:2 * d] = jnp.where(lax.broadcasted_iota(jnp.int32, (pc_rows, d), 1) == 0, NEG_INF, 0.0).astype(bf16)
        vwa_s[rows, 0:d] = zeros
        vwa_s[rows, d:2 * d] = zeros
    for c in range(SEQ // pc_rows):
        rows = slice(c * pc_rows, (c + 1) * pc_rows)
        wrows = slice(NSA_WINDOW + c * pc_rows, NSA_WINDOW + (c + 1) * pc_rows)
        ksa_s[rows, 0:d] = _rms(ks_ref[rows, :].astype(f32), kg_ref[1:2, :]).astype(bf16)
        ksa_s[rows, d:2 * d] = gsel_ref[rows, :]
        vsa_s[rows, 0:d] = vs_ref[rows, :]
        vsa_s[rows, d:2 * d] = ones
        kwa_s[wrows, 0:d] = _rms(kw_ref[rows, :].astype(f32), kg_ref[2:3, :]).astype(bf16)
        kwa_s[wrows, d:2 * d] = zeros
        vwa_s[wrows, 0:d] = vw_ref[rows, :]
        vwa_s[wrows, d:2 * d] = ones

    row_i = lax.broadcasted_iota(jnp.int32, (tq, tq), 0)
    col_i = lax.broadcasted_iota(jnp.int32, (tq, tq), 1)
    band_first = jnp.where(col_i > row_i, 0.0, NEG_INF)
    band_last = jnp.where(col_i <= row_i, 0.0, NEG_INF)
    pad_flag = jnp.where(lane0, 1.0, 0.0).astype(bf16)

    def softmax_update(s, m, acc, v):
        n = s.shape[-1]
        m_new = jnp.maximum(m, jnp.max(s, axis=-1, keepdims=True))
        alpha = jnp.exp2((m - m_new) * exp2_scale)
        e = jnp.exp2((s - m_new) * exp2_scale)
        pv = _dot(e.reshape(rep * tq, n).astype(bf16), v).reshape(rep, tq, 2 * d)
        return m_new, alpha * acc + pv

    def select_tile(i, carry):
        t0 = i * sq
        rows = pl.ds(pl.multiple_of(t0, sq), sq)
        qs = [_rms(q_ref[rows, r * d:(r + 1) * d].astype(f32), qg_ref[...]).astype(bf16) for r in range(rep)]
        for r in range(rep):
            qn_s[r, rows, :] = qs[r]
        tpos = t0 + lax.broadcasted_iota(jnp.int32, (sq, 1), 0)
        qst = jnp.concatenate(qs, axis=0)
        n_idx = lax.broadcasted_iota(jnp.int32, (1, LANE), 1)
        valid = (n_idx * CMP_STRIDE + (CMP_LEN - 1) <= tpos) & (n_idx < N_CMP)
        cbias = jnp.where(valid, 0.0, NEG_INF)
        sc = (_dot_nt(qst, kcmp_s[...]) * SCALE).reshape(rep, sq, LANE) + cbias[None]
        mc = jnp.max(sc, axis=-1, keepdims=True)
        ec = jnp.exp(sc - mc)
        any_valid = jnp.where(tpos >= CMP_LEN - 1, 1.0, 0.0)
        pc = ec / jnp.sum(ec, axis=-1, keepdims=True) * any_valid[None]
        o_cmp = _dot(pc.reshape(rep * sq, LANE).astype(bf16), vcmp_s[...]).reshape(rep, sq, d)
        gates = jax.nn.sigmoid(ng_ref[rows, :].astype(f32))
        for r in range(rep):
            o_ref[rows, r * d:(r + 1) * d] = (gates[:, r:r + 1] * o_cmp[r]).astype(o_ref.dtype)
        psum = pc[0]
        for r in range(1, rep):
            psum = psum + pc[r]

        imp = lax.dot_general(ovl_ref[...], psum, (((1,), (1,)), ((), ())),
                              preferred_element_type=f32, precision=lax.Precision.HIGHEST)[:N_SEL, :]
        jj = lax.broadcasted_iota(jnp.int32, (N_SEL, 1), 0)
        cur = (t0 + lax.broadcasted_iota(jnp.int32, (1, sq), 1)) // SEL_BLOCK
        forced = (jj == 0) | (jj == cur) | (jj == cur - 1)
        imp = jnp.where(forced, -NEG_INF, jnp.where(jj > cur, NEG_INF, imp))
        rank = jnp.zeros((N_SEL, sq), f32)
        for j2 in range(N_SEL):
            row = imp[j2:j2 + 1, :]
            beats = (row > imp) | ((row == imp) & (j2 < jj))
            rank = rank + jnp.where(beats, 1.0, 0.0)
        not_sel_t = jnp.where(rank < SEL_TOPK, 0.0, 1.0).astype(bf16)
        not_sel_s[rows, :] = _dot_tn(not_sel_t, eye_ref[...]).astype(bf16)
        return carry

    lax.fori_loop(0, SEQ // sq, select_tile, 0)

    def make_tile(n_full):
        def tile(i, carry):
            t0 = i * tq
            rows = pl.ds(pl.multiple_of(t0, tq), tq)
            qs = [qn_s[r, rows, :] for r in range(rep)]
            tpos = t0 + lax.broadcasted_iota(jnp.int32, (tq, 1), 0)
            m_init = jnp.full((rep, tq, 1), NEG_INF, f32)
            acc_init = jnp.zeros((rep, tq, 2 * d), f32)

            qw = jnp.concatenate([jnp.concatenate([q, pad_flag], axis=1) for q in qs], axis=0)
            wrows = pl.ds(pl.multiple_of(t0, tq), WIN_SPAN)
            sw = _dot_nt(qw, kwa_s[wrows, :]).reshape(rep, tq, WIN_SPAN)
            sw = jnp.concatenate([sw[:, :, :tq] + band_first[None], sw[:, :, tq:NSA_WINDOW],
                                  sw[:, :, NSA_WINDOW:] + band_last[None]], axis=2)
            _, acc_w = softmax_update(sw, m_init, acc_init, vwa_s[wrows, :])
            o_win = acc_w[:, :, :d] / acc_w[:, :, d:]

            not_sel = not_sel_s[rows, :]
            qsel = jnp.concatenate([jnp.concatenate([q, not_sel], axis=1) for q in qs], axis=0)
            m_s, acc_s = m_init, acc_init
            for kt in range(n_full + 1):
                krows = slice(kt * SEL_TK, (kt + 1) * SEL_TK)
                s = _dot_nt(qsel, ksa_s[krows, :]).reshape(rep, tq, SEL_TK)
                if kt == n_full:
                    kpos = kt * SEL_TK + lax.broadcasted_iota(jnp.int32, (1, SEL_TK), 1)
                    s = s + jnp.where(kpos <= tpos, 0.0, NEG_INF)[None]
                m_s, acc_s = softmax_update(s, m_s, acc_s, vsa_s[krows, :])
            o_sel = acc_s[:, :, :d] / acc_s[:, :, d:]

            gates = jax.nn.sigmoid(ng_ref[rows, :].astype(f32))
            for r in range(rep):
                g_s = gates[:, rep + r:rep + r + 1]
                g_w = gates[:, 2 * rep + r:2 * rep + r + 1]
                cols = slice(r * d, (r + 1) * d)
                o = o_ref[rows, cols].astype(f32) + g_s * o_sel[r] + g_w * o_win[r]
                o_ref[rows, cols] = (o * _silu(nz_ref[rows, cols].astype(f32))).astype(o_ref.dtype)
            return carry
        return tile

    tiles_per_key_tile = SEL_TK // tq
    for n_full in range(SEQ // SEL_TK):
        lax.fori_loop(n_full * tiles_per_key_tile, (n_full + 1) * tiles_per_key_tile, make_tile(n_full), 0)


def _nsa(h, hf, q_g, k_g, pos_k, pos_v, k_w1, k_w2, v_w1, v_w2):
    b = h.shape[0]
    grp_w = NSA_REP * HEAD_DIM
    cs = np.arange(LANE) * CMP_STRIDE
    ss = np.arange(LANE) * SEL_BLOCK
    ovl = ((cs[None, :] < ss[:, None] + SEL_BLOCK) & (cs[None, :] + CMP_LEN > ss[:, None])
           & (np.arange(LANE)[None, :] < N_CMP) & (np.arange(LANE)[:, None] < N_SEL))
    ovl_t = jnp.asarray(ovl.astype(np.float32))
    key_blk = np.arange(SEQ) // SEL_BLOCK
    gsel = jnp.asarray(np.where(key_blk[:, None] == np.arange(LANE)[None, :], NEG_INF, 0.0), dtype=bf16)
    eye = jnp.asarray(np.eye(N_SEL, LANE), dtype=bf16)

    def wide(off):
        return pl.BlockSpec((None, SEQ, grp_w), lambda bi, gi: (bi, 0, off // NSA_REP + gi))

    def col(off):
        return pl.BlockSpec((None, SEQ, HEAD_DIM), lambda bi, gi: (bi, 0, off + gi))

    def full(shape):
        return pl.BlockSpec(shape, lambda bi, gi: (0,) * len(shape))

    flat = CMP_LEN * HEAD_DIM
    return pl.pallas_call(
        _nsa_body,
        grid=(b, NSA_KV_HEADS),
        in_specs=[wide(EV_NQ), col(EV_F32_KC), col(EV_F32_VC), col(EV_KS), col(EV_VS), col(EV_KW), col(EV_VW),
                  wide(EV_NZ), col(EV_NG),
                  full((1, HEAD_DIM)), full((NSA_N_BRANCH, HEAD_DIM)), full((CMP_LEN, HEAD_DIM)), full((CMP_LEN, HEAD_DIM)),
                  full((flat, HEAD_DIM)), full((HEAD_DIM, HEAD_DIM)), full((flat, HEAD_DIM)), full((HEAD_DIM, HEAD_DIM)),
                  full((LANE, LANE)), full((SEQ, LANE)), full((N_SEL, LANE))],
        out_specs=pl.BlockSpec((None, SEQ, grp_w), lambda bi, gi: (bi, 0, gi)),
        out_shape=jax.ShapeDtypeStruct((b, SEQ, NSA_HEADS * HEAD_DIM), bf16),
        scratch_shapes=[pltpu.VMEM((SEQ // CMP_STRIDE, HEAD_DIM), bf16), pltpu.VMEM((SEQ // CMP_STRIDE, HEAD_DIM), bf16),
                        pltpu.VMEM((SEQ, 2 * HEAD_DIM), bf16), pltpu.VMEM((SEQ, 2 * HEAD_DIM), bf16),
                        pltpu.VMEM((SEQ + NSA_WINDOW, 2 * HEAD_DIM), bf16), pltpu.VMEM((SEQ + NSA_WINDOW, 2 * HEAD_DIM), bf16),
                        pltpu.VMEM((NSA_REP, SEQ, HEAD_DIM), bf16), pltpu.VMEM((SEQ, LANE), bf16)],
        compiler_params=_cparams("parallel", "parallel"),
        name="nsa",
    )(h, hf, hf, h, h, h, h, h, h,
      q_g.reshape(1, HEAD_DIM), k_g, pos_k, pos_v,
      k_w1.astype(bf16), k_w2.astype(bf16), v_w1.astype(bf16), v_w2.astype(bf16), ovl_t, gsel, eye)


def _conv_body(u_ref, g_ref, z_ref, w_ref, b_ref, lg_ref, lb_ref, o_ref, a_ext, a_sh, y_s):
    ext = CONV_HALO + CONV_T
    n_cb = CONV_W // LANE
    lead = CONV_HALO - (CONV_K - 1)

    @pl.when(pl.program_id(1) == 0)
    def _():
        for cb in range(n_cb):
            a_ext[cb, 0:CONV_HALO, :] = jnp.zeros((CONV_HALO, LANE), f32)
            a_ext[cb, ext:ext + SUBLANE, :] = jnp.zeros((SUBLANE, LANE), f32)

    for cb in range(n_cb):
        lanes = slice(cb * LANE, (cb + 1) * LANE)
        for c in range(CONV_T // 128):
            rows = slice(c * 128, (c + 1) * 128)
            a_ext[cb, CONV_HALO + c * 128:CONV_HALO + (c + 1) * 128, :] = u_ref[rows, lanes].astype(f32) * jax.nn.sigmoid(g_ref[rows, lanes].astype(f32))
        for r in range(1, SUBLANE):
            for c in range(ext // CONV_HALO):
                a_sh[r - 1, cb, c * CONV_HALO:(c + 1) * CONV_HALO, :] = a_ext[cb, c * CONV_HALO + r:(c + 1) * CONV_HALO + r, :]

        taps_w = [jnp.broadcast_to(w_ref[k:k + 1, lanes], (CONV_RB, LANE)) for k in range(CONV_K)]
        bias = jnp.broadcast_to(b_ref[:, lanes], (CONV_RB, LANE))

        def conv_rows(rb, carry, cb=cb, taps_w=taps_w, bias=bias):
            r0 = pl.multiple_of(rb * CONV_RB, CONV_RB)
            parts = [bias, None]
            for k in range(CONV_K):
                shift, base = (lead + k) % SUBLANE, (lead + k) // SUBLANE * SUBLANE
                rows = pl.ds(r0 + base, CONV_RB)
                tap = a_ext[cb, rows, :] if shift == 0 else a_sh[shift - 1, cb, rows, :]
                term = tap * taps_w[k]
                parts[k % 2] = term if parts[k % 2] is None else parts[k % 2] + term
            y_s[cb, pl.ds(r0, CONV_RB), :] = parts[0] + parts[1]
            return carry

        lax.fori_loop(0, CONV_T // CONV_RB, conv_rows, 0)
        a_ext[cb, 0:CONV_HALO, :] = a_ext[cb, CONV_T:CONV_T + CONV_HALO, :]

    def norm_rows(rb, carry):
        rows = pl.ds(pl.multiple_of(rb * CONV_RB, CONV_RB), CONV_RB)
        y = jnp.concatenate([y_s[cb, rows, :] for cb in range(n_cb)], axis=1)
        mu = jnp.mean(y, axis=-1, keepdims=True)
        var = jnp.mean(jnp.square(y - mu), axis=-1, keepdims=True)
        yn = (y - mu) * lax.rsqrt(var + NORM_EPS) * lg_ref[...] + lb_ref[...]
        o_ref[rows, :] = (_silu(yn) * _silu(z_ref[rows, :].astype(f32))).astype(o_ref.dtype)
        return carry

    lax.fori_loop(0, CONV_T // CONV_RB, norm_rows, 0, unroll=2)


def _conv(h, dw_w, dw_b, ln_g, ln_b):
    b = h.shape[0]

    def col(off):
        return pl.BlockSpec((None, CONV_T, CONV_W), lambda bi, ti: (bi, ti, off * LANE // CONV_W))

    def full(rows):
        return pl.BlockSpec((rows, CONV_W), lambda bi, ti: (0, 0))

    return pl.pallas_call(
        _conv_body,
        grid=(b, SEQ // CONV_T),
        in_specs=[col(OD_CU), col(OD_CG), col(OD_CZ), full(CONV_K), full(1), full(1), full(1)],
        out_specs=pl.BlockSpec((None, CONV_T, CONV_W), lambda bi, ti: (bi, ti, 0)),
        out_shape=jax.ShapeDtypeStruct((b, SEQ, CONV_W), bf16),
        scratch_shapes=[pltpu.VMEM((CONV_W // LANE, CONV_HALO + CONV_T + SUBLANE, LANE), f32),
                        pltpu.VMEM((SUBLANE - 1, CONV_W // LANE, CONV_HALO + CONV_T, LANE), f32),
                        pltpu.VMEM((CONV_W // LANE, CONV_T, LANE), f32)],
        compiler_params=_cparams("parallel", "arbitrary"),
        name="conformer_conv",
    )(h, h, h, dw_w, dw_b.reshape(1, CONV_W), ln_g.reshape(1, CONV_W), ln_b.reshape(1, CONV_W))


def _dil_body(q0_ref, k0_ref, v0_ref, q1_ref, k1_ref, v1_ref, q2_ref, k2_ref, v2_ref, z_ref, qg_ref, kg_ref,
              o_ref, qn_s, kn_s, va_s, o0_s, o1_s, o2_s, l0_s, l1_s, l2_s):
    blk = DIL_BLOCK
    n_tiles = SEQ // blk
    groups = ((q0_ref, k0_ref, v0_ref, o0_s, l0_s), (q1_ref, k1_ref, v1_ref, o1_s, l1_s), (q2_ref, k2_ref, v2_ref, o2_s, l2_s))
    kn_s[0:blk, :] = jnp.zeros((blk, HEAD_DIM), bf16)
    va_s[0:blk, :] = jnp.zeros((blk, 2 * HEAD_DIM), bf16)
    ones = jnp.ones((blk, HEAD_DIM), bf16)
    row2 = lax.broadcasted_iota(jnp.int32, (blk, 2 * blk), 0)
    col2 = lax.broadcasted_iota(jnp.int32, (blk, 2 * blk), 1)
    band_prev = jnp.where((col2 >= row2) & (col2 <= row2 + blk), 0.0, NEG_INF)
    band_first = jnp.where((col2 >= blk) & (col2 <= row2 + blk), 0.0, NEG_INF)

    for (q_ref, k_ref, v_ref, o_s, l_s), (window, dil) in zip(groups, DIL_PATTERNS):
        assert window // dil == blk
        nb = SEQ // dil // blk

        def natural_rows(u, dil=dil, nb=nb):
            start = u // nb + (u % nb) * (blk * dil)
            return pl.ds(start, blk, stride=dil) if dil > 1 else pl.ds(pl.multiple_of(start, blk), blk)

        def prep(u, carry, q_ref=q_ref, k_ref=k_ref, v_ref=v_ref, natural_rows=natural_rows):
            src = natural_rows(u)
            qn_s[pl.ds(pl.multiple_of(u * blk, blk), blk), :] = _rms(q_ref[src, :].astype(f32), qg_ref[...]).astype(bf16)
            dst = pl.ds(pl.multiple_of((u + 1) * blk, blk), blk)
            kn_s[dst, :] = _rms(k_ref[src, :].astype(f32), kg_ref[...]).astype(bf16)
            va_s[dst, :] = jnp.concatenate([v_ref[src, :].astype(bf16), ones], axis=1)
            return carry

        lax.fori_loop(0, n_tiles, prep, 0, unroll=DIL_UNROLL)

        def att(u, carry, o_s=o_s, l_s=l_s, nb=nb, natural_rows=natural_rows):
            q = qn_s[pl.ds(pl.multiple_of(u * blk, blk), blk), :]
            keys = pl.ds(pl.multiple_of(u * blk, blk), 2 * blk)
            band = jnp.where((u % nb) > 0, band_prev, band_first)
            s = _dot_nt(q, kn_s[keys, :]) * SCALE + band
            m = jnp.max(s, axis=-1, keepdims=True)
            pv = _dot(jnp.exp(s - m).astype(bf16), va_s[keys, :])
            den = pv[:, HEAD_DIM:]
            dst = natural_rows(u)
            o_s[dst, :] = pv[:, :HEAD_DIM] / den
            l_s[dst, :] = m + jnp.log(den)
            return carry

        lax.fori_loop(0, n_tiles, att, 0, unroll=DIL_UNROLL)

    def mix(c, carry):
        rows = pl.ds(pl.multiple_of(c * blk, blk), blk)
        l0, l1, l2 = l0_s[rows, :], l1_s[rows, :], l2_s[rows, :]
        m = jnp.maximum(jnp.maximum(l0, l1), l2)
        w0, w1, w2 = jnp.exp(l0 - m), jnp.exp(l1 - m), jnp.exp(l2 - m)
        o = (w0 * o0_s[rows, :] + w1 * o1_s[rows, :] + w2 * o2_s[rows, :]) / (w0 + w1 + w2)
        o_ref[rows, :] = (o * _silu(z_ref[rows, :].astype(f32))).astype(o_ref.dtype)
        return carry

    lax.fori_loop(0, n_tiles, mix, 0, unroll=DIL_UNROLL)


def _dilated(h, hf, q_g, k_g):
    b = h.shape[0]

    def col(off):
        return pl.BlockSpec((None, SEQ, HEAD_DIM), lambda bi, hi: (bi, 0, off + hi))

    gain = pl.BlockSpec((1, HEAD_DIM), lambda bi, hi: (0, 0))
    specs = [col(OD_DQ0), col(OD_DK0), col(OD_DV0)]
    for gi in range(len(DIL_PATTERNS) - 1):
        specs += [col(OD_F32_Q + gi * DIL_GROUP_HEADS), col(OD_F32_K + gi * DIL_GROUP_HEADS), col(OD_F32_V + gi * DIL_GROUP_HEADS)]
    return pl.pallas_call(
        _dil_body,
        grid=(b, DIL_GROUP_HEADS),
        in_specs=specs + [col(OD_DZ), gain, gain],
        out_specs=pl.BlockSpec((None, SEQ, HEAD_DIM), lambda bi, hi: (bi, 0, hi)),
        out_shape=jax.ShapeDtypeStruct((b, SEQ, DIL_GROUP_HEADS * HEAD_DIM), bf16),
        scratch_shapes=[pltpu.VMEM((SEQ, HEAD_DIM), bf16), pltpu.VMEM((SEQ + DIL_BLOCK, HEAD_DIM), bf16),
                        pltpu.VMEM((SEQ + DIL_BLOCK, 2 * HEAD_DIM), bf16)] + [pltpu.VMEM((SEQ, HEAD_DIM), f32)] * 6,
        compiler_params=_cparams("parallel", "parallel"),
        name="dilated",
    )(h, h, h, hf, hf, hf, hf, hf, hf, h, q_g.reshape(1, HEAD_DIM), k_g.reshape(1, HEAD_DIM))


def _even_w_in(w):
    ng0 = EV_NZ * LANE
    nz0 = ng0 + NSA_N_BRANCH * NSA_HEADS
    ng = w[:, ng0:nz0].reshape(-1, NSA_N_BRANCH, NSA_KV_HEADS, NSA_REP)
    ng = ng.transpose(0, 2, 1, 3).reshape(-1, NSA_KV_HEADS, NSA_N_BRANCH * NSA_REP)
    ng = jnp.pad(ng, ((0, 0), (0, 0), (0, LANE - NSA_N_BRANCH * NSA_REP))).reshape(-1, NSA_KV_HEADS * LANE)
    out = jnp.concatenate([w[:, :ng0], w[:, nz0:], ng], axis=1)
    return jnp.pad(out, ((0, 0), (0, EV_COLS - out.shape[1]))).astype(bf16)


def _odd_w_in(w):
    conv_w = 3 * CONV_W
    qkv_w = DIL_HEADS * HEAD_DIM
    g0 = DIL_GROUP_HEADS * HEAD_DIM
    q, k, v = (w[:, conv_w + i * qkv_w:conv_w + (i + 1) * qkv_w] for i in range(3))
    dz = w[:, conv_w + 3 * qkv_w:]
    out = jnp.concatenate([w[:, :conv_w], q[:, :g0], k[:, :g0], v[:, :g0], dz, q[:, g0:], k[:, g0:], v[:, g0:]], axis=1)
    return out.astype(bf16)


def kernel(x, ev_norm, ev_w_in, ev_w_out, ev_ret_norm, ev_nsa_q_norm, ev_nsa_k_norm, ev_cmp_pos_k, ev_cmp_pos_v, ev_cmp_k_w1, ev_cmp_k_w2, ev_cmp_v_w1, ev_cmp_v_w2, od_norm, od_w_in, od_w_out, od_dw_w, od_dw_b, od_conv_norm_g, od_conv_norm_b, od_dil_q_norm, od_dil_k_norm):
    b, s, d = x.shape
    assert (s, d) == (SEQ, D_MODEL)
    x0 = x.reshape(b * s, d)

    h0, h0f = _proj_in(x0, ev_norm[0], _even_w_in(ev_w_in[0]), EV_F32_TILE, 1)
    h0, h0f = h0.reshape(b, s, -1), h0f.reshape(b, s, -1)
    a_out = _retention(h0, ev_ret_norm[0])
    b_out = _nsa(h0, h0f, ev_nsa_q_norm[0], ev_nsa_k_norm[0], ev_cmp_pos_k[0], ev_cmp_pos_v[0],
                 ev_cmp_k_w1[0], ev_cmp_k_w2[0], ev_cmp_v_w1[0], ev_cmp_v_w2[0])
    x1 = _proj_out(x0, a_out.reshape(b * s, -1), b_out.reshape(b * s, -1), ev_w_out[0].astype(bf16))

    h1, h1f = _proj_in(x1, od_norm[0], _odd_w_in(od_w_in[0]), OD_F32_TILE, OD_F32_TILES)
    h1, h1f = h1.reshape(b, s, -1), h1f.reshape(b, s, -1)
    c_out = _conv(h1, od_dw_w[0], od_dw_b[0], od_conv_norm_g[0], od_conv_norm_b[0])
    d_out = _dilated(h1, h1f, od_dil_q_norm[0], od_dil_k_norm[0])
    x2 = _proj_out(x1, c_out.reshape(b * s, -1), d_out.reshape(b * s, -1), od_w_out[0].astype(bf16))
    return x2.reshape(b, s, d)
```

```python
import functools

import jax
import jax.numpy as jnp
import numpy as np
from jax import lax
from jax.experimental import pallas as pl
from jax.experimental.pallas import tpu as pltpu

f32 = jnp.float32
bf16 = jnp.bfloat16

D_MODEL = 2048
SEQ = 2048
HEAD_DIM = 128
SCALE = HEAD_DIM ** -0.5
NORM_EPS = 1e-6
NEG_INF = -1e30
LANE = 128
SUBLANE = 8

RET_HEADS = 8
RET_W = RET_HEADS * HEAD_DIM
RET_CHUNK = 256
RET_UNROLL = 8
ROPE_BASE = 10000.0
NSA_HEADS = 8
NSA_KV_HEADS = 2
NSA_REP = NSA_HEADS // NSA_KV_HEADS
NSA_N_BRANCH = 3
CMP_LEN = 32
CMP_STRIDE = 16
N_CMP = (SEQ - CMP_LEN) // CMP_STRIDE + 1
SEL_BLOCK = 64
N_SEL = SEQ // SEL_BLOCK
SEL_TOPK = 16
NSA_WINDOW = 512
NSA_TQ = 128
SEL_TK = 512
NSA_PREP_ROWS = 256
NSA_SELECT_TQ = 512
WIN_SPAN = NSA_WINDOW + NSA_TQ
CONV_W = 1024
CONV_K = 31
CONV_T = 512
CONV_HALO = 32
CONV_RB = 64
DIL_PATTERNS = ((128, 1), (512, 4), (2048, 16))
DIL_GROUP_HEADS = 4
DIL_HEADS = DIL_GROUP_HEADS * len(DIL_PATTERNS)
DIL_BLOCK = 128
DIL_UNROLL = 8

EV_RQ, EV_RK, EV_RV, EV_RZ = 0, 8, 16, 24
EV_NQ = 32
EV_KC, EV_VC, EV_KS, EV_VS, EV_KW, EV_VW = 40, 42, 44, 46, 48, 50
EV_NZ = 52
EV_NG = 60
EV_COLS = 8192
EV_F32_TILE = 5
EV_F32_KC, EV_F32_VC = 0, 2
OD_CU, OD_CG, OD_CZ = 0, 8, 16
OD_DQ0, OD_DK0, OD_DV0 = 24, 28, 32
OD_DZ = 36
OD_F32_TILE, OD_F32_TILES = 5, 3
OD_F32_Q, OD_F32_K, OD_F32_V = 0, 8, 16
OD_COLS = 8192

VMEM_LIMIT = 56 * 1024 * 1024


def _cparams(*sem):
    return pltpu.CompilerParams(dimension_semantics=sem, vmem_limit_bytes=VMEM_LIMIT)


def _rms(x, gain):
    ms = jnp.mean(x * x, axis=-1, keepdims=True)
    return x * lax.rsqrt(ms + NORM_EPS) * gain


def _silu(z):
    return z * jax.nn.sigmoid(z)


def _dot(a, b):
    return jnp.dot(a, b, preferred_element_type=f32)


def _dot_nt(a, b):
    return lax.dot_general(a, b, (((1,), (1,)), ((), ())), preferred_element_type=f32)


def _dot_tn(a, b):
    return lax.dot_general(a, b, (((0,), (0,)), ((), ())), preferred_element_type=f32)


PROJ_TM = 1024
PROJ_TN = 1024
NORM_ROWS = 256


EPI_NONE, EPI_RMS, EPI_ROT, EPI_ROT_SCALED = range(4)


def _proj_in_body(x_ref, g_ref, w_ref, pg_ref, cos_ref, sin_ref, o_ref, of_ref, xn_ref, *, f32_tile, f32_tiles, kinds):
    j = pl.program_id(1)

    @pl.when(j == 0)
    def _():
        for c in range(PROJ_TM // NORM_ROWS):
            rows = slice(c * NORM_ROWS, (c + 1) * NORM_ROWS)
            xn_ref[rows, :] = _rms(x_ref[rows, :], g_ref[...]).astype(bf16)

    def emit(tile):
        acc = _dot(xn_ref[...], w_ref[...])
        if tile is not None:
            blocks = []
            for c, kind in enumerate(kinds[tile]):
                lanes = slice(c * LANE, (c + 1) * LANE)
                blk = acc[:, lanes]
                if kind == EPI_RMS:
                    blk = _rms(blk, pg_ref[:, lanes])
                elif kind in (EPI_ROT, EPI_ROT_SCALED):
                    blk = blk * cos_ref[...] + pltpu.roll(blk, HEAD_DIM // 2, 1) * sin_ref[...]
                    if kind == EPI_ROT_SCALED:
                        blk = blk * SCALE
                blocks.append(blk)
            acc = jnp.concatenate(blocks, axis=1)
        o_ref[...] = acc.astype(o_ref.dtype)
        if tile is None:
            @pl.when((j >= f32_tile) & (j < f32_tile + f32_tiles))
            def _():
                of_ref[...] = acc
        elif f32_tile <= tile < f32_tile + f32_tiles:
            of_ref[...] = acc

    special = [t for t, k in enumerate(kinds) if any(k)]
    plain = j >= 0
    for t in special:
        pl.when(j == t)(functools.partial(emit, t))
        plain = plain & (j != t)
    pl.when(plain)(functools.partial(emit, None))


def _proj_in(x2, gain, w, f32_tile, f32_tiles, kinds, col_gain, cos_t, sin_t):
    m, d = x2.shape
    n = w.shape[1]
    assert len(kinds) == n // PROJ_TN and SEQ % PROJ_TM == 0
    return pl.pallas_call(
        functools.partial(_proj_in_body, f32_tile=f32_tile, f32_tiles=f32_tiles, kinds=kinds),
        grid=(m // PROJ_TM, n // PROJ_TN),
        in_specs=[
            pl.BlockSpec((PROJ_TM, d), lambda i, j: (i, 0)),
            pl.BlockSpec((1, d), lambda i, j: (0, 0)),
            pl.BlockSpec((d, PROJ_TN), lambda i, j: (0, j)),
            pl.BlockSpec((1, PROJ_TN), lambda i, j: (0, j)),
            pl.BlockSpec((PROJ_TM, HEAD_DIM), lambda i, j: (i % (SEQ // PROJ_TM), 0)),
            pl.BlockSpec((PROJ_TM, HEAD_DIM), lambda i, j: (i % (SEQ // PROJ_TM), 0)),
        ],
        out_specs=[pl.BlockSpec((PROJ_TM, PROJ_TN), lambda i, j: (i, j)),
                   pl.BlockSpec((PROJ_TM, PROJ_TN), lambda i, j: (i, jnp.clip(j - f32_tile, 0, f32_tiles - 1)))],
        out_shape=[jax.ShapeDtypeStruct((m, n), bf16), jax.ShapeDtypeStruct((m, f32_tiles * PROJ_TN), f32)],
        scratch_shapes=[pltpu.VMEM((PROJ_TM, d), bf16)],
        compiler_params=_cparams("parallel", "arbitrary"),
        name="proj_in",
    )(x2, gain.reshape(1, d), w, col_gain, cos_t, sin_t)


OUT_TM = 512


def _proj_out_body(x_ref, m0_ref, m1_ref, w0_ref, w1_ref, o_ref):
    o_ref[...] = x_ref[...] + _dot(m0_ref[...], w0_ref[...]) + _dot(m1_ref[...], w1_ref[...])


def _proj_out(x2, m0, m1, w):
    m, d = x2.shape
    k0, k1 = m0.shape[1], m1.shape[1]
    assert w.shape == (k0 + k1, d) and k0 % k1 == 0
    return pl.pallas_call(
        _proj_out_body,
        grid=(m // OUT_TM,),
        in_specs=[
            pl.BlockSpec((OUT_TM, d), lambda i: (i, 0)),
            pl.BlockSpec((OUT_TM, k0), lambda i: (i, 0)),
            pl.BlockSpec((OUT_TM, k1), lambda i: (i, 0)),
            pl.BlockSpec((k0, d), lambda i: (0, 0)),
            pl.BlockSpec((k1, d), lambda i: (k0 // k1, 0)),
        ],
        out_specs=pl.BlockSpec((OUT_TM, d), lambda i: (i, 0)),
        out_shape=jax.ShapeDtypeStruct((m, d), f32),
        compiler_params=_cparams("parallel"),
        name="proj_out",
    )(x2, m0, m1, w, w)


def _ret_body(q_ref, k_ref, v_ref, z_ref, dmask_ref, kdec_ref, qdec_ref, cdec_ref, g_ref, o_ref):
    c_len = RET_CHUNK

    def body(n, state):
        rows = pl.ds(pl.multiple_of(n * c_len, c_len), c_len)
        q = q_ref[rows, :]
        k = k_ref[rows, :]
        vb = v_ref[rows, :]
        s = _dot_nt(q, k) * dmask_ref[...]
        o = _dot(s.astype(bf16), vb)
        o = o + _dot((q.astype(f32) * qdec_ref[...]).astype(bf16), state.astype(bf16))
        kv = _dot_tn((k.astype(f32) * kdec_ref[...]).astype(bf16), vb)
        state = cdec_ref[...] * state + kv
        y = _rms(o, g_ref[...])
        o_ref[rows, :] = (y * _silu(z_ref[rows, :].astype(f32))).astype(o_ref.dtype)
        return state

    lax.fori_loop(0, SEQ // c_len, body, jnp.zeros((HEAD_DIM, HEAD_DIM), f32), unroll=RET_UNROLL)


def _rotary_tables():
    half = HEAD_DIM // 2
    pos = jnp.arange(SEQ)
    inv = ROPE_BASE ** (-jnp.arange(half, dtype=f32) / half)
    ang = pos.astype(f32)[:, None] * inv[None, :]
    cos, sin = jnp.cos(ang), jnp.sin(ang)
    return jnp.concatenate([cos, cos], axis=-1), jnp.concatenate([-sin, sin], axis=-1)


def _retention(h, ret_g):
    b = h.shape[0]
    c_len = RET_CHUNK
    log_g = jnp.log(1.0 - 2.0 ** (-5.0 - jnp.arange(RET_HEADS, dtype=f32)))
    idx = jnp.arange(c_len, dtype=f32)
    diff = idx[:, None] - idx[None, :]
    dmask = jnp.where(diff >= 0, jnp.exp(jnp.maximum(diff, 0.0)[None] * log_g[:, None, None]), 0.0)
    ones = jnp.ones((1, 1, HEAD_DIM), f32)
    k_dec = jnp.exp((c_len - 1 - idx)[None, :] * log_g[:, None])[:, :, None] * ones
    q_dec = jnp.exp((idx + 1.0)[None, :] * log_g[:, None])[:, :, None] * ones
    chunk_dec = jnp.exp(c_len * log_g)[:, None, None] * ones

    def col(off):
        return pl.BlockSpec((None, SEQ, HEAD_DIM), lambda bi, hi: (bi, 0, off + hi))

    def per_head(rows, cols=HEAD_DIM):
        return pl.BlockSpec((None, rows, cols), lambda bi, hi: (hi, 0, 0))

    return pl.pallas_call(
        _ret_body,
        grid=(b, RET_HEADS),
        in_specs=[col(EV_RQ), col(EV_RK), col(EV_RV), col(EV_RZ),
                  per_head(c_len, c_len), per_head(c_len), per_head(c_len), per_head(1), per_head(1)],
        out_specs=pl.BlockSpec((None, SEQ, HEAD_DIM), lambda bi, hi: (bi, 0, hi)),
        out_shape=jax.ShapeDtypeStruct((b, SEQ, RET_W), bf16),
        compiler_params=_cparams("parallel", "parallel"),
        name="retention",
    )(h, h, h, h, dmask, k_dec, q_dec, chunk_dec, ret_g.reshape(RET_HEADS, 1, HEAD_DIM))


def _nsa_body(q_ref, kc_ref, vc_ref, ks_ref, vs_ref, kw_ref, vw_ref, nz_ref, ng_ref,
              kg_ref, posk_ref, posv_ref, w1k_ref, w2k_ref, w1v_ref, w2v_ref, ovl_ref, gsel_ref, eye_ref,
              o_ref, kcmp_s, vcmp_s, ksa_s, vsa_s, kwa_s, vwa_s, not_sel_s):
    tq = NSA_TQ
    sq = NSA_SELECT_TQ
    rep = NSA_REP
    half_blk = CMP_LEN // 2
    d = HEAD_DIM
    exp2_scale = SCALE * float(np.log2(np.e))

    def compress(t_ref, pos_ref, w1_ref, w2_ref):
        first = jnp.zeros((SEQ // CMP_STRIDE, d), f32)
        second = jnp.zeros((SEQ // CMP_STRIDE, d), f32)
        for i in range(half_blk):
            ti = t_ref[pl.ds(i, SEQ // CMP_STRIDE, stride=CMP_STRIDE), :]
            first = first + _dot((ti + pos_ref[i:i + 1, :]).astype(bf16), w1_ref[i * d:(i + 1) * d, :])
            j = half_blk + i
            second = second + _dot((ti + pos_ref[j:j + 1, :]).astype(bf16), w1_ref[j * d:(j + 1) * d, :])
        pre = first + pltpu.roll(second, SEQ // CMP_STRIDE - 1, 0)
        return _dot(jax.nn.gelu(pre).astype(bf16), w2_ref[...])

    kcmp_s[...] = _rms(compress(kc_ref, posk_ref, w1k_ref, w2k_ref), kg_ref[0:1, :]).astype(bf16)
    vcmp_s[...] = compress(vc_ref, posv_ref, w1v_ref, w2v_ref).astype(bf16)
    lane0 = lax.broadcasted_iota(jnp.int32, (tq, d), 1) == 0
    pc_rows = NSA_PREP_ROWS
    ones = jnp.ones((pc_rows, d), bf16)
    zeros = jnp.zeros((pc_rows, d), bf16)
    for c in range(NSA_WINDOW // pc_rows):
        rows = slice(c * pc_rows, (c + 1) * pc_rows)
        kwa_s[rows, 0:d] = zeros
        kwa_s[rows, d:2 * d] = jnp.where(lax.broadcasted_iota(jnp.int32, (pc_rows, d), 1) == 0, NEG_INF, 0.0).astype(bf16)
        vwa_s[rows, 0:d] = zeros
        vwa_s[rows, d:2 * d] = zeros
    for c in range(SEQ // pc_rows):
        rows = slice(c * pc_rows, (c + 1) * pc_rows)
        wrows = slice(NSA_WINDOW + c * pc_rows, NSA_WINDOW + (c + 1) * pc_rows)
        ksa_s[rows, 0:d] = ks_ref[rows, :]
        ksa_s[rows, d:2 * d] = gsel_ref[rows, :]
        vsa_s[rows, 0:d] = vs_ref[rows, :]
        vsa_s[rows, d:2 * d] = ones
        kwa_s[wrows, 0:d] = kw_ref[rows, :]
        kwa_s[wrows, d:2 * d] = zeros
        vwa_s[wrows, 0:d] = vw_ref[rows, :]
        vwa_s[wrows, d:2 * d] = ones

    row_i = lax.broadcasted_iota(jnp.int32, (tq, tq), 0)
    col_i = lax.broadcasted_iota(jnp.int32, (tq, tq), 1)
    band_first = jnp.where(col_i > row_i, 0.0, NEG_INF)
    band_last = jnp.where(col_i <= row_i, 0.0, NEG_INF)
    pad_flag = jnp.where(lane0, 1.0, 0.0).astype(bf16)

    def softmax_update(s, m, acc, v):
        n = s.shape[-1]
        m_new = jnp.maximum(m, jnp.max(s, axis=-1, keepdims=True))
        alpha = jnp.exp2((m - m_new) * exp2_scale)
        e = jnp.exp2((s - m_new) * exp2_scale)
        pv = _dot(e.reshape(rep * tq, n).astype(bf16), v).reshape(rep, tq, 2 * d)
        return m_new, alpha * acc + pv

    def select_tile(i, carry):
        t0 = i * sq
        rows = pl.ds(pl.multiple_of(t0, sq), sq)
        qs = [q_ref[rows, r * d:(r + 1) * d] for r in range(rep)]
        tpos = t0 + lax.broadcasted_iota(jnp.int32, (sq, 1), 0)
        qst = jnp.concatenate(qs, axis=0)
        n_idx = lax.broadcasted_iota(jnp.int32, (1, LANE), 1)
        valid = (n_idx * CMP_STRIDE + (CMP_LEN - 1) <= tpos) & (n_idx < N_CMP)
        cbias = jnp.where(valid, 0.0, NEG_INF)
        sc = (_dot_nt(qst, kcmp_s[...]) * SCALE).reshape(rep, sq, LANE) + cbias[None]
        mc = jnp.max(sc, axis=-1, keepdims=True)
        ec = jnp.exp(sc - mc)
        any_valid = jnp.where(tpos >= CMP_LEN - 1, 1.0, 0.0)
        pc = ec / jnp.sum(ec, axis=-1, keepdims=True) * any_valid[None]
        o_cmp = _dot(pc.reshape(rep * sq, LANE).astype(bf16), vcmp_s[...]).reshape(rep, sq, d)
        gates = jax.nn.sigmoid(ng_ref[rows, :].astype(f32))
        for r in range(rep):
            o_ref[rows, r * d:(r + 1) * d] = (gates[:, r:r + 1] * o_cmp[r]).astype(o_ref.dtype)
        psum = pc[0]
        for r in range(1, rep):
            psum = psum + pc[r]

        p_hi = psum.astype(bf16)
        rest = psum - p_hi.astype(f32)
        p_mid = rest.astype(bf16)
        p_lo = (rest - p_mid.astype(f32)).astype(bf16)
        imp = (_dot_nt(ovl_ref[...], p_hi) + _dot_nt(ovl_ref[...], p_mid) + _dot_nt(ovl_ref[...], p_lo))[:N_SEL, :]
        jj = lax.broadcasted_iota(jnp.int32, (N_SEL, 1), 0)
        cur = (t0 + lax.broadcasted_iota(jnp.int32, (1, sq), 1)) // SEL_BLOCK
        forced = (jj == 0) | (jj == cur) | (jj == cur - 1)
        imp = jnp.where(forced, -NEG_INF, jnp.where(jj > cur, NEG_INF, imp))
        rank = jnp.zeros((N_SEL, sq), f32)
        for j2 in range(N_SEL):
            row = imp[j2:j2 + 1, :]
            beats = (row > imp) | ((row == imp) & (j2 < jj))
            rank = rank + jnp.where(beats, 1.0, 0.0)
        not_sel_t = jnp.where(rank < SEL_TOPK, 0.0, 1.0).astype(bf16)
        not_sel_s[rows, :] = _dot_tn(not_sel_t, eye_ref[...]).astype(bf16)
        return carry

    lax.fori_loop(0, SEQ // sq, select_tile, 0)

    def make_tile(n_full):
        def tile(i, carry):
            t0 = i * tq
            rows = pl.ds(pl.multiple_of(t0, tq), tq)
            qs = [q_ref[rows, r * d:(r + 1) * d] for r in range(rep)]
            tpos = t0 + lax.broadcasted_iota(jnp.int32, (tq, 1), 0)
            m_init = jnp.full((rep, tq, 1), NEG_INF, f32)
            acc_init = jnp.zeros((rep, tq, 2 * d), f32)

            qw = jnp.concatenate([jnp.concatenate([q, pad_flag], axis=1) for q in qs], axis=0)
            wrows = pl.ds(pl.multiple_of(t0, tq), WIN_SPAN)
            sw = _dot_nt(qw, kwa_s[wrows, :]).reshape(rep, tq, WIN_SPAN)
            sw = jnp.concatenate([sw[:, :, :tq] + band_first[None], sw[:, :, tq:NSA_WINDOW],
                                  sw[:, :, NSA_WINDOW:] + band_last[None]], axis=2)
            _, acc_w = softmax_update(sw, m_init, acc_init, vwa_s[wrows, :])
            o_win = acc_w[:, :, :d] / acc_w[:, :, d:]

            not_sel = not_sel_s[rows, :]
            qsel = jnp.concatenate([jnp.concatenate([q, not_sel], axis=1) for q in qs], axis=0)
            m_s, acc_s = m_init, acc_init
            for kt in range(n_full + 1):
                krows = slice(kt * SEL_TK, (kt + 1) * SEL_TK)
                s = _dot_nt(qsel, ksa_s[krows, :]).reshape(rep, tq, SEL_TK)
                if kt == n_full:
                    kpos = kt * SEL_TK + lax.broadcasted_iota(jnp.int32, (1, SEL_TK), 1)
                    s = s + jnp.where(kpos <= tpos, 0.0, NEG_INF)[None]
                m_s, acc_s = softmax_update(s, m_s, acc_s, vsa_s[krows, :])
            o_sel = acc_s[:, :, :d] / acc_s[:, :, d:]

            gates = jax.nn.sigmoid(ng_ref[rows, :].astype(f32))
            for r in range(rep):
                g_s = gates[:, rep + r:rep + r + 1]
                g_w = gates[:, 2 * rep + r:2 * rep + r + 1]
                cols = slice(r * d, (r + 1) * d)
                o = o_ref[rows, cols].astype(f32) + g_s * o_sel[r] + g_w * o_win[r]
                o_ref[rows, cols] = (o * _silu(nz_ref[rows, cols].astype(f32))).astype(o_ref.dtype)
            return carry
        return tile

    tiles_per_key_tile = SEL_TK // tq
    for n_full in range(SEQ // SEL_TK):
        lax.fori_loop(n_full * tiles_per_key_tile, (n_full + 1) * tiles_per_key_tile, make_tile(n_full), 0)


def _nsa(h, hf, k_g, pos_k, pos_v, k_w1, k_w2, v_w1, v_w2):
    b = h.shape[0]
    grp_w = NSA_REP * HEAD_DIM
    cs = np.arange(LANE) * CMP_STRIDE
    ss = np.arange(LANE) * SEL_BLOCK
    ovl = ((cs[None, :] < ss[:, None] + SEL_BLOCK) & (cs[None, :] + CMP_LEN > ss[:, None])
           & (np.arange(LANE)[None, :] < N_CMP) & (np.arange(LANE)[:, None] < N_SEL))
    ovl_t = jnp.asarray(ovl, dtype=bf16)
    key_blk = np.arange(SEQ) // SEL_BLOCK
    gsel = jnp.asarray(np.where(key_blk[:, None] == np.arange(LANE)[None, :], NEG_INF, 0.0), dtype=bf16)
    eye = jnp.asarray(np.eye(N_SEL, LANE), dtype=bf16)

    def wide(off):
        return pl.BlockSpec((None, SEQ, grp_w), lambda bi, gi: (bi, 0, off // NSA_REP + gi))

    def col(off):
        return pl.BlockSpec((None, SEQ, HEAD_DIM), lambda bi, gi: (bi, 0, off + gi))

    def full(shape):
        return pl.BlockSpec(shape, lambda bi, gi: (0,) * len(shape))

    flat = CMP_LEN * HEAD_DIM
    return pl.pallas_call(
        _nsa_body,
        grid=(b, NSA_KV_HEADS),
        in_specs=[wide(EV_NQ), col(EV_F32_KC), col(EV_F32_VC), col(EV_KS), col(EV_VS), col(EV_KW), col(EV_VW),
                  wide(EV_NZ), col(EV_NG),
                  full((NSA_N_BRANCH, HEAD_DIM)), full((CMP_LEN, HEAD_DIM)), full((CMP_LEN, HEAD_DIM)),
                  full((flat, HEAD_DIM)), full((HEAD_DIM, HEAD_DIM)), full((flat, HEAD_DIM)), full((HEAD_DIM, HEAD_DIM)),
                  full((LANE, LANE)), full((SEQ, LANE)), full((N_SEL, LANE))],
        out_specs=pl.BlockSpec((None, SEQ, grp_w), lambda bi, gi: (bi, 0, gi)),
        out_shape=jax.ShapeDtypeStruct((b, SEQ, NSA_HEADS * HEAD_DIM), bf16),
        scratch_shapes=[pltpu.VMEM((SEQ // CMP_STRIDE, HEAD_DIM), bf16), pltpu.VMEM((SEQ // CMP_STRIDE, HEAD_DIM), bf16),
                        pltpu.VMEM((SEQ, 2 * HEAD_DIM), bf16), pltpu.VMEM((SEQ, 2 * HEAD_DIM), bf16),
                        pltpu.VMEM((SEQ + NSA_WINDOW, 2 * HEAD_DIM), bf16), pltpu.VMEM((SEQ + NSA_WINDOW, 2 * HEAD_DIM), bf16),
                        pltpu.VMEM((SEQ, LANE), bf16)],
        compiler_params=_cparams("parallel", "parallel"),
        name="nsa",
    )(h, hf, hf, h, h, h, h, h, h,
      k_g, pos_k, pos_v,
      k_w1.astype(bf16), k_w2.astype(bf16), v_w1.astype(bf16), v_w2.astype(bf16), ovl_t, gsel, eye)


def _conv_body(u_ref, g_ref, z_ref, w_ref, b_ref, lg_ref, lb_ref, o_ref, a_ext, a_sh, y_s):
    ext = CONV_HALO + CONV_T
    n_cb = CONV_W // LANE
    lead = CONV_HALO - (CONV_K - 1)

    @pl.when(pl.program_id(1) == 0)
    def _():
        for cb in range(n_cb):
            a_ext[cb, 0:CONV_HALO, :] = jnp.zeros((CONV_HALO, LANE), f32)
            a_ext[cb, ext:ext + SUBLANE, :] = jnp.zeros((SUBLANE, LANE), f32)

    for cb in range(n_cb):
        lanes = slice(cb * LANE, (cb + 1) * LANE)
        for c in range(CONV_T // 128):
            rows = slice(c * 128, (c + 1) * 128)
            a_ext[cb, CONV_HALO + c * 128:CONV_HALO + (c + 1) * 128, :] = u_ref[rows, lanes].astype(f32) * jax.nn.sigmoid(g_ref[rows, lanes].astype(f32))
        for r in range(1, SUBLANE):
            for c in range(ext // CONV_HALO):
                a_sh[r - 1, cb, c * CONV_HALO:(c + 1) * CONV_HALO, :] = a_ext[cb, c * CONV_HALO + r:(c + 1) * CONV_HALO + r, :]

        taps_w = [jnp.broadcast_to(w_ref[k:k + 1, lanes], (CONV_RB, LANE)) for k in range(CONV_K)]
        bias = jnp.broadcast_to(b_ref[:, lanes], (CONV_RB, LANE))

        def conv_rows(rb, carry, cb=cb, taps_w=taps_w, bias=bias):
            r0 = pl.multiple_of(rb * CONV_RB, CONV_RB)
            parts = [bias, None]
            for k in range(CONV_K):
                shift, base = (lead + k) % SUBLANE, (lead + k) // SUBLANE * SUBLANE
                rows = pl.ds(r0 + base, CONV_RB)
                tap = a_ext[cb, rows, :] if shift == 0 else a_sh[shift - 1, cb, rows, :]
                term = tap * taps_w[k]
                parts[k % 2] = term if parts[k % 2] is None else parts[k % 2] + term
            y_s[cb, pl.ds(r0, CONV_RB), :] = parts[0] + parts[1]
            return carry

        lax.fori_loop(0, CONV_T // CONV_RB, conv_rows, 0)
        a_ext[cb, 0:CONV_HALO, :] = a_ext[cb, CONV_T:CONV_T + CONV_HALO, :]

    def norm_rows(rb, carry):
        rows = pl.ds(pl.multiple_of(rb * CONV_RB, CONV_RB), CONV_RB)
        y = jnp.concatenate([y_s[cb, rows, :] for cb in range(n_cb)], axis=1)
        mu = jnp.mean(y, axis=-1, keepdims=True)
        var = jnp.mean(jnp.square(y - mu), axis=-1, keepdims=True)
        yn = (y - mu) * lax.rsqrt(var + NORM_EPS) * lg_ref[...] + lb_ref[...]
        o_ref[rows, :] = (_silu(yn) * _silu(z_ref[rows, :].astype(f32))).astype(o_ref.dtype)
        return carry

    lax.fori_loop(0, CONV_T // CONV_RB, norm_rows, 0, unroll=2)


def _conv(h, dw_w, dw_b, ln_g, ln_b):
    b = h.shape[0]

    def col(off):
        return pl.BlockSpec((None, CONV_T, CONV_W), lambda bi, ti: (bi, ti, off * LANE // CONV_W))

    def full(rows):
        return pl.BlockSpec((rows, CONV_W), lambda bi, ti: (0, 0))

    return pl.pallas_call(
        _conv_body,
        grid=(b, SEQ // CONV_T),
        in_specs=[col(OD_CU), col(OD_CG), col(OD_CZ), full(CONV_K), full(1), full(1), full(1)],
        out_specs=pl.BlockSpec((None, CONV_T, CONV_W), lambda bi, ti: (bi, ti, 0)),
        out_shape=jax.ShapeDtypeStruct((b, SEQ, CONV_W), bf16),
        scratch_shapes=[pltpu.VMEM((CONV_W // LANE, CONV_HALO + CONV_T + SUBLANE, LANE), f32),
                        pltpu.VMEM((SUBLANE - 1, CONV_W // LANE, CONV_HALO + CONV_T, LANE), f32),
                        pltpu.VMEM((CONV_W // LANE, CONV_T, LANE), f32)],
        compiler_params=_cparams("parallel", "arbitrary"),
        name="conformer_conv",
    )(h, h, h, dw_w, dw_b.reshape(1, CONV_W), ln_g.reshape(1, CONV_W), ln_b.reshape(1, CONV_W))


def _dil_body(q0_ref, k0_ref, v0_ref, q1_ref, k1_ref, v1_ref, q2_ref, k2_ref, v2_ref, z_ref,
              o_ref, qn_s, kn_s, va_s, o0_s, o1_s, o2_s, l0_s, l1_s, l2_s):
    blk = DIL_BLOCK
    n_tiles = SEQ // blk
    groups = ((q0_ref, k0_ref, v0_ref, o0_s, l0_s), (q1_ref, k1_ref, v1_ref, o1_s, l1_s), (q2_ref, k2_ref, v2_ref, o2_s, l2_s))
    kn_s[0:blk, :] = jnp.zeros((blk, HEAD_DIM), bf16)
    va_s[0:blk, :] = jnp.zeros((blk, 2 * HEAD_DIM), bf16)
    ones = jnp.ones((blk, HEAD_DIM), bf16)
    row2 = lax.broadcasted_iota(jnp.int32, (blk, 2 * blk), 0)
    col2 = lax.broadcasted_iota(jnp.int32, (blk, 2 * blk), 1)
    band_prev = jnp.where((col2 >= row2) & (col2 <= row2 + blk), 0.0, NEG_INF)
    band_first = jnp.where((col2 >= blk) & (col2 <= row2 + blk), 0.0, NEG_INF)

    for (q_ref, k_ref, v_ref, o_s, l_s), (window, dil) in zip(groups, DIL_PATTERNS):
        assert window // dil == blk
        nb = SEQ // dil // blk

        def natural_rows(u, dil=dil, nb=nb):
            start = u // nb + (u % nb) * (blk * dil)
            return pl.ds(start, blk, stride=dil) if dil > 1 else pl.ds(pl.multiple_of(start, blk), blk)

        def prep(u, carry, q_ref=q_ref, k_ref=k_ref, v_ref=v_ref, natural_rows=natural_rows):
            src = natural_rows(u)
            qn_s[pl.ds(pl.multiple_of(u * blk, blk), blk), :] = q_ref[src, :].astype(bf16)
            dst = pl.ds(pl.multiple_of((u + 1) * blk, blk), blk)
            kn_s[dst, :] = k_ref[src, :].astype(bf16)
            va_s[dst, :] = jnp.concatenate([v_ref[src, :].astype(bf16), ones], axis=1)
            return carry

        lax.fori_loop(0, n_tiles, prep, 0, unroll=DIL_UNROLL)

        def att(u, carry, o_s=o_s, l_s=l_s, nb=nb, natural_rows=natural_rows):
            q = qn_s[pl.ds(pl.multiple_of(u * blk, blk), blk), :]
            keys = pl.ds(pl.multiple_of(u * blk, blk), 2 * blk)
            band = jnp.where((u % nb) > 0, band_prev, band_first)
            s = _dot_nt(q, kn_s[keys, :]) * SCALE + band
            m = jnp.max(s, axis=-1, keepdims=True)
            pv = _dot(jnp.exp(s - m).astype(bf16), va_s[keys, :])
            den = pv[:, HEAD_DIM:]
            dst = natural_rows(u)
            o_s[dst, :] = pv[:, :HEAD_DIM] / den
            l_s[dst, :] = m + jnp.log(den)
            return carry

        lax.fori_loop(0, n_tiles, att, 0, unroll=DIL_UNROLL)

    def mix(c, carry):
        rows = pl.ds(pl.multiple_of(c * blk, blk), blk)
        l0, l1, l2 = l0_s[rows, :], l1_s[rows, :], l2_s[rows, :]
        m = jnp.maximum(jnp.maximum(l0, l1), l2)
        w0, w1, w2 = jnp.exp(l0 - m), jnp.exp(l1 - m), jnp.exp(l2 - m)
        o = (w0 * o0_s[rows, :] + w1 * o1_s[rows, :] + w2 * o2_s[rows, :]) / (w0 + w1 + w2)
        o_ref[rows, :] = (o * _silu(z_ref[rows, :].astype(f32))).astype(o_ref.dtype)
        return carry

    lax.fori_loop(0, n_tiles, mix, 0, unroll=DIL_UNROLL)


def _dilated(h, hf):
    b = h.shape[0]

    def col(off):
        return pl.BlockSpec((None, SEQ, HEAD_DIM), lambda bi, hi: (bi, 0, off + hi))

    specs = [col(OD_DQ0), col(OD_DK0), col(OD_DV0)]
    for gi in range(len(DIL_PATTERNS) - 1):
        specs += [col(OD_F32_Q + gi * DIL_GROUP_HEADS), col(OD_F32_K + gi * DIL_GROUP_HEADS), col(OD_F32_V + gi * DIL_GROUP_HEADS)]
    return pl.pallas_call(
        _dil_body,
        grid=(b, DIL_GROUP_HEADS),
        in_specs=specs + [col(OD_DZ)],
        out_specs=pl.BlockSpec((None, SEQ, HEAD_DIM), lambda bi, hi: (bi, 0, hi)),
        out_shape=jax.ShapeDtypeStruct((b, SEQ, DIL_GROUP_HEADS * HEAD_DIM), bf16),
        scratch_shapes=[pltpu.VMEM((SEQ, HEAD_DIM), bf16), pltpu.VMEM((SEQ + DIL_BLOCK, HEAD_DIM), bf16),
                        pltpu.VMEM((SEQ + DIL_BLOCK, 2 * HEAD_DIM), bf16)] + [pltpu.VMEM((SEQ, HEAD_DIM), f32)] * 6,
        compiler_params=_cparams("parallel", "parallel"),
        name="dilated",
    )(h, h, h, hf, hf, hf, hf, hf, hf, h)


def _even_w_in(w):
    ng0 = EV_NZ * LANE
    nz0 = ng0 + NSA_N_BRANCH * NSA_HEADS
    ng = w[:, ng0:nz0].reshape(-1, NSA_N_BRANCH, NSA_KV_HEADS, NSA_REP)
    ng = ng.transpose(0, 2, 1, 3).reshape(-1, NSA_KV_HEADS, NSA_N_BRANCH * NSA_REP)
    ng = jnp.pad(ng, ((0, 0), (0, 0), (0, LANE - NSA_N_BRANCH * NSA_REP))).reshape(-1, NSA_KV_HEADS * LANE)
    out = jnp.concatenate([w[:, :ng0], w[:, nz0:], ng], axis=1)
    return jnp.pad(out, ((0, 0), (0, EV_COLS - out.shape[1]))).astype(bf16)


def _odd_w_in(w):
    conv_w = 3 * CONV_W
    qkv_w = DIL_HEADS * HEAD_DIM
    g0 = DIL_GROUP_HEADS * HEAD_DIM
    q, k, v = (w[:, conv_w + i * qkv_w:conv_w + (i + 1) * qkv_w] for i in range(3))
    dz = w[:, conv_w + 3 * qkv_w:]
    out = jnp.concatenate([w[:, :conv_w], q[:, :g0], k[:, :g0], v[:, :g0], dz, q[:, g0:], k[:, g0:], v[:, g0:]], axis=1)
    return out.astype(bf16)


def _tile_table(entries, fill):
    n_blk = EV_COLS // LANE
    flat = [fill] * n_blk
    for off, (count, value) in entries.items():
        flat[off:off + count] = [value] * count
    per_tile = PROJ_TN // LANE
    return tuple(tuple(flat[t * per_tile:(t + 1) * per_tile]) for t in range(n_blk // per_tile))


def _col_gain(entries):
    g = jnp.ones((EV_COLS // LANE, HEAD_DIM), f32)
    for off, (count, gain) in entries.items():
        g = g.at[off:off + count].set(jnp.broadcast_to(gain, (count, HEAD_DIM)))
    return g.reshape(1, EV_COLS)


def kernel(x, ev_norm, ev_w_in, ev_w_out, ev_ret_norm, ev_nsa_q_norm, ev_nsa_k_norm, ev_cmp_pos_k, ev_cmp_pos_v, ev_cmp_k_w1, ev_cmp_k_w2, ev_cmp_v_w1, ev_cmp_v_w2, od_norm, od_w_in, od_w_out, od_dw_w, od_dw_b, od_conv_norm_g, od_conv_norm_b, od_dil_q_norm, od_dil_k_norm):
    b, s, d = x.shape
    assert (s, d) == (SEQ, D_MODEL) and EV_COLS == OD_COLS
    x0 = x.reshape(b * s, d)
    cos_t, sin_t = _rotary_tables()

    ev_kinds = _tile_table({EV_RQ: (RET_HEADS, EPI_ROT), EV_RK: (RET_HEADS, EPI_ROT_SCALED), EV_NQ: (NSA_HEADS, EPI_RMS),
                            EV_KS: (NSA_KV_HEADS, EPI_RMS), EV_KW: (NSA_KV_HEADS, EPI_RMS)}, EPI_NONE)
    ev_gain = _col_gain({EV_NQ: (NSA_HEADS, ev_nsa_q_norm[0]), EV_KS: (NSA_KV_HEADS, ev_nsa_k_norm[0][1]),
                         EV_KW: (NSA_KV_HEADS, ev_nsa_k_norm[0][2])})
    h0, h0f = _proj_in(x0, ev_norm[0], _even_w_in(ev_w_in[0]), EV_F32_TILE, 1, ev_kinds, ev_gain, cos_t, sin_t)
    h0, h0f = h0.reshape(b, s, -1), h0f.reshape(b, s, -1)
    a_out = _retention(h0, ev_ret_norm[0])
    b_out = _nsa(h0, h0f, ev_nsa_k_norm[0], ev_cmp_pos_k[0], ev_cmp_pos_v[0],
                 ev_cmp_k_w1[0], ev_cmp_k_w2[0], ev_cmp_v_w1[0], ev_cmp_v_w2[0])
    x1 = _proj_out(x0, a_out.reshape(b * s, -1), b_out.reshape(b * s, -1), ev_w_out[0].astype(bf16))

    f32_q, f32_k = OD_F32_TILE * (PROJ_TN // LANE) + OD_F32_Q, OD_F32_TILE * (PROJ_TN // LANE) + OD_F32_K
    strided_heads = DIL_HEADS - DIL_GROUP_HEADS
    od_norm_blocks = {OD_DQ0: (DIL_GROUP_HEADS, od_dil_q_norm[0]), OD_DK0: (DIL_GROUP_HEADS, od_dil_k_norm[0]),
                      f32_q: (strided_heads, od_dil_q_norm[0]), f32_k: (strided_heads, od_dil_k_norm[0])}
    od_kinds = _tile_table({off: (count, EPI_RMS) for off, (count, _) in od_norm_blocks.items()}, EPI_NONE)
    h1, h1f = _proj_in(x1, od_norm[0], _odd_w_in(od_w_in[0]), OD_F32_TILE, OD_F32_TILES, od_kinds,
                       _col_gain(od_norm_blocks), cos_t, sin_t)
    h1, h1f = h1.reshape(b, s, -1), h1f.reshape(b, s, -1)
    c_out = _conv(h1, od_dw_w[0], od_dw_b[0], od_conv_norm_g[0], od_conv_norm_b[0])
    d_out = _dilated(h1, h1f)
    x2 = _proj_out(x1, c_out.reshape(b * s, -1), d_out.reshape(b * s, -1), od_w_out[0].astype(bf16))
    return x2.reshape(b, s, d)
```

```python
import functools

import jax
import jax.numpy as jnp
import numpy as np
from jax import lax
from jax.experimental import pallas as pl
from jax.experimental.pallas import tpu as pltpu

f32 = jnp.float32
bf16 = jnp.bfloat16

D_MODEL = 2048
SEQ = 2048
HEAD_DIM = 128
SCALE = HEAD_DIM ** -0.5
NORM_EPS = 1e-6
NEG_INF = -1e30
LANE = 128
SUBLANE = 8

RET_HEADS = 8
RET_W = RET_HEADS * HEAD_DIM
RET_CHUNK = 256
RET_UNROLL = 8
ROPE_BASE = 10000.0
NSA_HEADS = 8
NSA_KV_HEADS = 2
NSA_REP = NSA_HEADS // NSA_KV_HEADS
NSA_N_BRANCH = 3
CMP_LEN = 32
CMP_STRIDE = 16
N_CMP = (SEQ - CMP_LEN) // CMP_STRIDE + 1
SEL_BLOCK = 64
N_SEL = SEQ // SEL_BLOCK
SEL_TOPK = 16
NSA_WINDOW = 512
NSA_TQ = 128
SEL_TK = 512
NSA_PREP_ROWS = 256
NSA_SELECT_TQ = 512
WIN_SPAN = NSA_WINDOW + NSA_TQ
CONV_W = 1024
CONV_K = 31
CONV_T = 512
CONV_HALO = 32
CONV_RB = 64
DIL_PATTERNS = ((128, 1), (512, 4), (2048, 16))
DIL_GROUP_HEADS = 4
DIL_HEADS = DIL_GROUP_HEADS * len(DIL_PATTERNS)
DIL_BLOCK = 128
DIL_UNROLL = 8

EV_RQ, EV_RK, EV_RV, EV_RZ = 0, 8, 16, 24
EV_NQ = 32
EV_W_KC, EV_W_VC, EV_W_KS, EV_W_VS, EV_W_KW, EV_W_VW = 40, 42, 44, 46, 48, 50
EV_W_NZ = 52
EV_COLS = 8192
EV_F32_TILE = 5
EV_F32_KC, EV_F32_VC, EV_F32_KS, EV_F32_VS = 0, 2, 4, 6
EV_KW, EV_VW = 40, 42
EV_NZ = 44
EV_NG = 52
OD_CU, OD_CG, OD_CZ = 0, 8, 16
OD_DQ0, OD_DK0, OD_DV0 = 24, 28, 32
OD_DZ = 36
OD_F32_TILE, OD_F32_TILES = 5, 3
OD_F32_Q, OD_F32_K, OD_F32_V = 0, 8, 16
OD_COLS = 8192

VMEM_LIMIT = 56 * 1024 * 1024


def _cparams(*sem):
    return pltpu.CompilerParams(dimension_semantics=sem, vmem_limit_bytes=VMEM_LIMIT)


def _rms(x, gain):
    ms = jnp.mean(x * x, axis=-1, keepdims=True)
    return x * lax.rsqrt(ms + NORM_EPS) * gain


def _silu(z):
    return z * jax.nn.sigmoid(z)


def _dot(a, b):
    return jnp.dot(a, b, preferred_element_type=f32)


def _dot_nt(a, b):
    return lax.dot_general(a, b, (((1,), (1,)), ((), ())), preferred_element_type=f32)


def _dot_tn(a, b):
    return lax.dot_general(a, b, (((0,), (0,)), ((), ())), preferred_element_type=f32)


PROJ_TM = 1024
PROJ_TN = 1024
NORM_ROWS = 256


EPI_NONE, EPI_RMS, EPI_ROT, EPI_ROT_SCALED = range(4)


def _proj_in_body(x_ref, g_ref, w_ref, pg_ref, cos_ref, sin_ref, o_ref, of_ref, xn_ref, *, f32_tile, f32_tiles, kinds):
    j = pl.program_id(1)

    @pl.when(j == 0)
    def _():
        for c in range(PROJ_TM // NORM_ROWS):
            rows = slice(c * NORM_ROWS, (c + 1) * NORM_ROWS)
            xn_ref[rows, :] = _rms(x_ref[rows, :], g_ref[...]).astype(bf16)

    def emit(tile):
        acc = _dot(xn_ref[...], w_ref[...])
        if tile is not None:
            blocks = []
            for c, kind in enumerate(kinds[tile]):
                lanes = slice(c * LANE, (c + 1) * LANE)
                blk = acc[:, lanes]
                if kind == EPI_RMS:
                    blk = _rms(blk, pg_ref[:, lanes])
                elif kind in (EPI_ROT, EPI_ROT_SCALED):
                    blk = blk * cos_ref[...] + pltpu.roll(blk, HEAD_DIM // 2, 1) * sin_ref[...]
                    if kind == EPI_ROT_SCALED:
                        blk = blk * SCALE
                blocks.append(blk)
            acc = jnp.concatenate(blocks, axis=1)
        if tile is not None and f32_tile <= tile < f32_tile + f32_tiles:
            of_ref[...] = acc
        else:
            o_ref[...] = acc.astype(o_ref.dtype)

    special = [t for t, k in enumerate(kinds) if any(k) or f32_tile <= t < f32_tile + f32_tiles]
    plain = j >= 0
    for t in special:
        pl.when(j == t)(functools.partial(emit, t))
        plain = plain & (j != t)
    pl.when(plain)(functools.partial(emit, None))


def _proj_in(x2, gain, w, f32_tile, f32_tiles, kinds, col_gain, cos_t, sin_t):
    m, d = x2.shape
    n = w.shape[1]
    assert len(kinds) == n // PROJ_TN and SEQ % PROJ_TM == 0 and f32_tile >= 1
    return pl.pallas_call(
        functools.partial(_proj_in_body, f32_tile=f32_tile, f32_tiles=f32_tiles, kinds=kinds),
        grid=(m // PROJ_TM, n // PROJ_TN),
        in_specs=[
            pl.BlockSpec((PROJ_TM, d), lambda i, j: (i, 0)),
            pl.BlockSpec((1, d), lambda i, j: (0, 0)),
            pl.BlockSpec((d, PROJ_TN), lambda i, j: (0, j)),
            pl.BlockSpec((1, PROJ_TN), lambda i, j: (0, j)),
            pl.BlockSpec((PROJ_TM, HEAD_DIM), lambda i, j: (i % (SEQ // PROJ_TM), 0)),
            pl.BlockSpec((PROJ_TM, HEAD_DIM), lambda i, j: (i % (SEQ // PROJ_TM), 0)),
        ],
        out_specs=[pl.BlockSpec((PROJ_TM, PROJ_TN), lambda i, j: (i, j - jnp.clip(j - f32_tile + 1, 0, f32_tiles))),
                   pl.BlockSpec((PROJ_TM, PROJ_TN), lambda i, j: (i, jnp.clip(j - f32_tile, 0, f32_tiles - 1)))],
        out_shape=[jax.ShapeDtypeStruct((m, n - f32_tiles * PROJ_TN), bf16),
                   jax.ShapeDtypeStruct((m, f32_tiles * PROJ_TN), f32)],
        scratch_shapes=[pltpu.VMEM((PROJ_TM, d), bf16)],
        compiler_params=_cparams("parallel", "arbitrary"),
        name="proj_in",
    )(x2, gain.reshape(1, d), w, col_gain, cos_t, sin_t)


OUT_TM = 512


def _proj_out_body(x_ref, m0_ref, m1_ref, w0_ref, w1_ref, o_ref):
    o_ref[...] = x_ref[...] + _dot(m0_ref[...], w0_ref[...]) + _dot(m1_ref[...], w1_ref[...])


def _proj_out(x2, m0, m1, w):
    m, d = x2.shape
    k0, k1 = m0.shape[1], m1.shape[1]
    assert w.shape == (k0 + k1, d) and k0 % k1 == 0
    return pl.pallas_call(
        _proj_out_body,
        grid=(m // OUT_TM,),
        in_specs=[
            pl.BlockSpec((OUT_TM, d), lambda i: (i, 0)),
            pl.BlockSpec((OUT_TM, k0), lambda i: (i, 0)),
            pl.BlockSpec((OUT_TM, k1), lambda i: (i, 0)),
            pl.BlockSpec((k0, d), lambda i: (0, 0)),
            pl.BlockSpec((k1, d), lambda i: (k0 // k1, 0)),
        ],
        out_specs=pl.BlockSpec((OUT_TM, d), lambda i: (i, 0)),
        out_shape=jax.ShapeDtypeStruct((m, d), f32),
        compiler_params=_cparams("parallel"),
        name="proj_out",
    )(x2, m0, m1, w, w)


def _ret_body(q_ref, k_ref, v_ref, z_ref, dmask_ref, kdec_ref, qdec_ref, cdec_ref, g_ref, o_ref):
    c_len = RET_CHUNK

    def body(n, state):
        rows = pl.ds(pl.multiple_of(n * c_len, c_len), c_len)
        q = q_ref[rows, :]
        k = k_ref[rows, :]
        vb = v_ref[rows, :]
        s = _dot_nt(q, k) * dmask_ref[...]
        o = _dot(s.astype(bf16), vb)
        o = o + _dot((q.astype(f32) * qdec_ref[...]).astype(bf16), state.astype(bf16))
        kv = _dot_tn((k.astype(f32) * kdec_ref[...]).astype(bf16), vb)
        state = cdec_ref[...] * state + kv
        y = _rms(o, g_ref[...])
        o_ref[rows, :] = (y * _silu(z_ref[rows, :].astype(f32))).astype(o_ref.dtype)
        return state

    lax.fori_loop(0, SEQ // c_len, body, jnp.zeros((HEAD_DIM, HEAD_DIM), f32), unroll=RET_UNROLL)


def _rotary_tables():
    half = HEAD_DIM // 2
    pos = jnp.arange(SEQ)
    inv = ROPE_BASE ** (-jnp.arange(half, dtype=f32) / half)
    ang = pos.astype(f32)[:, None] * inv[None, :]
    cos, sin = jnp.cos(ang), jnp.sin(ang)
    return jnp.concatenate([cos, cos], axis=-1), jnp.concatenate([-sin, sin], axis=-1)


def _retention(h, ret_g):
    b = h.shape[0]
    c_len = RET_CHUNK
    log_g = jnp.log(1.0 - 2.0 ** (-5.0 - jnp.arange(RET_HEADS, dtype=f32)))
    idx = jnp.arange(c_len, dtype=f32)
    diff = idx[:, None] - idx[None, :]
    dmask = jnp.where(diff >= 0, jnp.exp(jnp.maximum(diff, 0.0)[None] * log_g[:, None, None]), 0.0)
    ones = jnp.ones((1, 1, HEAD_DIM), f32)
    k_dec = jnp.exp((c_len - 1 - idx)[None, :] * log_g[:, None])[:, :, None] * ones
    q_dec = jnp.exp((idx + 1.0)[None, :] * log_g[:, None])[:, :, None] * ones
    chunk_dec = jnp.exp(c_len * log_g)[:, None, None] * ones

    def col(off):
        return pl.BlockSpec((None, SEQ, HEAD_DIM), lambda bi, hi: (bi, 0, off + hi))

    def per_head(rows, cols=HEAD_DIM):
        return pl.BlockSpec((None, rows, cols), lambda bi, hi: (hi, 0, 0))

    return pl.pallas_call(
        _ret_body,
        grid=(b, RET_HEADS),
        in_specs=[col(EV_RQ), col(EV_RK), col(EV_RV), col(EV_RZ),
                  per_head(c_len, c_len), per_head(c_len), per_head(c_len), per_head(1), per_head(1)],
        out_specs=pl.BlockSpec((None, SEQ, HEAD_DIM), lambda bi, hi: (bi, 0, hi)),
        out_shape=jax.ShapeDtypeStruct((b, SEQ, RET_W), bf16),
        compiler_params=_cparams("parallel", "parallel"),
        name="retention",
    )(h, h, h, h, dmask, k_dec, q_dec, chunk_dec, ret_g.reshape(RET_HEADS, 1, HEAD_DIM))


def _nsa_body(q_ref, kc_ref, vc_ref, ks_ref, vs_ref, kw_ref, vw_ref, nz_ref, ng_ref,
              kg_ref, posk_ref, posv_ref, w1k_ref, w2k_ref, w1v_ref, w2v_ref, ovl_ref, gsel_ref, eye_ref,
              o_ref, kcmp_s, vcmp_s, ksa_s, vsa_s, kwa_s, vwa_s, not_sel_s):
    tq = NSA_TQ
    sq = NSA_SELECT_TQ
    rep = NSA_REP
    half_blk = CMP_LEN // 2
    d = HEAD_DIM
    exp2_scale = SCALE * float(np.log2(np.e))

    def compress(t_ref, pos_ref, w1_ref, w2_ref):
        first = jnp.zeros((SEQ // CMP_STRIDE, d), f32)
        second = jnp.zeros((SEQ // CMP_STRIDE, d), f32)
        for i in range(half_blk):
            ti = t_ref[pl.ds(i, SEQ // CMP_STRIDE, stride=CMP_STRIDE), :]
            first = first + _dot((ti + pos_ref[i:i + 1, :]).astype(bf16), w1_ref[i * d:(i + 1) * d, :])
            j = half_blk + i
            second = second + _dot((ti + pos_ref[j:j + 1, :]).astype(bf16), w1_ref[j * d:(j + 1) * d, :])
        pre = first + pltpu.roll(second, SEQ // CMP_STRIDE - 1, 0)
        return _dot(jax.nn.gelu(pre).astype(bf16), w2_ref[...])

    kcmp_s[...] = _rms(compress(kc_ref, posk_ref, w1k_ref, w2k_ref), kg_ref[0:1, :]).astype(bf16)
    vcmp_s[...] = compress(vc_ref, posv_ref, w1v_ref, w2v_ref).astype(bf16)
    lane0 = lax.broadcasted_iota(jnp.int32, (tq, d), 1) == 0
    pc_rows = NSA_PREP_ROWS
    ones = jnp.ones((pc_rows, d), bf16)
    zeros = jnp.zeros((pc_rows, d), bf16)
    for c in range(NSA_WINDOW // pc_rows):
        rows = slice(c * pc_rows, (c + 1) * pc_rows)
        kwa_s[rows, 0:d] = zeros
        kwa_s[rows, d:2 * d] = jnp.where(lax.broadcasted_iota(jnp.int32, (pc_rows, d), 1) == 0, NEG_INF, 0.0).astype(bf16)
        vwa_s[rows, 0:d] = zeros
        vwa_s[rows, d:2 * d] = zeros
    for c in range(SEQ // pc_rows):
        rows = slice(c * pc_rows, (c + 1) * pc_rows)
        wrows = slice(NSA_WINDOW + c * pc_rows, NSA_WINDOW + (c + 1) * pc_rows)
        ksa_s[rows, 0:d] = ks_ref[rows, :].astype(bf16)
        ksa_s[rows, d:2 * d] = gsel_ref[rows, :]
        vsa_s[rows, 0:d] = vs_ref[rows, :].astype(bf16)
        vsa_s[rows, d:2 * d] = ones
        kwa_s[wrows, 0:d] = kw_ref[rows, :]
        kwa_s[wrows, d:2 * d] = zeros
        vwa_s[wrows, 0:d] = vw_ref[rows, :]
        vwa_s[wrows, d:2 * d] = ones

    row_i = lax.broadcasted_iota(jnp.int32, (tq, tq), 0)
    col_i = lax.broadcasted_iota(jnp.int32, (tq, tq), 1)
    band_first = jnp.where(col_i > row_i, 0.0, NEG_INF)
    band_last = jnp.where(col_i <= row_i, 0.0, NEG_INF)
    pad_flag = jnp.where(lane0, 1.0, 0.0).astype(bf16)

    def softmax_update(s, m, acc, v):
        n = s.shape[-1]
        m_new = jnp.maximum(m, jnp.max(s, axis=-1, keepdims=True))
        alpha = jnp.exp2((m - m_new) * exp2_scale)
        e = jnp.exp2((s - m_new) * exp2_scale)
        pv = _dot(e.reshape(rep * tq, n).astype(bf16), v).reshape(rep, tq, 2 * d)
        return m_new, alpha * acc + pv

    def select_tile(i, carry):
        t0 = i * sq
        rows = pl.ds(pl.multiple_of(t0, sq), sq)
        qs = [q_ref[rows, r * d:(r + 1) * d] for r in range(rep)]
        tpos = t0 + lax.broadcasted_iota(jnp.int32, (sq, 1), 0)
        qst = jnp.concatenate(qs, axis=0)
        n_idx = lax.broadcasted_iota(jnp.int32, (1, LANE), 1)
        valid = (n_idx * CMP_STRIDE + (CMP_LEN - 1) <= tpos) & (n_idx < N_CMP)
        cbias = jnp.where(valid, 0.0, NEG_INF)
        sc = (_dot_nt(qst, kcmp_s[...]) * SCALE).reshape(rep, sq, LANE) + cbias[None]
        mc = jnp.max(sc, axis=-1, keepdims=True)
        ec = jnp.exp(sc - mc)
        any_valid = jnp.where(tpos >= CMP_LEN - 1, 1.0, 0.0)
        pc = ec / jnp.sum(ec, axis=-1, keepdims=True) * any_valid[None]
        o_cmp = _dot(pc.reshape(rep * sq, LANE).astype(bf16), vcmp_s[...]).reshape(rep, sq, d)
        gates = jax.nn.sigmoid(ng_ref[rows, :].astype(f32))
        for r in range(rep):
            o_ref[rows, r * d:(r + 1) * d] = (gates[:, r:r + 1] * o_cmp[r]).astype(o_ref.dtype)
        psum = pc[0]
        for r in range(1, rep):
            psum = psum + pc[r]

        p_hi = psum.astype(bf16)
        rest = psum - p_hi.astype(f32)
        p_mid = rest.astype(bf16)
        p_lo = (rest - p_mid.astype(f32)).astype(bf16)
        imp = (_dot_nt(ovl_ref[...], p_hi) + _dot_nt(ovl_ref[...], p_mid) + _dot_nt(ovl_ref[...], p_lo))[:N_SEL, :]
        jj = lax.broadcasted_iota(jnp.int32, (N_SEL, 1), 0)
        cur = (t0 + lax.broadcasted_iota(jnp.int32, (1, sq), 1)) // SEL_BLOCK
        forced = (jj == 0) | (jj == cur) | (jj == cur - 1)
        imp = jnp.where(forced, -NEG_INF, jnp.where(jj > cur, NEG_INF, imp))
        rank = jnp.zeros((N_SEL, sq), f32)
        for j2 in range(N_SEL):
            row = imp[j2:j2 + 1, :]
            beats = (row > imp) | ((row == imp) & (j2 < jj))
            rank = rank + jnp.where(beats, 1.0, 0.0)
        not_sel_t = jnp.where(rank < SEL_TOPK, 0.0, 1.0).astype(bf16)
        not_sel_s[rows, :] = _dot_tn(not_sel_t, eye_ref[...]).astype(bf16)
        return carry

    lax.fori_loop(0, SEQ // sq, select_tile, 0)

    def make_tile(n_full):
        def tile(i, carry):
            t0 = i * tq
            rows = pl.ds(pl.multiple_of(t0, tq), tq)
            qs = [q_ref[rows, r * d:(r + 1) * d] for r in range(rep)]
            tpos = t0 + lax.broadcasted_iota(jnp.int32, (tq, 1), 0)
            m_init = jnp.full((rep, tq, 1), NEG_INF, f32)
            acc_init = jnp.zeros((rep, tq, 2 * d), f32)

            qw = jnp.concatenate([jnp.concatenate([q, pad_flag], axis=1) for q in qs], axis=0)
            wrows = pl.ds(pl.multiple_of(t0, tq), WIN_SPAN)
            sw = _dot_nt(qw, kwa_s[wrows, :]).reshape(rep, tq, WIN_SPAN)
            sw = jnp.concatenate([sw[:, :, :tq] + band_first[None], sw[:, :, tq:NSA_WINDOW],
                                  sw[:, :, NSA_WINDOW:] + band_last[None]], axis=2)
            _, acc_w = softmax_update(sw, m_init, acc_init, vwa_s[wrows, :])
            o_win = acc_w[:, :, :d] / acc_w[:, :, d:]

            not_sel = not_sel_s[rows, :]
            qsel = jnp.concatenate([jnp.concatenate([q, not_sel], axis=1) for q in qs], axis=0)
            m_s, acc_s = m_init, acc_init
            for kt in range(n_full + 1):
                krows = slice(kt * SEL_TK, (kt + 1) * SEL_TK)
                s = _dot_nt(qsel, ksa_s[krows, :]).reshape(rep, tq, SEL_TK)
                if kt == n_full:
                    kpos = kt * SEL_TK + lax.broadcasted_iota(jnp.int32, (1, SEL_TK), 1)
                    s = s + jnp.where(kpos <= tpos, 0.0, NEG_INF)[None]
                m_s, acc_s = softmax_update(s, m_s, acc_s, vsa_s[krows, :])
            o_sel = acc_s[:, :, :d] / acc_s[:, :, d:]

            gates = jax.nn.sigmoid(ng_ref[rows, :].astype(f32))
            for r in range(rep):
                g_s = gates[:, rep + r:rep + r + 1]
                g_w = gates[:, 2 * rep + r:2 * rep + r + 1]
                cols = slice(r * d, (r + 1) * d)
                o = o_ref[rows, cols].astype(f32) + g_s * o_sel[r] + g_w * o_win[r]
                o_ref[rows, cols] = (o * _silu(nz_ref[rows, cols].astype(f32))).astype(o_ref.dtype)
            return carry
        return tile

    tiles_per_key_tile = SEL_TK // tq
    for n_full in range(SEQ // SEL_TK):
        lax.fori_loop(n_full * tiles_per_key_tile, (n_full + 1) * tiles_per_key_tile, make_tile(n_full), 0)


def _nsa(h, hf, k_g, pos_k, pos_v, k_w1, k_w2, v_w1, v_w2):
    b = h.shape[0]
    grp_w = NSA_REP * HEAD_DIM
    cs = np.arange(LANE) * CMP_STRIDE
    ss = np.arange(LANE) * SEL_BLOCK
    ovl = ((cs[None, :] < ss[:, None] + SEL_BLOCK) & (cs[None, :] + CMP_LEN > ss[:, None])
           & (np.arange(LANE)[None, :] < N_CMP) & (np.arange(LANE)[:, None] < N_SEL))
    ovl_t = jnp.asarray(ovl, dtype=bf16)
    key_blk = np.arange(SEQ) // SEL_BLOCK
    gsel = jnp.asarray(np.where(key_blk[:, None] == np.arange(LANE)[None, :], NEG_INF, 0.0), dtype=bf16)
    eye = jnp.asarray(np.eye(N_SEL, LANE), dtype=bf16)

    def wide(off):
        return pl.BlockSpec((None, SEQ, grp_w), lambda bi, gi: (bi, 0, off // NSA_REP + gi))

    def col(off):
        return pl.BlockSpec((None, SEQ, HEAD_DIM), lambda bi, gi: (bi, 0, off + gi))

    def full(shape):
        return pl.BlockSpec(shape, lambda bi, gi: (0,) * len(shape))

    flat = CMP_LEN * HEAD_DIM
    return pl.pallas_call(
        _nsa_body,
        grid=(b, NSA_KV_HEADS),
        in_specs=[wide(EV_NQ), col(EV_F32_KC), col(EV_F32_VC), col(EV_F32_KS), col(EV_F32_VS), col(EV_KW), col(EV_VW),
                  wide(EV_NZ), col(EV_NG),
                  full((NSA_N_BRANCH, HEAD_DIM)), full((CMP_LEN, HEAD_DIM)), full((CMP_LEN, HEAD_DIM)),
                  full((flat, HEAD_DIM)), full((HEAD_DIM, HEAD_DIM)), full((flat, HEAD_DIM)), full((HEAD_DIM, HEAD_DIM)),
                  full((LANE, LANE)), full((SEQ, LANE)), full((N_SEL, LANE))],
        out_specs=pl.BlockSpec((None, SEQ, grp_w), lambda bi, gi: (bi, 0, gi)),
        out_shape=jax.ShapeDtypeStruct((b, SEQ, NSA_HEADS * HEAD_DIM), bf16),
        scratch_shapes=[pltpu.VMEM((SEQ // CMP_STRIDE, HEAD_DIM), bf16), pltpu.VMEM((SEQ // CMP_STRIDE, HEAD_DIM), bf16),
                        pltpu.VMEM((SEQ, 2 * HEAD_DIM), bf16), pltpu.VMEM((SEQ, 2 * HEAD_DIM), bf16),
                        pltpu.VMEM((SEQ + NSA_WINDOW, 2 * HEAD_DIM), bf16), pltpu.VMEM((SEQ + NSA_WINDOW, 2 * HEAD_DIM), bf16),
                        pltpu.VMEM((SEQ, LANE), bf16)],
        compiler_params=_cparams("parallel", "parallel"),
        name="nsa",
    )(h, hf, hf, hf, hf, h, h, h, h,
      k_g, pos_k, pos_v,
      k_w1.astype(bf16), k_w2.astype(bf16), v_w1.astype(bf16), v_w2.astype(bf16), ovl_t, gsel, eye)


def _conv_body(u_ref, g_ref, z_ref, w_ref, b_ref, lg_ref, lb_ref, o_ref, a_ext, a_sh, y_s):
    ext = CONV_HALO + CONV_T
    n_cb = CONV_W // LANE
    lead = CONV_HALO - (CONV_K - 1)

    @pl.when(pl.program_id(1) == 0)
    def _():
        for cb in range(n_cb):
            a_ext[cb, 0:CONV_HALO, :] = jnp.zeros((CONV_HALO, LANE), f32)
            a_ext[cb, ext:ext + SUBLANE, :] = jnp.zeros((SUBLANE, LANE), f32)

    for cb in range(n_cb):
        lanes = slice(cb * LANE, (cb + 1) * LANE)
        for c in range(CONV_T // 128):
            rows = slice(c * 128, (c + 1) * 128)
            a_ext[cb, CONV_HALO + c * 128:CONV_HALO + (c + 1) * 128, :] = u_ref[rows, lanes].astype(f32) * jax.nn.sigmoid(g_ref[rows, lanes].astype(f32))
        for r in range(1, SUBLANE):
            for c in range(ext // CONV_HALO):
                a_sh[r - 1, cb, c * CONV_HALO:(c + 1) * CONV_HALO, :] = a_ext[cb, c * CONV_HALO + r:(c + 1) * CONV_HALO + r, :]

        taps_w = [jnp.broadcast_to(w_ref[k:k + 1, lanes], (CONV_RB, LANE)) for k in range(CONV_K)]
        bias = jnp.broadcast_to(b_ref[:, lanes], (CONV_RB, LANE))

        def conv_rows(rb, carry, cb=cb, taps_w=taps_w, bias=bias):
            r0 = pl.multiple_of(rb * CONV_RB, CONV_RB)
            parts = [bias, None]
            for k in range(CONV_K):
                shift, base = (lead + k) % SUBLANE, (lead + k) // SUBLANE * SUBLANE
                rows = pl.ds(r0 + base, CONV_RB)
                tap = a_ext[cb, rows, :] if shift == 0 else a_sh[shift - 1, cb, rows, :]
                term = tap * taps_w[k]
                parts[k % 2] = term if parts[k % 2] is None else parts[k % 2] + term
            y_s[cb, pl.ds(r0, CONV_RB), :] = parts[0] + parts[1]
            return carry

        lax.fori_loop(0, CONV_T // CONV_RB, conv_rows, 0)
        a_ext[cb, 0:CONV_HALO, :] = a_ext[cb, CONV_T:CONV_T + CONV_HALO, :]

    def norm_rows(rb, carry):
        rows = pl.ds(pl.multiple_of(rb * CONV_RB, CONV_RB), CONV_RB)
        y = jnp.concatenate([y_s[cb, rows, :] for cb in range(n_cb)], axis=1)
        mu = jnp.mean(y, axis=-1, keepdims=True)
        var = jnp.mean(jnp.square(y - mu), axis=-1, keepdims=True)
        yn = (y - mu) * lax.rsqrt(var + NORM_EPS) * lg_ref[...] + lb_ref[...]
        o_ref[rows, :] = (_silu(yn) * _silu(z_ref[rows, :].astype(f32))).astype(o_ref.dtype)
        return carry

    lax.fori_loop(0, CONV_T // CONV_RB, norm_rows, 0, unroll=2)


def _conv(h, dw_w, dw_b, ln_g, ln_b):
    b = h.shape[0]

    def col(off):
        return pl.BlockSpec((None, CONV_T, CONV_W), lambda bi, ti: (bi, ti, off * LANE // CONV_W))

    def full(rows):
        return pl.BlockSpec((rows, CONV_W), lambda bi, ti: (0, 0))

    return pl.pallas_call(
        _conv_body,
        grid=(b, SEQ // CONV_T),
        in_specs=[col(OD_CU), col(OD_CG), col(OD_CZ), full(CONV_K), full(1), full(1), full(1)],
        out_specs=pl.BlockSpec((None, CONV_T, CONV_W), lambda bi, ti: (bi, ti, 0)),
        out_shape=jax.ShapeDtypeStruct((b, SEQ, CONV_W), bf16),
        scratch_shapes=[pltpu.VMEM((CONV_W // LANE, CONV_HALO + CONV_T + SUBLANE, LANE), f32),
                        pltpu.VMEM((SUBLANE - 1, CONV_W // LANE, CONV_HALO + CONV_T, LANE), f32),
                        pltpu.VMEM((CONV_W // LANE, CONV_T, LANE), f32)],
        compiler_params=_cparams("parallel", "arbitrary"),
        name="conformer_conv",
    )(h, h, h, dw_w, dw_b.reshape(1, CONV_W), ln_g.reshape(1, CONV_W), ln_b.reshape(1, CONV_W))


def _dil_body(q0_ref, k0_ref, v0_ref, q1_ref, k1_ref, v1_ref, q2_ref, k2_ref, v2_ref, z_ref,
              o_ref, qn_s, kn_s, va_s, o0_s, o1_s, o2_s, l0_s, l1_s, l2_s):
    blk = DIL_BLOCK
    n_tiles = SEQ // blk
    groups = ((q0_ref, k0_ref, v0_ref, o0_s, l0_s), (q1_ref, k1_ref, v1_ref, o1_s, l1_s), (q2_ref, k2_ref, v2_ref, o2_s, l2_s))
    kn_s[0:blk, :] = jnp.zeros((blk, HEAD_DIM), bf16)
    va_s[0:blk, :] = jnp.zeros((blk, 2 * HEAD_DIM), bf16)
    ones = jnp.ones((blk, HEAD_DIM), bf16)
    row2 = lax.broadcasted_iota(jnp.int32, (blk, 2 * blk), 0)
    col2 = lax.broadcasted_iota(jnp.int32, (blk, 2 * blk), 1)
    band_prev = jnp.where((col2 >= row2) & (col2 <= row2 + blk), 0.0, NEG_INF)
    band_first = jnp.where((col2 >= blk) & (col2 <= row2 + blk), 0.0, NEG_INF)

    for (q_ref, k_ref, v_ref, o_s, l_s), (window, dil) in zip(groups, DIL_PATTERNS):
        assert window // dil == blk
        nb = SEQ // dil // blk

        def natural_rows(u, dil=dil, nb=nb):
            start = u // nb + (u % nb) * (blk * dil)
            return pl.ds(start, blk, stride=dil) if dil > 1 else pl.ds(pl.multiple_of(start, blk), blk)

        def prep(u, carry, q_ref=q_ref, k_ref=k_ref, v_ref=v_ref, natural_rows=natural_rows):
            src = natural_rows(u)
            qn_s[pl.ds(pl.multiple_of(u * blk, blk), blk), :] = q_ref[src, :].astype(bf16)
            dst = pl.ds(pl.multiple_of((u + 1) * blk, blk), blk)
            kn_s[dst, :] = k_ref[src, :].astype(bf16)
            va_s[dst, :] = jnp.concatenate([v_ref[src, :].astype(bf16), ones], axis=1)
            return carry

        lax.fori_loop(0, n_tiles, prep, 0, unroll=DIL_UNROLL)

        def att(u, carry, o_s=o_s, l_s=l_s, nb=nb, natural_rows=natural_rows):
            q = qn_s[pl.ds(pl.multiple_of(u * blk, blk), blk), :]
            keys = pl.ds(pl.multiple_of(u * blk, blk), 2 * blk)
            band = jnp.where((u % nb) > 0, band_prev, band_first)
            s = _dot_nt(q, kn_s[keys, :]) * SCALE + band
            m = jnp.max(s, axis=-1, keepdims=True)
            pv = _dot(jnp.exp(s - m).astype(bf16), va_s[keys, :])
            den = pv[:, HEAD_DIM:]
            dst = natural_rows(u)
            o_s[dst, :] = pv[:, :HEAD_DIM] / den
            l_s[dst, :] = m + jnp.log(den)
            return carry

        lax.fori_loop(0, n_tiles, att, 0, unroll=DIL_UNROLL)

    def mix(c, carry):
        rows = pl.ds(pl.multiple_of(c * blk, blk), blk)
        l0, l1, l2 = l0_s[rows, :], l1_s[rows, :], l2_s[rows, :]
        m = jnp.maximum(jnp.maximum(l0, l1), l2)
        w0, w1, w2 = jnp.exp(l0 - m), jnp.exp(l1 - m), jnp.exp(l2 - m)
        o = (w0 * o0_s[rows, :] + w1 * o1_s[rows, :] + w2 * o2_s[rows, :]) / (w0 + w1 + w2)
        o_ref[rows, :] = (o * _silu(z_ref[rows, :].astype(f32))).astype(o_ref.dtype)
        return carry

    lax.fori_loop(0, n_tiles, mix, 0, unroll=DIL_UNROLL)


def _dilated(h, hf):
    b = h.shape[0]

    def col(off):
        return pl.BlockSpec((None, SEQ, HEAD_DIM), lambda bi, hi: (bi, 0, off + hi))

    specs = [col(OD_DQ0), col(OD_DK0), col(OD_DV0)]
    for gi in range(len(DIL_PATTERNS) - 1):
        specs += [col(OD_F32_Q + gi * DIL_GROUP_HEADS), col(OD_F32_K + gi * DIL_GROUP_HEADS), col(OD_F32_V + gi * DIL_GROUP_HEADS)]
    return pl.pallas_call(
        _dil_body,
        grid=(b, DIL_GROUP_HEADS),
        in_specs=specs + [col(OD_DZ)],
        out_specs=pl.BlockSpec((None, SEQ, HEAD_DIM), lambda bi, hi: (bi, 0, hi)),
        out_shape=jax.ShapeDtypeStruct((b, SEQ, DIL_GROUP_HEADS * HEAD_DIM), bf16),
        scratch_shapes=[pltpu.VMEM((SEQ, HEAD_DIM), bf16), pltpu.VMEM((SEQ + DIL_BLOCK, HEAD_DIM), bf16),
                        pltpu.VMEM((SEQ + DIL_BLOCK, 2 * HEAD_DIM), bf16)] + [pltpu.VMEM((SEQ, HEAD_DIM), f32)] * 6,
        compiler_params=_cparams("parallel", "parallel"),
        name="dilated",
    )(h, h, h, hf, hf, hf, hf, hf, hf, h)


def _even_w_in(w):
    w = w.astype(bf16)
    ng0 = EV_W_NZ * LANE
    nz0 = ng0 + NSA_N_BRANCH * NSA_HEADS
    ng = w[:, ng0:nz0].reshape(-1, NSA_N_BRANCH, NSA_KV_HEADS, NSA_REP)
    ng = ng.transpose(0, 2, 1, 3).reshape(-1, NSA_KV_HEADS, NSA_N_BRANCH * NSA_REP)
    ng = jnp.pad(ng, ((0, 0), (0, 0), (0, LANE - NSA_N_BRANCH * NSA_REP))).reshape(-1, NSA_KV_HEADS * LANE)
    out = jnp.concatenate([w[:, :ng0], w[:, nz0:], ng], axis=1)
    return jnp.pad(out, ((0, 0), (0, EV_COLS - out.shape[1])))


def _odd_w_in(w):
    w = w.astype(bf16)
    conv_w = 3 * CONV_W
    qkv_w = DIL_HEADS * HEAD_DIM
    g0 = DIL_GROUP_HEADS * HEAD_DIM
    q, k, v = (w[:, conv_w + i * qkv_w:conv_w + (i + 1) * qkv_w] for i in range(3))
    dz = w[:, conv_w + 3 * qkv_w:]
    return jnp.concatenate([w[:, :conv_w], q[:, :g0], k[:, :g0], v[:, :g0], dz, q[:, g0:], k[:, g0:], v[:, g0:]], axis=1)


def _tile_table(entries, fill):
    n_blk = EV_COLS // LANE
    flat = [fill] * n_blk
    for off, (count, value) in entries.items():
        flat[off:off + count] = [value] * count
    per_tile = PROJ_TN // LANE
    return tuple(tuple(flat[t * per_tile:(t + 1) * per_tile]) for t in range(n_blk // per_tile))


def _col_gain(entries):
    g = jnp.ones((EV_COLS // LANE, HEAD_DIM), f32)
    for off, (count, gain) in entries.items():
        g = g.at[off:off + count].set(jnp.broadcast_to(gain, (count, HEAD_DIM)))
    return g.reshape(1, EV_COLS)


def kernel(x, ev_norm, ev_w_in, ev_w_out, ev_ret_norm, ev_nsa_q_norm, ev_nsa_k_norm, ev_cmp_pos_k, ev_cmp_pos_v, ev_cmp_k_w1, ev_cmp_k_w2, ev_cmp_v_w1, ev_cmp_v_w2, od_norm, od_w_in, od_w_out, od_dw_w, od_dw_b, od_conv_norm_g, od_conv_norm_b, od_dil_q_norm, od_dil_k_norm):
    b, s, d = x.shape
    assert (s, d) == (SEQ, D_MODEL) and EV_COLS == OD_COLS
    x0 = x.reshape(b * s, d)
    cos_t, sin_t = _rotary_tables()

    ev_kinds = _tile_table({EV_RQ: (RET_HEADS, EPI_ROT), EV_RK: (RET_HEADS, EPI_ROT_SCALED), EV_NQ: (NSA_HEADS, EPI_RMS),
                            EV_W_KS: (NSA_KV_HEADS, EPI_RMS), EV_W_KW: (NSA_KV_HEADS, EPI_RMS)}, EPI_NONE)
    ev_gain = _col_gain({EV_NQ: (NSA_HEADS, ev_nsa_q_norm[0]), EV_W_KS: (NSA_KV_HEADS, ev_nsa_k_norm[0][1]),
                         EV_W_KW: (NSA_KV_HEADS, ev_nsa_k_norm[0][2])})
    h0, h0f = _proj_in(x0, ev_norm[0], _even_w_in(ev_w_in[0]), EV_F32_TILE, 1, ev_kinds, ev_gain, cos_t, sin_t)
    h0, h0f = h0.reshape(b, s, -1), h0f.reshape(b, s, -1)
    a_out = _retention(h0, ev_ret_norm[0])
    b_out = _nsa(h0, h0f, ev_nsa_k_norm[0], ev_cmp_pos_k[0], ev_cmp_pos_v[0],
                 ev_cmp_k_w1[0], ev_cmp_k_w2[0], ev_cmp_v_w1[0], ev_cmp_v_w2[0])
    x1 = _proj_out(x0, a_out.reshape(b * s, -1), b_out.reshape(b * s, -1), ev_w_out[0].astype(bf16))

    f32_q, f32_k = OD_F32_TILE * (PROJ_TN // LANE) + OD_F32_Q, OD_F32_TILE * (PROJ_TN // LANE) + OD_F32_K
    strided_heads = DIL_HEADS - DIL_GROUP_HEADS
    od_norm_blocks = {OD_DQ0: (DIL_GROUP_HEADS, od_dil_q_norm[0]), OD_DK0: (DIL_GROUP_HEADS, od_dil_k_norm[0]),
                      f32_q: (strided_heads, od_dil_q_norm[0]), f32_k: (strided_heads, od_dil_k_norm[0])}
    od_kinds = _tile_table({off: (count, EPI_RMS) for off, (count, _) in od_norm_blocks.items()}, EPI_NONE)
    h1, h1f = _proj_in(x1, od_norm[0], _odd_w_in(od_w_in[0]), OD_F32_TILE, OD_F32_TILES, od_kinds,
                       _col_gain(od_norm_blocks), cos_t, sin_t)
    h1, h1f = h1.reshape(b, s, -1), h1f.reshape(b, s, -1)
    c_out = _conv(h1, od_dw_w[0], od_dw_b[0], od_conv_norm_g[0], od_conv_norm_b[0])
    d_out = _dilated(h1, h1f)
    x2 = _proj_out(x1, c_out.reshape(b * s, -1), d_out.reshape(b * s, -1), od_w_out[0].astype(bf16))
    return x2.reshape(b, s, d)
```

```python
import functools

import jax
import jax.numpy as jnp
import numpy as np
from jax import lax
from jax.experimental import pallas as pl
from jax.experimental.pallas import tpu as pltpu

f32 = jnp.float32
bf16 = jnp.bfloat16

D_MODEL = 2048
SEQ = 2048
HEAD_DIM = 128
SCALE = HEAD_DIM ** -0.5
NORM_EPS = 1e-6
NEG_INF = -1e30
LANE = 128
SUBLANE = 8

RET_HEADS = 8
RET_W = RET_HEADS * HEAD_DIM
RET_CHUNK = 256
RET_UNROLL = 8
ROPE_BASE = 10000.0
NSA_HEADS = 8
NSA_KV_HEADS = 2
NSA_REP = NSA_HEADS // NSA_KV_HEADS
NSA_N_BRANCH = 3
CMP_LEN = 32
CMP_STRIDE = 16
N_CMP = (SEQ - CMP_LEN) // CMP_STRIDE + 1
SEL_BLOCK = 64
N_SEL = SEQ // SEL_BLOCK
SEL_TOPK = 16
NSA_WINDOW = 512
NSA_TQ = 128
SEL_TK = 512
NSA_PREP_ROWS = 256
NSA_SELECT_TQ = 512
WIN_SPAN = NSA_WINDOW + NSA_TQ
NSA_TILE_UNROLL = 2
CONV_W = 1024
CONV_K = 31
CONV_T = 512
CONV_HALO = 32
CONV_RB = 64
DIL_PATTERNS = ((128, 1), (512, 4), (2048, 16))
DIL_GROUP_HEADS = 4
DIL_HEADS = DIL_GROUP_HEADS * len(DIL_PATTERNS)
DIL_BLOCK = 128
DIL_UNROLL = 8

EV_RQ, EV_RK, EV_RV, EV_RZ = 0, 8, 16, 24
EV_NQ = 32
EV_W_NZ = 52
EV_COLS = 8192
EV_F32_TILE = 5
EV_F32_KC, EV_F32_VC, EV_F32_KS, EV_F32_VS = 0, 2, 4, 6
EV_KW, EV_VW = 40, 42
EV_NZ = 44
EV_NG = 52
OD_CU, OD_CG, OD_CZ = 0, 8, 16
OD_DQ, OD_DK, OD_DV = 24, 36, 48
OD_DZ = 60
OD_COLS = 8192

VMEM_LIMIT = 56 * 1024 * 1024


def _cparams(*sem):
    return pltpu.CompilerParams(dimension_semantics=sem, vmem_limit_bytes=VMEM_LIMIT)


def _rms(x, gain):
    ms = jnp.mean(x * x, axis=-1, keepdims=True)
    return x * lax.rsqrt(ms + NORM_EPS) * gain


def _silu(z):
    return z * jax.nn.sigmoid(z)


def _dot(a, b):
    return jnp.dot(a, b, preferred_element_type=f32)


def _dot_nt(a, b):
    return lax.dot_general(a, b, (((1,), (1,)), ((), ())), preferred_element_type=f32)


def _dot_tn(a, b):
    return lax.dot_general(a, b, (((0,), (0,)), ((), ())), preferred_element_type=f32)


PROJ_TM = 1024
PROJ_TN = 1024
NORM_ROWS = 256


def _proj_in_body(x_ref, g_ref, w_ref, *refs, f32_tile, f32_tiles, n_tiles):
    xn_ref = refs[-1]
    j = pl.program_id(1)

    @pl.when(j == 0)
    def _():
        for c in range(PROJ_TM // NORM_ROWS):
            rows = slice(c * NORM_ROWS, (c + 1) * NORM_ROWS)
            xn_ref[rows, :] = _rms(x_ref[rows, :], g_ref[...]).astype(bf16)

    def emit(out_ref):
        out_ref[...] = _dot(xn_ref[...], w_ref[...]).astype(out_ref.dtype)

    if f32_tiles == n_tiles:
        emit(refs[0])
    else:
        in_f32 = (j >= f32_tile) & (j < f32_tile + f32_tiles)
        pl.when(in_f32)(functools.partial(emit, refs[1]))
        pl.when(jnp.logical_not(in_f32))(functools.partial(emit, refs[0]))


def _proj_in(x2, gain, w, f32_tile, f32_tiles):
    m, d = x2.shape
    n = w.shape[1]
    n_tiles = n // PROJ_TN
    all_f32 = f32_tiles == n_tiles
    assert all_f32 or f32_tile >= 1
    bf16_spec = pl.BlockSpec((PROJ_TM, PROJ_TN), lambda i, j: (i, j - jnp.clip(j - f32_tile + 1, 0, f32_tiles)))
    f32_spec = pl.BlockSpec((PROJ_TM, PROJ_TN), lambda i, j: (i, jnp.clip(j - f32_tile, 0, f32_tiles - 1)))
    bf16_shape = jax.ShapeDtypeStruct((m, n - f32_tiles * PROJ_TN), bf16)
    f32_shape = jax.ShapeDtypeStruct((m, f32_tiles * PROJ_TN), f32)
    out = pl.pallas_call(
        functools.partial(_proj_in_body, f32_tile=f32_tile, f32_tiles=f32_tiles, n_tiles=n_tiles),
        grid=(m // PROJ_TM, n_tiles),
        in_specs=[
            pl.BlockSpec((PROJ_TM, d), lambda i, j: (i, 0)),
            pl.BlockSpec((1, d), lambda i, j: (0, 0)),
            pl.BlockSpec((d, PROJ_TN), lambda i, j: (0, j)),
        ],
        out_specs=[f32_spec] if all_f32 else [bf16_spec, f32_spec],
        out_shape=[f32_shape] if all_f32 else [bf16_shape, f32_shape],
        scratch_shapes=[pltpu.VMEM((PROJ_TM, d), bf16)],
        compiler_params=_cparams("parallel", "arbitrary"),
        name="proj_in",
    )(x2, gain.reshape(1, d), w)
    return (None, out[0]) if all_f32 else tuple(out)


OUT_TM = 512


def _proj_out_body(x_ref, m0_ref, m1_ref, w0_ref, w1_ref, o_ref):
    o_ref[...] = x_ref[...] + _dot(m0_ref[...], w0_ref[...]) + _dot(m1_ref[...], w1_ref[...])


def _proj_out(x2, m0, m1, w):
    m, d = x2.shape
    k0, k1 = m0.shape[1], m1.shape[1]
    assert w.shape == (k0 + k1, d) and k0 % k1 == 0
    return pl.pallas_call(
        _proj_out_body,
        grid=(m // OUT_TM,),
        in_specs=[
            pl.BlockSpec((OUT_TM, d), lambda i: (i, 0)),
            pl.BlockSpec((OUT_TM, k0), lambda i: (i, 0)),
            pl.BlockSpec((OUT_TM, k1), lambda i: (i, 0)),
            pl.BlockSpec((k0, d), lambda i: (0, 0)),
            pl.BlockSpec((k1, d), lambda i: (k0 // k1, 0)),
        ],
        out_specs=pl.BlockSpec((OUT_TM, d), lambda i: (i, 0)),
        out_shape=jax.ShapeDtypeStruct((m, d), f32),
        compiler_params=_cparams("parallel"),
        name="proj_out",
    )(x2, m0, m1, w, w)


def _ret_body(q_ref, k_ref, v_ref, z_ref, cos_ref, sin_ref, dmask_ref, kdec_ref, qdec_ref, cdec_ref, g_ref, o_ref):
    c_len = RET_CHUNK

    def rot(t, cs, sn):
        return t * cs + pltpu.roll(t, HEAD_DIM // 2, 1) * sn

    def body(n, state):
        rows = pl.ds(pl.multiple_of(n * c_len, c_len), c_len)
        cs = cos_ref[rows, :]
        sn = sin_ref[rows, :]
        q = rot(q_ref[rows, :].astype(f32), cs, sn)
        k = rot(k_ref[rows, :].astype(f32), cs, sn) * SCALE
        vb = v_ref[rows, :]
        s = _dot_nt(q.astype(bf16), k.astype(bf16)) * dmask_ref[...]
        o = _dot(s.astype(bf16), vb)
        o = o + _dot((q * qdec_ref[...]).astype(bf16), state.astype(bf16))
        kv = _dot_tn((k * kdec_ref[...]).astype(bf16), vb)
        state = cdec_ref[...] * state + kv
        y = _rms(o, g_ref[...])
        o_ref[rows, :] = (y * _silu(z_ref[rows, :].astype(f32))).astype(o_ref.dtype)
        return state

    lax.fori_loop(0, SEQ // c_len, body, jnp.zeros((HEAD_DIM, HEAD_DIM), f32), unroll=RET_UNROLL)


def _retention(h, ret_g):
    b = h.shape[0]
    c_len = RET_CHUNK
    half = HEAD_DIM // 2
    pos = jnp.arange(SEQ)
    inv = ROPE_BASE ** (-jnp.arange(half, dtype=f32) / half)
    ang = pos.astype(f32)[:, None] * inv[None, :]
    cos, sin = jnp.cos(ang), jnp.sin(ang)
    cos_t = jnp.concatenate([cos, cos], axis=-1)
    sin_t = jnp.concatenate([-sin, sin], axis=-1)
    log_g = jnp.log(1.0 - 2.0 ** (-5.0 - jnp.arange(RET_HEADS, dtype=f32)))
    idx = jnp.arange(c_len, dtype=f32)
    diff = idx[:, None] - idx[None, :]
    dmask = jnp.where(diff >= 0, jnp.exp(jnp.maximum(diff, 0.0)[None] * log_g[:, None, None]), 0.0)
    ones = jnp.ones((1, 1, HEAD_DIM), f32)
    k_dec = jnp.exp((c_len - 1 - idx)[None, :] * log_g[:, None])[:, :, None] * ones
    q_dec = jnp.exp((idx + 1.0)[None, :] * log_g[:, None])[:, :, None] * ones
    chunk_dec = jnp.exp(c_len * log_g)[:, None, None] * ones

    def col(off):
        return pl.BlockSpec((None, SEQ, HEAD_DIM), lambda bi, hi: (bi, 0, off + hi))

    table = pl.BlockSpec((SEQ, HEAD_DIM), lambda bi, hi: (0, 0))

    def per_head(rows, cols=HEAD_DIM):
        return pl.BlockSpec((None, rows, cols), lambda bi, hi: (hi, 0, 0))

    return pl.pallas_call(
        _ret_body,
        grid=(b, RET_HEADS),
        in_specs=[col(EV_RQ), col(EV_RK), col(EV_RV), col(EV_RZ), table, table,
                  per_head(c_len, c_len), per_head(c_len), per_head(c_len), per_head(1), per_head(1)],
        out_specs=pl.BlockSpec((None, SEQ, HEAD_DIM), lambda bi, hi: (bi, 0, hi)),
        out_shape=jax.ShapeDtypeStruct((b, SEQ, RET_W), bf16),
        compiler_params=_cparams("parallel", "parallel"),
        name="retention",
    )(h, h, h, h, cos_t, sin_t, dmask, k_dec, q_dec, chunk_dec, ret_g.reshape(RET_HEADS, 1, HEAD_DIM))


def _nsa_body(q_ref, kc_ref, vc_ref, ks_ref, vs_ref, kw_ref, vw_ref, nz_ref, ng_ref,
              qg_ref, kg_ref, posk_ref, posv_ref, w1k_ref, w2k_ref, w1v_ref, w2v_ref, ovl_ref, gsel_ref, eye_ref,
              o_ref, kcmp_s, vcmp_s, ksa_s, vsa_s, kwa_s, vwa_s, qn_s, not_sel_s):
    tq = NSA_TQ
    sq = NSA_SELECT_TQ
    rep = NSA_REP
    half_blk = CMP_LEN // 2
    d = HEAD_DIM
    exp2_scale = SCALE * float(np.log2(np.e))

    def compress(t_ref, pos_ref, w1_ref, w2_ref):
        first = jnp.zeros((SEQ // CMP_STRIDE, d), f32)
        second = jnp.zeros((SEQ // CMP_STRIDE, d), f32)
        for i in range(half_blk):
            ti = t_ref[pl.ds(i, SEQ // CMP_STRIDE, stride=CMP_STRIDE), :]
            first = first + _dot((ti + pos_ref[i:i + 1, :]).astype(bf16), w1_ref[i * d:(i + 1) * d, :])
            j = half_blk + i
            second = second + _dot((ti + pos_ref[j:j + 1, :]).astype(bf16), w1_ref[j * d:(j + 1) * d, :])
        pre = first + pltpu.roll(second, SEQ // CMP_STRIDE - 1, 0)
        return _dot(jax.nn.gelu(pre).astype(bf16), w2_ref[...])

    kcmp_s[...] = _rms(compress(kc_ref, posk_ref, w1k_ref, w2k_ref), kg_ref[0:1, :]).astype(bf16)
    vcmp_s[...] = compress(vc_ref, posv_ref, w1v_ref, w2v_ref).astype(bf16)
    lane0 = lax.broadcasted_iota(jnp.int32, (tq, d), 1) == 0
    pc_rows = NSA_PREP_ROWS
    ones = jnp.ones((pc_rows, d), bf16)
    zeros = jnp.zeros((pc_rows, d), bf16)
    for c in range(NSA_WINDOW // pc_rows):
        rows = slice(c * pc_rows, (c + 1) * pc_rows)
        kwa_s[rows, 0:d] = zeros
        kwa_s[rows, d:2 * d] = jnp.where(lax.broadcasted_iota(jnp.int32, (pc_rows, d), 1) == 0, NEG_INF, 0.0).astype(bf16)
        vwa_s[rows, 0:d] = zeros
        vwa_s[rows, d:2 * d] = zeros
    for c in range(SEQ // pc_rows):
        rows = slice(c * pc_rows, (c + 1) * pc_rows)
        wrows = slice(NSA_WINDOW + c * pc_rows, NSA_WINDOW + (c + 1) * pc_rows)
        ksa_s[rows, 0:d] = _rms(ks_ref[rows, :], kg_ref[1:2, :]).astype(bf16)
        ksa_s[rows, d:2 * d] = gsel_ref[rows, :]
        vsa_s[rows, 0:d] = vs_ref[rows, :].astype(bf16)
        vsa_s[rows, d:2 * d] = ones
        kwa_s[wrows, 0:d] = _rms(kw_ref[rows, :].astype(f32), kg_ref[2:3, :]).astype(bf16)
        kwa_s[wrows, d:2 * d] = zeros
        vwa_s[wrows, 0:d] = vw_ref[rows, :]
        vwa_s[wrows, d:2 * d] = ones

    row_i = lax.broadcasted_iota(jnp.int32, (tq, tq), 0)
    col_i = lax.broadcasted_iota(jnp.int32, (tq, tq), 1)
    band_first = jnp.where(col_i > row_i, 0.0, NEG_INF)
    band_last = jnp.where(col_i <= row_i, 0.0, NEG_INF)
    pad_flag = jnp.where(lane0, 1.0, 0.0).astype(bf16)

    def softmax_update(s, m, acc, v):
        n = s.shape[-1]
        m_new = jnp.maximum(m, jnp.max(s, axis=-1, keepdims=True))
        alpha = jnp.exp2((m - m_new) * exp2_scale)
        e = jnp.exp2((s - m_new) * exp2_scale)
        pv = _dot(e.reshape(rep * tq, n).astype(bf16), v).reshape(rep, tq, 2 * d)
        return m_new, alpha * acc + pv

    def select_tile(i, carry):
        t0 = i * sq
        rows = pl.ds(pl.multiple_of(t0, sq), sq)
        qs = [_rms(q_ref[rows, r * d:(r + 1) * d].astype(f32), qg_ref[...]).astype(bf16) for r in range(rep)]
        for r in range(rep):
            qn_s[r, rows, :] = qs[r]
        tpos = t0 + lax.broadcasted_iota(jnp.int32, (sq, 1), 0)
        qst = jnp.concatenate(qs, axis=0)
        n_idx = lax.broadcasted_iota(jnp.int32, (1, LANE), 1)
        valid = (n_idx * CMP_STRIDE + (CMP_LEN - 1) <= tpos) & (n_idx < N_CMP)
        cbias = jnp.where(valid, 0.0, NEG_INF)
        sc = (_dot_nt(qst, kcmp_s[...]) * SCALE).reshape(rep, sq, LANE) + cbias[None]
        mc = jnp.max(sc, axis=-1, keepdims=True)
        ec = jnp.exp(sc - mc)
        any_valid = jnp.where(tpos >= CMP_LEN - 1, 1.0, 0.0)
        pc = ec / jnp.sum(ec, axis=-1, keepdims=True) * any_valid[None]
        o_cmp = _dot(pc.reshape(rep * sq, LANE).astype(bf16), vcmp_s[...]).reshape(rep, sq, d)
        gates = jax.nn.sigmoid(ng_ref[rows, :].astype(f32))
        for r in range(rep):
            o_ref[rows, r * d:(r + 1) * d] = (gates[:, r:r + 1] * o_cmp[r]).astype(o_ref.dtype)
        psum = pc[0]
        for r in range(1, rep):
            psum = psum + pc[r]

        p_hi = psum.astype(bf16)
        rest = psum - p_hi.astype(f32)
        p_mid = rest.astype(bf16)
        p_lo = (rest - p_mid.astype(f32)).astype(bf16)
        imp = (_dot_nt(ovl_ref[...], p_hi) + _dot_nt(ovl_ref[...], p_mid) + _dot_nt(ovl_ref[...], p_lo))[:N_SEL, :]
        jj = lax.broadcasted_iota(jnp.int32, (N_SEL, 1), 0)
        cur = (t0 + lax.broadcasted_iota(jnp.int32, (1, sq), 1)) // SEL_BLOCK
        forced = (jj == 0) | (jj == cur) | (jj == cur - 1)
        imp = jnp.where(forced, -NEG_INF, jnp.where(jj > cur, NEG_INF, imp))
        rank = jnp.zeros((N_SEL, sq), f32)
        for j2 in range(N_SEL):
            row = imp[j2:j2 + 1, :]
            beats = (row > imp) | ((row == imp) & (j2 < jj))
            rank = rank + jnp.where(beats, 1.0, 0.0)
        not_sel_t = jnp.where(rank < SEL_TOPK, 0.0, 1.0).astype(bf16)
        not_sel_s[rows, :] = _dot_tn(not_sel_t, eye_ref[...]).astype(bf16)
        return carry

    lax.fori_loop(0, SEQ // sq, select_tile, 0)

    def make_tile(n_full):
        def tile(i, carry):
            t0 = i * tq
            rows = pl.ds(pl.multiple_of(t0, tq), tq)
            qs = [qn_s[r, rows, :] for r in range(rep)]
            tpos = t0 + lax.broadcasted_iota(jnp.int32, (tq, 1), 0)
            m_init = jnp.full((rep, tq, 1), NEG_INF, f32)
            acc_init = jnp.zeros((rep, tq, 2 * d), f32)

            qw = jnp.concatenate([jnp.concatenate([q, pad_flag], axis=1) for q in qs], axis=0)
            wrows = pl.ds(pl.multiple_of(t0, tq), WIN_SPAN)
            sw = _dot_nt(qw, kwa_s[wrows, :]).reshape(rep, tq, WIN_SPAN)
            sw = jnp.concatenate([sw[:, :, :tq] + band_first[None], sw[:, :, tq:NSA_WINDOW],
                                  sw[:, :, NSA_WINDOW:] + band_last[None]], axis=2)
            _, acc_w = softmax_update(sw, m_init, acc_init, vwa_s[wrows, :])
            o_win = acc_w[:, :, :d] / acc_w[:, :, d:]

            not_sel = not_sel_s[rows, :]
            qsel = jnp.concatenate([jnp.concatenate([q, not_sel], axis=1) for q in qs], axis=0)
            m_s, acc_s = m_init, acc_init
            for kt in range(n_full + 1):
                krows = slice(kt * SEL_TK, (kt + 1) * SEL_TK)
                s = _dot_nt(qsel, ksa_s[krows, :]).reshape(rep, tq, SEL_TK)
                if kt == n_full:
                    kpos = kt * SEL_TK + lax.broadcasted_iota(jnp.int32, (1, SEL_TK), 1)
                    s = s + jnp.where(kpos <= tpos, 0.0, NEG_INF)[None]
                m_s, acc_s = softmax_update(s, m_s, acc_s, vsa_s[krows, :])
            o_sel = acc_s[:, :, :d] / acc_s[:, :, d:]

            gates = jax.nn.sigmoid(ng_ref[rows, :].astype(f32))
            for r in range(rep):
                g_s = gates[:, rep + r:rep + r + 1]
                g_w = gates[:, 2 * rep + r:2 * rep + r + 1]
                cols = slice(r * d, (r + 1) * d)
                o = o_ref[rows, cols].astype(f32) + g_s * o_sel[r] + g_w * o_win[r]
                o_ref[rows, cols] = (o * _silu(nz_ref[rows, cols].astype(f32))).astype(o_ref.dtype)
            return carry
        return tile

    tiles_per_key_tile = SEL_TK // tq
    for n_full in range(SEQ // SEL_TK):
        lax.fori_loop(n_full * tiles_per_key_tile, (n_full + 1) * tiles_per_key_tile, make_tile(n_full), 0,
                      unroll=NSA_TILE_UNROLL)


def _nsa(h, hf, q_g, k_g, pos_k, pos_v, k_w1, k_w2, v_w1, v_w2):
    b = h.shape[0]
    grp_w = NSA_REP * HEAD_DIM
    cs = np.arange(LANE) * CMP_STRIDE
    ss = np.arange(LANE) * SEL_BLOCK
    ovl = ((cs[None, :] < ss[:, None] + SEL_BLOCK) & (cs[None, :] + CMP_LEN > ss[:, None])
           & (np.arange(LANE)[None, :] < N_CMP) & (np.arange(LANE)[:, None] < N_SEL))
    ovl_t = jnp.asarray(ovl, dtype=bf16)
    key_blk = np.arange(SEQ) // SEL_BLOCK
    gsel = jnp.asarray(np.where(key_blk[:, None] == np.arange(LANE)[None, :], NEG_INF, 0.0), dtype=bf16)
    eye = jnp.asarray(np.eye(N_SEL, LANE), dtype=bf16)

    def wide(off):
        return pl.BlockSpec((None, SEQ, grp_w), lambda bi, gi: (bi, 0, off // NSA_REP + gi))

    def col(off):
        return pl.BlockSpec((None, SEQ, HEAD_DIM), lambda bi, gi: (bi, 0, off + gi))

    def full(shape):
        return pl.BlockSpec(shape, lambda bi, gi: (0,) * len(shape))

    flat = CMP_LEN * HEAD_DIM
    return pl.pallas_call(
        _nsa_body,
        grid=(b, NSA_KV_HEADS),
        in_specs=[wide(EV_NQ), col(EV_F32_KC), col(EV_F32_VC), col(EV_F32_KS), col(EV_F32_VS), col(EV_KW), col(EV_VW),
                  wide(EV_NZ), col(EV_NG),
                  full((1, HEAD_DIM)), full((NSA_N_BRANCH, HEAD_DIM)), full((CMP_LEN, HEAD_DIM)), full((CMP_LEN, HEAD_DIM)),
                  full((flat, HEAD_DIM)), full((HEAD_DIM, HEAD_DIM)), full((flat, HEAD_DIM)), full((HEAD_DIM, HEAD_DIM)),
                  full((LANE, LANE)), full((SEQ, LANE)), full((N_SEL, LANE))],
        out_specs=pl.BlockSpec((None, SEQ, grp_w), lambda bi, gi: (bi, 0, gi)),
        out_shape=jax.ShapeDtypeStruct((b, SEQ, NSA_HEADS * HEAD_DIM), bf16),
        scratch_shapes=[pltpu.VMEM((SEQ // CMP_STRIDE, HEAD_DIM), bf16), pltpu.VMEM((SEQ // CMP_STRIDE, HEAD_DIM), bf16),
                        pltpu.VMEM((SEQ, 2 * HEAD_DIM), bf16), pltpu.VMEM((SEQ, 2 * HEAD_DIM), bf16),
                        pltpu.VMEM((SEQ + NSA_WINDOW, 2 * HEAD_DIM), bf16), pltpu.VMEM((SEQ + NSA_WINDOW, 2 * HEAD_DIM), bf16),
                        pltpu.VMEM((NSA_REP, SEQ, HEAD_DIM), bf16), pltpu.VMEM((SEQ, LANE), bf16)],
        compiler_params=_cparams("parallel", "parallel"),
        name="nsa",
    )(h, hf, hf, hf, hf, h, h, h, h,
      q_g.reshape(1, HEAD_DIM), k_g, pos_k, pos_v,
      k_w1.astype(bf16), k_w2.astype(bf16), v_w1.astype(bf16), v_w2.astype(bf16), ovl_t, gsel, eye)


def _conv_body(u_ref, g_ref, z_ref, w_ref, b_ref, lg_ref, lb_ref, o_ref, a_ext, a_sh, y_s):
    ext = CONV_HALO + CONV_T
    n_cb = CONV_W // LANE
    lead = CONV_HALO - (CONV_K - 1)

    @pl.when(pl.program_id(1) == 0)
    def _():
        for cb in range(n_cb):
            a_ext[cb, 0:CONV_HALO, :] = jnp.zeros((CONV_HALO, LANE), f32)
            a_ext[cb, ext:ext + SUBLANE, :] = jnp.zeros((SUBLANE, LANE), f32)

    for cb in range(n_cb):
        lanes = slice(cb * LANE, (cb + 1) * LANE)
        for c in range(CONV_T // 128):
            rows = slice(c * 128, (c + 1) * 128)
            a_ext[cb, CONV_HALO + c * 128:CONV_HALO + (c + 1) * 128, :] = u_ref[rows, lanes] * jax.nn.sigmoid(g_ref[rows, lanes])
        for r in range(1, SUBLANE):
            for c in range(ext // CONV_HALO):
                a_sh[r - 1, cb, c * CONV_HALO:(c + 1) * CONV_HALO, :] = a_ext[cb, c * CONV_HALO + r:(c + 1) * CONV_HALO + r, :]

        taps_w = [jnp.broadcast_to(w_ref[k:k + 1, lanes], (CONV_RB, LANE)) for k in range(CONV_K)]
        bias = jnp.broadcast_to(b_ref[:, lanes], (CONV_RB, LANE))

        def conv_rows(rb, carry, cb=cb, taps_w=taps_w, bias=bias):
            r0 = pl.multiple_of(rb * CONV_RB, CONV_RB)
            parts = [bias, None]
            for k in range(CONV_K):
                shift, base = (lead + k) % SUBLANE, (lead + k) // SUBLANE * SUBLANE
                rows = pl.ds(r0 + base, CONV_RB)
                tap = a_ext[cb, rows, :] if shift == 0 else a_sh[shift - 1, cb, rows, :]
                term = tap * taps_w[k]
                parts[k % 2] = term if parts[k % 2] is None else parts[k % 2] + term
            y_s[cb, pl.ds(r0, CONV_RB), :] = parts[0] + parts[1]
            return carry

        lax.fori_loop(0, CONV_T // CONV_RB, conv_rows, 0)
        a_ext[cb, 0:CONV_HALO, :] = a_ext[cb, CONV_T:CONV_T + CONV_HALO, :]

    def norm_rows(rb, carry):
        rows = pl.ds(pl.multiple_of(rb * CONV_RB, CONV_RB), CONV_RB)
        y = jnp.concatenate([y_s[cb, rows, :] for cb in range(n_cb)], axis=1)
        mu = jnp.mean(y, axis=-1, keepdims=True)
        var = jnp.mean(jnp.square(y - mu), axis=-1, keepdims=True)
        yn = (y - mu) * lax.rsqrt(var + NORM_EPS) * lg_ref[...] + lb_ref[...]
        o_ref[rows, :] = (_silu(yn) * _silu(z_ref[rows, :])).astype(o_ref.dtype)
        return carry

    lax.fori_loop(0, CONV_T // CONV_RB, norm_rows, 0, unroll=2)


def _conv(h, dw_w, dw_b, ln_g, ln_b):
    b = h.shape[0]

    def col(off):
        return pl.BlockSpec((None, CONV_T, CONV_W), lambda bi, ti: (bi, ti, off * LANE // CONV_W))

    def full(rows):
        return pl.BlockSpec((rows, CONV_W), lambda bi, ti: (0, 0))

    return pl.pallas_call(
        _conv_body,
        grid=(b, SEQ // CONV_T),
        in_specs=[col(OD_CU), col(OD_CG), col(OD_CZ), full(CONV_K), full(1), full(1), full(1)],
        out_specs=pl.BlockSpec((None, CONV_T, CONV_W), lambda bi, ti: (bi, ti, 0)),
        out_shape=jax.ShapeDtypeStruct((b, SEQ, CONV_W), bf16),
        scratch_shapes=[pltpu.VMEM((CONV_W // LANE, CONV_HALO + CONV_T + SUBLANE, LANE), f32),
                        pltpu.VMEM((SUBLANE - 1, CONV_W // LANE, CONV_HALO + CONV_T, LANE), f32),
                        pltpu.VMEM((CONV_W // LANE, CONV_T, LANE), f32)],
        compiler_params=_cparams("parallel", "arbitrary"),
        name="conformer_conv",
    )(h, h, h, dw_w, dw_b.reshape(1, CONV_W), ln_g.reshape(1, CONV_W), ln_b.reshape(1, CONV_W))


def _dil_body(q0_ref, k0_ref, v0_ref, q1_ref, k1_ref, v1_ref, q2_ref, k2_ref, v2_ref, z_ref, qg_ref, kg_ref,
              o_ref, qn_s, kn_s, va_s, o0_s, o1_s, o2_s, l0_s, l1_s, l2_s):
    blk = DIL_BLOCK
    n_tiles = SEQ // blk
    groups = ((q0_ref, k0_ref, v0_ref, o0_s, l0_s), (q1_ref, k1_ref, v1_ref, o1_s, l1_s), (q2_ref, k2_ref, v2_ref, o2_s, l2_s))
    kn_s[0:blk, :] = jnp.zeros((blk, HEAD_DIM), bf16)
    va_s[0:blk, :] = jnp.zeros((blk, 2 * HEAD_DIM), bf16)
    ones = jnp.ones((blk, HEAD_DIM), bf16)
    row2 = lax.broadcasted_iota(jnp.int32, (blk, 2 * blk), 0)
    col2 = lax.broadcasted_iota(jnp.int32, (blk, 2 * blk), 1)
    band_prev = jnp.where((col2 >= row2) & (col2 <= row2 + blk), 0.0, NEG_INF)
    band_first = jnp.where((col2 >= blk) & (col2 <= row2 + blk), 0.0, NEG_INF)

    for (q_ref, k_ref, v_ref, o_s, l_s), (window, dil) in zip(groups, DIL_PATTERNS):
        assert window // dil == blk
        nb = SEQ // dil // blk

        def natural_rows(u, dil=dil, nb=nb):
            start = u // nb + (u % nb) * (blk * dil)
            return pl.ds(start, blk, stride=dil) if dil > 1 else pl.ds(pl.multiple_of(start, blk), blk)

        def prep(u, carry, q_ref=q_ref, k_ref=k_ref, v_ref=v_ref, natural_rows=natural_rows):
            src = natural_rows(u)
            qn_s[pl.ds(pl.multiple_of(u * blk, blk), blk), :] = _rms(q_ref[src, :], qg_ref[...]).astype(bf16)
            dst = pl.ds(pl.multiple_of((u + 1) * blk, blk), blk)
            kn_s[dst, :] = _rms(k_ref[src, :], kg_ref[...]).astype(bf16)
            va_s[dst, :] = jnp.concatenate([v_ref[src, :].astype(bf16), ones], axis=1)
            return carry

        lax.fori_loop(0, n_tiles, prep, 0, unroll=DIL_UNROLL)

        def att(u, carry, o_s=o_s, l_s=l_s, nb=nb, natural_rows=natural_rows):
            q = qn_s[pl.ds(pl.multiple_of(u * blk, blk), blk), :]
            keys = pl.ds(pl.multiple_of(u * blk, blk), 2 * blk)
            band = jnp.where((u % nb) > 0, band_prev, band_first)
            s = _dot_nt(q, kn_s[keys, :]) * SCALE + band
            m = jnp.max(s, axis=-1, keepdims=True)
            pv = _dot(jnp.exp(s - m).astype(bf16), va_s[keys, :])
            den = pv[:, HEAD_DIM:]
            dst = natural_rows(u)
            o_s[dst, :] = pv[:, :HEAD_DIM] / den
            l_s[dst, :] = m + jnp.log(den)
            return carry

        lax.fori_loop(0, n_tiles, att, 0, unroll=DIL_UNROLL)

    def mix(c, carry):
        rows = pl.ds(pl.multiple_of(c * blk, blk), blk)
        l0, l1, l2 = l0_s[rows, :], l1_s[rows, :], l2_s[rows, :]
        m = jnp.maximum(jnp.maximum(l0, l1), l2)
        w0, w1, w2 = jnp.exp(l0 - m), jnp.exp(l1 - m), jnp.exp(l2 - m)
        o = (w0 * o0_s[rows, :] + w1 * o1_s[rows, :] + w2 * o2_s[rows, :]) / (w0 + w1 + w2)
        o_ref[rows, :] = (o * _silu(z_ref[rows, :])).astype(o_ref.dtype)
        return carry

    lax.fori_loop(0, n_tiles, mix, 0, unroll=DIL_UNROLL)


def _dilated(h, q_g, k_g):
    b = h.shape[0]

    def col(off):
        return pl.BlockSpec((None, SEQ, HEAD_DIM), lambda bi, hi: (bi, 0, off + hi))

    gain = pl.BlockSpec((1, HEAD_DIM), lambda bi, hi: (0, 0))
    specs = []
    for gi in range(len(DIL_PATTERNS)):
        specs += [col(OD_DQ + gi * DIL_GROUP_HEADS), col(OD_DK + gi * DIL_GROUP_HEADS), col(OD_DV + gi * DIL_GROUP_HEADS)]
    return pl.pallas_call(
        _dil_body,
        grid=(b, DIL_GROUP_HEADS),
        in_specs=specs + [col(OD_DZ), gain, gain],
        out_specs=pl.BlockSpec((None, SEQ, HEAD_DIM), lambda bi, hi: (bi, 0, hi)),
        out_shape=jax.ShapeDtypeStruct((b, SEQ, DIL_GROUP_HEADS * HEAD_DIM), bf16),
        scratch_shapes=[pltpu.VMEM((SEQ, HEAD_DIM), bf16), pltpu.VMEM((SEQ + DIL_BLOCK, HEAD_DIM), bf16),
                        pltpu.VMEM((SEQ + DIL_BLOCK, 2 * HEAD_DIM), bf16)] + [pltpu.VMEM((SEQ, HEAD_DIM), f32)] * 6,
        compiler_params=_cparams("parallel", "parallel"),
        name="dilated",
    )(*([h] * 10), q_g.reshape(1, HEAD_DIM), k_g.reshape(1, HEAD_DIM))


def _even_w_in(w):
    ng0 = EV_W_NZ * LANE
    nz0 = ng0 + NSA_N_BRANCH * NSA_HEADS
    ng = w[:, ng0:nz0].reshape(-1, NSA_N_BRANCH, NSA_KV_HEADS, NSA_REP)
    ng = ng.transpose(0, 2, 1, 3).reshape(-1, NSA_KV_HEADS, NSA_N_BRANCH * NSA_REP)
    ng = jnp.pad(ng, ((0, 0), (0, 0), (0, LANE - NSA_N_BRANCH * NSA_REP))).reshape(-1, NSA_KV_HEADS * LANE)
    out = jnp.concatenate([w[:, :ng0], w[:, nz0:], ng], axis=1)
    return jnp.pad(out, ((0, 0), (0, EV_COLS - out.shape[1]))).astype(bf16)


def kernel(x, ev_norm, ev_w_in, ev_w_out, ev_ret_norm, ev_nsa_q_norm, ev_nsa_k_norm, ev_cmp_pos_k, ev_cmp_pos_v, ev_cmp_k_w1, ev_cmp_k_w2, ev_cmp_v_w1, ev_cmp_v_w2, od_norm, od_w_in, od_w_out, od_dw_w, od_dw_b, od_conv_norm_g, od_conv_norm_b, od_dil_q_norm, od_dil_k_norm):
    b, s, d = x.shape
    assert (s, d) == (SEQ, D_MODEL)
    x0 = x.reshape(b * s, d)

    h0, h0f = _proj_in(x0, ev_norm[0], _even_w_in(ev_w_in[0]), EV_F32_TILE, 1)
    h0, h0f = h0.reshape(b, s, -1), h0f.reshape(b, s, -1)
    a_out = _retention(h0, ev_ret_norm[0])
    b_out = _nsa(h0, h0f, ev_nsa_q_norm[0], ev_nsa_k_norm[0], ev_cmp_pos_k[0], ev_cmp_pos_v[0],
                 ev_cmp_k_w1[0], ev_cmp_k_w2[0], ev_cmp_v_w1[0], ev_cmp_v_w2[0])
    x1 = _proj_out(x0, a_out.reshape(b * s, -1), b_out.reshape(b * s, -1), ev_w_out[0].astype(bf16))

    _, h1 = _proj_in(x1, od_norm[0], od_w_in[0].astype(bf16), 0, OD_COLS // PROJ_TN)
    h1 = h1.reshape(b, s, -1)
    c_out = _conv(h1, od_dw_w[0], od_dw_b[0], od_conv_norm_g[0], od_conv_norm_b[0])
    d_out = _dilated(h1, od_dil_q_norm[0], od_dil_k_norm[0])
    x2 = _proj_out(x1, c_out.reshape(b * s, -1), d_out.reshape(b * s, -1), od_w_out[0].astype(bf16))
    return x2.reshape(b, s, d)
```

```python
import functools

import jax
import jax.numpy as jnp
import numpy as np
from jax import lax
from jax.experimental import pallas as pl
from jax.experimental.pallas import tpu as pltpu

f32 = jnp.float32
bf16 = jnp.bfloat16

D_MODEL = 2048
SEQ = 2048
HEAD_DIM = 128
SCALE = HEAD_DIM ** -0.5
NORM_EPS = 1e-6
NEG_INF = -1e30
LANE = 128
SUBLANE = 8

RET_HEADS = 8
RET_W = RET_HEADS * HEAD_DIM
RET_CHUNK = 256
RET_UNROLL = 8
ROPE_BASE = 10000.0
NSA_HEADS = 8
NSA_KV_HEADS = 2
NSA_REP = NSA_HEADS // NSA_KV_HEADS
NSA_N_BRANCH = 3
CMP_LEN = 32
CMP_STRIDE = 16
N_CMP = (SEQ - CMP_LEN) // CMP_STRIDE + 1
SEL_BLOCK = 64
N_SEL = SEQ // SEL_BLOCK
SEL_TOPK = 16
NSA_WINDOW = 512
NSA_TQ = 128
SEL_TK = 512
NSA_PREP_ROWS = 256
NSA_SELECT_TQ = 512
WIN_SPAN = NSA_WINDOW + NSA_TQ
NSA_TILE_UNROLL = 2
CONV_W = 1024
CONV_K = 31
CONV_T = 512
CONV_HALO = 32
CONV_RB = 64
DIL_PATTERNS = ((128, 1), (512, 4), (2048, 16))
DIL_GROUP_HEADS = 4
DIL_HEADS = DIL_GROUP_HEADS * len(DIL_PATTERNS)
DIL_BLOCK = 128
DIL_UNROLL = 16

EV_RQ, EV_RK, EV_RV, EV_RZ = 0, 8, 16, 24
EV_NQ = 32
EV_W_NZ = 52
EV_COLS = 8192
EV_F32_TILE = 5
EV_F32_KC, EV_F32_VC, EV_F32_KS, EV_F32_VS = 0, 2, 4, 6
EV_KW, EV_VW = 40, 42
EV_NZ = 44
EV_NG = 52
OD_CU, OD_CG, OD_CZ = 0, 8, 16
OD_DQ, OD_DK, OD_DV = 24, 36, 48
OD_DZ = 60
OD_COLS = 8192

VMEM_LIMIT = 56 * 1024 * 1024


def _cparams(*sem):
    return pltpu.CompilerParams(dimension_semantics=sem, vmem_limit_bytes=VMEM_LIMIT)


def _rms(x, gain):
    ms = jnp.mean(x * x, axis=-1, keepdims=True)
    return x * lax.rsqrt(ms + NORM_EPS) * gain


def _silu(z):
    return z * jax.nn.sigmoid(z)


def _dot(a, b):
    return jnp.dot(a, b, preferred_element_type=f32)


def _dot_nt(a, b):
    return lax.dot_general(a, b, (((1,), (1,)), ((), ())), preferred_element_type=f32)


def _dot_tn(a, b):
    return lax.dot_general(a, b, (((0,), (0,)), ((), ())), preferred_element_type=f32)


PROJ_TM = 1024
PROJ_TN = 1024
NORM_ROWS = 256


def _proj_in_body(x_ref, g_ref, w_ref, *refs, f32_tile, f32_tiles, n_tiles):
    xn_ref = refs[-1]
    j = pl.program_id(1)

    @pl.when(j == 0)
    def _():
        for c in range(PROJ_TM // NORM_ROWS):
            rows = slice(c * NORM_ROWS, (c + 1) * NORM_ROWS)
            xn_ref[rows, :] = _rms(x_ref[rows, :], g_ref[...]).astype(bf16)

    def emit(out_ref):
        out_ref[...] = _dot(xn_ref[...], w_ref[...]).astype(out_ref.dtype)

    if f32_tiles == n_tiles:
        emit(refs[0])
    else:
        in_f32 = (j >= f32_tile) & (j < f32_tile + f32_tiles)
        pl.when(in_f32)(functools.partial(emit, refs[1]))
        pl.when(jnp.logical_not(in_f32))(functools.partial(emit, refs[0]))


def _proj_in(x2, gain, w, f32_tile, f32_tiles):
    m, d = x2.shape
    n = w.shape[1]
    n_tiles = n // PROJ_TN
    all_f32 = f32_tiles == n_tiles
    assert all_f32 or f32_tile >= 1
    bf16_spec = pl.BlockSpec((PROJ_TM, PROJ_TN), lambda i, j: (i, j - jnp.clip(j - f32_tile + 1, 0, f32_tiles)))
    f32_spec = pl.BlockSpec((PROJ_TM, PROJ_TN), lambda i, j: (i, jnp.clip(j - f32_tile, 0, f32_tiles - 1)))
    bf16_shape = jax.ShapeDtypeStruct((m, n - f32_tiles * PROJ_TN), bf16)
    f32_shape = jax.ShapeDtypeStruct((m, f32_tiles * PROJ_TN), f32)
    out = pl.pallas_call(
        functools.partial(_proj_in_body, f32_tile=f32_tile, f32_tiles=f32_tiles, n_tiles=n_tiles),
        grid=(m // PROJ_TM, n_tiles),
        in_specs=[
            pl.BlockSpec((PROJ_TM, d), lambda i, j: (i, 0)),
            pl.BlockSpec((1, d), lambda i, j: (0, 0)),
            pl.BlockSpec((d, PROJ_TN), lambda i, j: (0, j)),
        ],
        out_specs=[f32_spec] if all_f32 else [bf16_spec, f32_spec],
        out_shape=[f32_shape] if all_f32 else [bf16_shape, f32_shape],
        scratch_shapes=[pltpu.VMEM((PROJ_TM, d), bf16)],
        compiler_params=_cparams("parallel", "arbitrary"),
        name="proj_in",
    )(x2, gain.reshape(1, d), w)
    return (None, out[0]) if all_f32 else tuple(out)


OUT_TM = 512


def _proj_out_body(x_ref, m0_ref, m1_ref, w0_ref, w1_ref, o_ref):
    o_ref[...] = x_ref[...] + _dot(m0_ref[...], w0_ref[...]) + _dot(m1_ref[...], w1_ref[...])


def _proj_out(x2, m0, m1, w):
    m, d = x2.shape
    k0, k1 = m0.shape[1], m1.shape[1]
    assert w.shape == (k0 + k1, d) and k0 % k1 == 0
    return pl.pallas_call(
        _proj_out_body,
        grid=(m // OUT_TM,),
        in_specs=[
            pl.BlockSpec((OUT_TM, d), lambda i: (i, 0)),
            pl.BlockSpec((OUT_TM, k0), lambda i: (i, 0)),
            pl.BlockSpec((OUT_TM, k1), lambda i: (i, 0)),
            pl.BlockSpec((k0, d), lambda i: (0, 0)),
            pl.BlockSpec((k1, d), lambda i: (k0 // k1, 0)),
        ],
        out_specs=pl.BlockSpec((OUT_TM, d), lambda i: (i, 0)),
        out_shape=jax.ShapeDtypeStruct((m, d), f32),
        compiler_params=_cparams("parallel"),
        name="proj_out",
    )(x2, m0, m1, w, w)


def _ret_body(q_ref, k_ref, v_ref, z_ref, cos_ref, sin_ref, dmask_ref, kdec_ref, qdec_ref, cdec_ref, g_ref, o_ref):
    c_len = RET_CHUNK

    def rot(t, cs, sn):
        return t * cs + pltpu.roll(t, HEAD_DIM // 2, 1) * sn

    def body(n, state):
        rows = pl.ds(pl.multiple_of(n * c_len, c_len), c_len)
        cs = cos_ref[rows, :]
        sn = sin_ref[rows, :]
        q = rot(q_ref[rows, :].astype(f32), cs, sn)
        k = rot(k_ref[rows, :].astype(f32), cs, sn) * SCALE
        vb = v_ref[rows, :]
        s = _dot_nt(q.astype(bf16), k.astype(bf16)) * dmask_ref[...]
        o = _dot(s.astype(bf16), vb)
        o = o + _dot((q * qdec_ref[...]).astype(bf16), state.astype(bf16))
        kv = _dot_tn((k * kdec_ref[...]).astype(bf16), vb)
        state = cdec_ref[...] * state + kv
        y = _rms(o, g_ref[...])
        o_ref[rows, :] = (y * _silu(z_ref[rows, :].astype(f32))).astype(o_ref.dtype)
        return state

    lax.fori_loop(0, SEQ // c_len, body, jnp.zeros((HEAD_DIM, HEAD_DIM), f32), unroll=RET_UNROLL)


def _retention(h, ret_g):
    b = h.shape[0]
    c_len = RET_CHUNK
    half = HEAD_DIM // 2
    pos = jnp.arange(SEQ)
    inv = ROPE_BASE ** (-jnp.arange(half, dtype=f32) / half)
    ang = pos.astype(f32)[:, None] * inv[None, :]
    cos, sin = jnp.cos(ang), jnp.sin(ang)
    cos_t = jnp.concatenate([cos, cos], axis=-1)
    sin_t = jnp.concatenate([-sin, sin], axis=-1)
    log_g = jnp.log(1.0 - 2.0 ** (-5.0 - jnp.arange(RET_HEADS, dtype=f32)))
    idx = jnp.arange(c_len, dtype=f32)
    diff = idx[:, None] - idx[None, :]
    dmask = jnp.where(diff >= 0, jnp.exp(jnp.maximum(diff, 0.0)[None] * log_g[:, None, None]), 0.0)
    ones = jnp.ones((1, 1, HEAD_DIM), f32)
    k_dec = jnp.exp((c_len - 1 - idx)[None, :] * log_g[:, None])[:, :, None] * ones
    q_dec = jnp.exp((idx + 1.0)[None, :] * log_g[:, None])[:, :, None] * ones
    chunk_dec = jnp.exp(c_len * log_g)[:, None, None] * ones

    def col(off):
        return pl.BlockSpec((None, SEQ, HEAD_DIM), lambda bi, hi: (bi, 0, off + hi))

    table = pl.BlockSpec((SEQ, HEAD_DIM), lambda bi, hi: (0, 0))

    def per_head(rows, cols=HEAD_DIM):
        return pl.BlockSpec((None, rows, cols), lambda bi, hi: (hi, 0, 0))

    return pl.pallas_call(
        _ret_body,
        grid=(b, RET_HEADS),
        in_specs=[col(EV_RQ), col(EV_RK), col(EV_RV), col(EV_RZ), table, table,
                  per_head(c_len, c_len), per_head(c_len), per_head(c_len), per_head(1), per_head(1)],
        out_specs=pl.BlockSpec((None, SEQ, HEAD_DIM), lambda bi, hi: (bi, 0, hi)),
        out_shape=jax.ShapeDtypeStruct((b, SEQ, RET_W), bf16),
        compiler_params=_cparams("parallel", "parallel"),
        name="retention",
    )(h, h, h, h, cos_t, sin_t, dmask, k_dec, q_dec, chunk_dec, ret_g.reshape(RET_HEADS, 1, HEAD_DIM))


def _nsa_body(q_ref, kc_ref, vc_ref, ks_ref, vs_ref, kw_ref, vw_ref, nz_ref, ng_ref,
              qg_ref, kg_ref, posk_ref, posv_ref, w1k_ref, w2k_ref, w1v_ref, w2v_ref, ovl_ref, gsel_ref, eye_ref,
              o_ref, kcmp_s, vcmp_s, ksa_s, vsa_s, kwa_s, vwa_s, qn_s, not_sel_s):
    tq = NSA_TQ
    sq = NSA_SELECT_TQ
    rep = NSA_REP
    half_blk = CMP_LEN // 2
    d = HEAD_DIM
    exp2_scale = SCALE * float(np.log2(np.e))

    def compress(t_ref, pos_ref, w1_ref, w2_ref):
        first = jnp.zeros((SEQ // CMP_STRIDE, d), f32)
        second = jnp.zeros((SEQ // CMP_STRIDE, d), f32)
        for i in range(half_blk):
            ti = t_ref[pl.ds(i, SEQ // CMP_STRIDE, stride=CMP_STRIDE), :]
            first = first + _dot((ti + pos_ref[i:i + 1, :]).astype(bf16), w1_ref[i * d:(i + 1) * d, :])
            j = half_blk + i
            second = second + _dot((ti + pos_ref[j:j + 1, :]).astype(bf16), w1_ref[j * d:(j + 1) * d, :])
        pre = first + pltpu.roll(second, SEQ // CMP_STRIDE - 1, 0)
        return _dot(jax.nn.gelu(pre).astype(bf16), w2_ref[...])

    kcmp_s[...] = _rms(compress(kc_ref, posk_ref, w1k_ref, w2k_ref), kg_ref[0:1, :]).astype(bf16)
    vcmp_s[...] = compress(vc_ref, posv_ref, w1v_ref, w2v_ref).astype(bf16)
    lane0 = lax.broadcasted_iota(jnp.int32, (tq, d), 1) == 0
    pc_rows = NSA_PREP_ROWS
    ones = jnp.ones((pc_rows, d), bf16)
    zeros = jnp.zeros((pc_rows, d), bf16)
    for c in range(NSA_WINDOW // pc_rows):
        rows = slice(c * pc_rows, (c + 1) * pc_rows)
        kwa_s[rows, 0:d] = zeros
        kwa_s[rows, d:2 * d] = jnp.where(lax.broadcasted_iota(jnp.int32, (pc_rows, d), 1) == 0, NEG_INF, 0.0).astype(bf16)
        vwa_s[rows, 0:d] = zeros
        vwa_s[rows, d:2 * d] = zeros
    for c in range(SEQ // pc_rows):
        rows = slice(c * pc_rows, (c + 1) * pc_rows)
        wrows = slice(NSA_WINDOW + c * pc_rows, NSA_WINDOW + (c + 1) * pc_rows)
        ksa_s[rows, 0:d] = _rms(ks_ref[rows, :], kg_ref[1:2, :]).astype(bf16)
        ksa_s[rows, d:2 * d] = gsel_ref[rows, :]
        vsa_s[rows, 0:d] = vs_ref[rows, :].astype(bf16)
        vsa_s[rows, d:2 * d] = ones
        kwa_s[wrows, 0:d] = _rms(kw_ref[rows, :].astype(f32), kg_ref[2:3, :]).astype(bf16)
        kwa_s[wrows, d:2 * d] = zeros
        vwa_s[wrows, 0:d] = vw_ref[rows, :]
        vwa_s[wrows, d:2 * d] = ones

    row_i = lax.broadcasted_iota(jnp.int32, (tq, tq), 0)
    col_i = lax.broadcasted_iota(jnp.int32, (tq, tq), 1)
    band_first = jnp.where(col_i > row_i, 0.0, NEG_INF)
    band_last = jnp.where(col_i <= row_i, 0.0, NEG_INF)
    pad_flag = jnp.where(lane0, 1.0, 0.0).astype(bf16)

    def softmax_update(s, m, acc, v):
        n = s.shape[-1]
        m_new = jnp.maximum(m, jnp.max(s, axis=-1, keepdims=True))
        alpha = jnp.exp2((m - m_new) * exp2_scale)
        e = jnp.exp2((s - m_new) * exp2_scale)
        pv = _dot(e.reshape(rep * tq, n).astype(bf16), v).reshape(rep, tq, 2 * d)
        return m_new, alpha * acc + pv

    def select_tile(i, carry):
        t0 = i * sq
        rows = pl.ds(pl.multiple_of(t0, sq), sq)
        qs = [_rms(q_ref[rows, r * d:(r + 1) * d].astype(f32), qg_ref[...]).astype(bf16) for r in range(rep)]
        for r in range(rep):
            qn_s[r, rows, :] = qs[r]
        tpos = t0 + lax.broadcasted_iota(jnp.int32, (sq, 1), 0)
        qst = jnp.concatenate(qs, axis=0)
        n_idx = lax.broadcasted_iota(jnp.int32, (1, LANE), 1)
        valid = (n_idx * CMP_STRIDE + (CMP_LEN - 1) <= tpos) & (n_idx < N_CMP)
        cbias = jnp.where(valid, 0.0, NEG_INF)
        sc = (_dot_nt(qst, kcmp_s[...]) * SCALE).reshape(rep, sq, LANE) + cbias[None]
        mc = jnp.max(sc, axis=-1, keepdims=True)
        ec = jnp.exp(sc - mc)
        any_valid = jnp.where(tpos >= CMP_LEN - 1, 1.0, 0.0)
        pc = ec / jnp.sum(ec, axis=-1, keepdims=True) * any_valid[None]
        o_cmp = _dot(pc.reshape(rep * sq, LANE).astype(bf16), vcmp_s[...]).reshape(rep, sq, d)
        gates = jax.nn.sigmoid(ng_ref[rows, :].astype(f32))
        for r in range(rep):
            o_ref[rows, r * d:(r + 1) * d] = (gates[:, r:r + 1] * o_cmp[r]).astype(o_ref.dtype)
        psum = pc[0]
        for r in range(1, rep):
            psum = psum + pc[r]

        p_hi = psum.astype(bf16)
        rest = psum - p_hi.astype(f32)
        p_mid = rest.astype(bf16)
        p_lo = (rest - p_mid.astype(f32)).astype(bf16)
        imp = (_dot_nt(ovl_ref[...], p_hi) + _dot_nt(ovl_ref[...], p_mid) + _dot_nt(ovl_ref[...], p_lo))[:N_SEL, :]
        jj = lax.broadcasted_iota(jnp.int32, (N_SEL, 1), 0)
        cur = (t0 + lax.broadcasted_iota(jnp.int32, (1, sq), 1)) // SEL_BLOCK
        forced = (jj == 0) | (jj == cur) | (jj == cur - 1)
        imp = jnp.where(forced, -NEG_INF, jnp.where(jj > cur, NEG_INF, imp))
        rank = jnp.zeros((N_SEL, sq), f32)
        for j2 in range(N_SEL):
            row = imp[j2:j2 + 1, :]
            beats = (row > imp) | ((row == imp) & (j2 < jj))
            rank = rank + jnp.where(beats, 1.0, 0.0)
        not_sel_t = jnp.where(rank < SEL_TOPK, 0.0, 1.0).astype(bf16)
        not_sel_s[rows, :] = _dot_tn(not_sel_t, eye_ref[...]).astype(bf16)
        return carry

    lax.fori_loop(0, SEQ // sq, select_tile, 0)

    def make_tile(n_full):
        def tile(i, carry):
            t0 = i * tq
            rows = pl.ds(pl.multiple_of(t0, tq), tq)
            qs = [qn_s[r, rows, :] for r in range(rep)]
            tpos = t0 + lax.broadcasted_iota(jnp.int32, (tq, 1), 0)
            m_init = jnp.full((rep, tq, 1), NEG_INF, f32)
            acc_init = jnp.zeros((rep, tq, 2 * d), f32)

            qw = jnp.concatenate([jnp.concatenate([q, pad_flag], axis=1) for q in qs], axis=0)
            wrows = pl.ds(pl.multiple_of(t0, tq), WIN_SPAN)
            sw = _dot_nt(qw, kwa_s[wrows, :]).reshape(rep, tq, WIN_SPAN)
            sw = jnp.concatenate([sw[:, :, :tq] + band_first[None], sw[:, :, tq:NSA_WINDOW],
                                  sw[:, :, NSA_WINDOW:] + band_last[None]], axis=2)
            _, acc_w = softmax_update(sw, m_init, acc_init, vwa_s[wrows, :])
            o_win = acc_w[:, :, :d] / acc_w[:, :, d:]

            not_sel = not_sel_s[rows, :]
            qsel = jnp.concatenate([jnp.concatenate([q, not_sel], axis=1) for q in qs], axis=0)
            m_s, acc_s = m_init, acc_init
            for kt in range(n_full + 1):
                krows = slice(kt * SEL_TK, (kt + 1) * SEL_TK)
                s = _dot_nt(qsel, ksa_s[krows, :]).reshape(rep, tq, SEL_TK)
                if kt == n_full:
                    kpos = kt * SEL_TK + lax.broadcasted_iota(jnp.int32, (1, SEL_TK), 1)
                    s = s + jnp.where(kpos <= tpos, 0.0, NEG_INF)[None]
                m_s, acc_s = softmax_update(s, m_s, acc_s, vsa_s[krows, :])
            o_sel = acc_s[:, :, :d] / acc_s[:, :, d:]

            gates = jax.nn.sigmoid(ng_ref[rows, :].astype(f32))
            for r in range(rep):
                g_s = gates[:, rep + r:rep + r + 1]
                g_w = gates[:, 2 * rep + r:2 * rep + r + 1]
                cols = slice(r * d, (r + 1) * d)
                o = o_ref[rows, cols].astype(f32) + g_s * o_sel[r] + g_w * o_win[r]
                o_ref[rows, cols] = (o * _silu(nz_ref[rows, cols].astype(f32))).astype(o_ref.dtype)
            return carry
        return tile

    tiles_per_key_tile = SEL_TK // tq
    for n_full in range(SEQ // SEL_TK):
        lax.fori_loop(n_full * tiles_per_key_tile, (n_full + 1) * tiles_per_key_tile, make_tile(n_full), 0,
                      unroll=NSA_TILE_UNROLL)


def _nsa(h, hf, q_g, k_g, pos_k, pos_v, k_w1, k_w2, v_w1, v_w2):
    b = h.shape[0]
    grp_w = NSA_REP * HEAD_DIM
    cs = np.arange(LANE) * CMP_STRIDE
    ss = np.arange(LANE) * SEL_BLOCK
    ovl = ((cs[None, :] < ss[:, None] + SEL_BLOCK) & (cs[None, :] + CMP_LEN > ss[:, None])
           & (np.arange(LANE)[None, :] < N_CMP) & (np.arange(LANE)[:, None] < N_SEL))
    ovl_t = jnp.asarray(ovl, dtype=bf16)
    key_blk = np.arange(SEQ) // SEL_BLOCK
    gsel = jnp.asarray(np.where(key_blk[:, None] == np.arange(LANE)[None, :], NEG_INF, 0.0), dtype=bf16)
    eye = jnp.asarray(np.eye(N_SEL, LANE), dtype=bf16)

    def wide(off):
        return pl.BlockSpec((None, SEQ, grp_w), lambda bi, gi: (bi, 0, off // NSA_REP + gi))

    def col(off):
        return pl.BlockSpec((None, SEQ, HEAD_DIM), lambda bi, gi: (bi, 0, off + gi))

    def full(shape):
        return pl.BlockSpec(shape, lambda bi, gi: (0,) * len(shape))

    flat = CMP_LEN * HEAD_DIM
    return pl.pallas_call(
        _nsa_body,
        grid=(b, NSA_KV_HEADS),
        in_specs=[wide(EV_NQ), col(EV_F32_KC), col(EV_F32_VC), col(EV_F32_KS), col(EV_F32_VS), col(EV_KW), col(EV_VW),
                  wide(EV_NZ), col(EV_NG),
                  full((1, HEAD_DIM)), full((NSA_N_BRANCH, HEAD_DIM)), full((CMP_LEN, HEAD_DIM)), full((CMP_LEN, HEAD_DIM)),
                  full((flat, HEAD_DIM)), full((HEAD_DIM, HEAD_DIM)), full((flat, HEAD_DIM)), full((HEAD_DIM, HEAD_DIM)),
                  full((LANE, LANE)), full((SEQ, LANE)), full((N_SEL, LANE))],
        out_specs=pl.BlockSpec((None, SEQ, grp_w), lambda bi, gi: (bi, 0, gi)),
        out_shape=jax.ShapeDtypeStruct((b, SEQ, NSA_HEADS * HEAD_DIM), bf16),
        scratch_shapes=[pltpu.VMEM((SEQ // CMP_STRIDE, HEAD_DIM), bf16), pltpu.VMEM((SEQ // CMP_STRIDE, HEAD_DIM), bf16),
                        pltpu.VMEM((SEQ, 2 * HEAD_DIM), bf16), pltpu.VMEM((SEQ, 2 * HEAD_DIM), bf16),
                        pltpu.VMEM((SEQ + NSA_WINDOW, 2 * HEAD_DIM), bf16), pltpu.VMEM((SEQ + NSA_WINDOW, 2 * HEAD_DIM), bf16),
                        pltpu.VMEM((NSA_REP, SEQ, HEAD_DIM), bf16), pltpu.VMEM((SEQ, LANE), bf16)],
        compiler_params=_cparams("parallel", "parallel"),
        name="nsa",
    )(h, hf, hf, hf, hf, h, h, h, h,
      q_g.reshape(1, HEAD_DIM), k_g, pos_k, pos_v,
      k_w1.astype(bf16), k_w2.astype(bf16), v_w1.astype(bf16), v_w2.astype(bf16), ovl_t, gsel, eye)


def _conv_body(u_ref, g_ref, z_ref, w_ref, b_ref, lg_ref, lb_ref, o_ref, a_ext, a_sh, y_s):
    ext = CONV_HALO + CONV_T
    n_cb = CONV_W // LANE
    lead = CONV_HALO - (CONV_K - 1)

    @pl.when(pl.program_id(1) == 0)
    def _():
        for cb in range(n_cb):
            a_ext[cb, 0:CONV_HALO, :] = jnp.zeros((CONV_HALO, LANE), f32)
            a_ext[cb, ext:ext + SUBLANE, :] = jnp.zeros((SUBLANE, LANE), f32)

    for cb in range(n_cb):
        lanes = slice(cb * LANE, (cb + 1) * LANE)
        for c in range(CONV_T // 128):
            rows = slice(c * 128, (c + 1) * 128)
            a_ext[cb, CONV_HALO + c * 128:CONV_HALO + (c + 1) * 128, :] = u_ref[rows, lanes] * jax.nn.sigmoid(g_ref[rows, lanes])
        for r in range(1, SUBLANE):
            for c in range(ext // CONV_HALO):
                a_sh[r - 1, cb, c * CONV_HALO:(c + 1) * CONV_HALO, :] = a_ext[cb, c * CONV_HALO + r:(c + 1) * CONV_HALO + r, :]

        taps_w = [jnp.broadcast_to(w_ref[k:k + 1, lanes], (CONV_RB, LANE)) for k in range(CONV_K)]
        bias = jnp.broadcast_to(b_ref[:, lanes], (CONV_RB, LANE))

        def conv_rows(rb, carry, cb=cb, taps_w=taps_w, bias=bias):
            r0 = pl.multiple_of(rb * CONV_RB, CONV_RB)
            parts = [bias, None]
            for k in range(CONV_K):
                shift, base = (lead + k) % SUBLANE, (lead + k) // SUBLANE * SUBLANE
                rows = pl.ds(r0 + base, CONV_RB)
                tap = a_ext[cb, rows, :] if shift == 0 else a_sh[shift - 1, cb, rows, :]
                term = tap * taps_w[k]
                parts[k % 2] = term if parts[k % 2] is None else parts[k % 2] + term
            y_s[cb, pl.ds(r0, CONV_RB), :] = parts[0] + parts[1]
            return carry

        lax.fori_loop(0, CONV_T // CONV_RB, conv_rows, 0)
        a_ext[cb, 0:CONV_HALO, :] = a_ext[cb, CONV_T:CONV_T + CONV_HALO, :]

    def norm_rows(rb, carry):
        rows = pl.ds(pl.multiple_of(rb * CONV_RB, CONV_RB), CONV_RB)
        y = jnp.concatenate([y_s[cb, rows, :] for cb in range(n_cb)], axis=1)
        mu = jnp.mean(y, axis=-1, keepdims=True)
        var = jnp.mean(jnp.square(y - mu), axis=-1, keepdims=True)
        yn = (y - mu) * lax.rsqrt(var + NORM_EPS) * lg_ref[...] + lb_ref[...]
        o_ref[rows, :] = (_silu(yn) * _silu(z_ref[rows, :])).astype(o_ref.dtype)
        return carry

    lax.fori_loop(0, CONV_T // CONV_RB, norm_rows, 0, unroll=4)


def _conv(h, dw_w, dw_b, ln_g, ln_b):
    b = h.shape[0]

    def col(off):
        return pl.BlockSpec((None, CONV_T, CONV_W), lambda bi, ti: (bi, ti, off * LANE // CONV_W))

    def full(rows):
        return pl.BlockSpec((rows, CONV_W), lambda bi, ti: (0, 0))

    return pl.pallas_call(
        _conv_body,
        grid=(b, SEQ // CONV_T),
        in_specs=[col(OD_CU), col(OD_CG), col(OD_CZ), full(CONV_K), full(1), full(1), full(1)],
        out_specs=pl.BlockSpec((None, CONV_T, CONV_W), lambda bi, ti: (bi, ti, 0)),
        out_shape=jax.ShapeDtypeStruct((b, SEQ, CONV_W), bf16),
        scratch_shapes=[pltpu.VMEM((CONV_W // LANE, CONV_HALO + CONV_T + SUBLANE, LANE), f32),
                        pltpu.VMEM((SUBLANE - 1, CONV_W // LANE, CONV_HALO + CONV_T, LANE), f32),
                        pltpu.VMEM((CONV_W // LANE, CONV_T, LANE), f32)],
        compiler_params=_cparams("parallel", "arbitrary"),
        name="conformer_conv",
    )(h, h, h, dw_w, dw_b.reshape(1, CONV_W), ln_g.reshape(1, CONV_W), ln_b.reshape(1, CONV_W))


def _dil_body(q0_ref, k0_ref, v0_ref, q1_ref, k1_ref, v1_ref, q2_ref, k2_ref, v2_ref, z_ref, qg_ref, kg_ref,
              o_ref, qn_s, kn_s, va_s, o0_s, o1_s, o2_s, l0_s, l1_s, l2_s):
    blk = DIL_BLOCK
    n_tiles = SEQ // blk
    groups = ((q0_ref, k0_ref, v0_ref, o0_s, l0_s), (q1_ref, k1_ref, v1_ref, o1_s, l1_s), (q2_ref, k2_ref, v2_ref, o2_s, l2_s))
    kn_s[0:blk, :] = jnp.zeros((blk, HEAD_DIM), bf16)
    va_s[0:blk, :] = jnp.zeros((blk, 2 * HEAD_DIM), bf16)
    ones = jnp.ones((blk, HEAD_DIM), bf16)
    row2 = lax.broadcasted_iota(jnp.int32, (blk, 2 * blk), 0)
    col2 = lax.broadcasted_iota(jnp.int32, (blk, 2 * blk), 1)
    band_prev = jnp.where((col2 >= row2) & (col2 <= row2 + blk), 0.0, NEG_INF)
    band_first = jnp.where((col2 >= blk) & (col2 <= row2 + blk), 0.0, NEG_INF)

    for (q_ref, k_ref, v_ref, o_s, l_s), (window, dil) in zip(groups, DIL_PATTERNS):
        assert window // dil == blk
        nb = SEQ // dil // blk

        def natural_rows(u, dil=dil, nb=nb):
            start = u // nb + (u % nb) * (blk * dil)
            return pl.ds(start, blk, stride=dil) if dil > 1 else pl.ds(pl.multiple_of(start, blk), blk)

        def prep(u, carry, q_ref=q_ref, k_ref=k_ref, v_ref=v_ref, natural_rows=natural_rows):
            src = natural_rows(u)
            qn_s[pl.ds(pl.multiple_of(u * blk, blk), blk), :] = _rms(q_ref[src, :], qg_ref[...]).astype(bf16)
            dst = pl.ds(pl.multiple_of((u + 1) * blk, blk), blk)
            kn_s[dst, :] = _rms(k_ref[src, :], kg_ref[...]).astype(bf16)
            va_s[dst, :] = jnp.concatenate([v_ref[src, :].astype(bf16), ones], axis=1)
            return carry

        lax.fori_loop(0, n_tiles, prep, 0, unroll=DIL_UNROLL)

        def att(u, carry, o_s=o_s, l_s=l_s, nb=nb, natural_rows=natural_rows):
            q = qn_s[pl.ds(pl.multiple_of(u * blk, blk), blk), :]
            keys = pl.ds(pl.multiple_of(u * blk, blk), 2 * blk)
            band = jnp.where((u % nb) > 0, band_prev, band_first)
            s = _dot_nt(q, kn_s[keys, :]) * SCALE + band
            m = jnp.max(s, axis=-1, keepdims=True)
            pv = _dot(jnp.exp(s - m).astype(bf16), va_s[keys, :])
            den = pv[:, HEAD_DIM:]
            dst = natural_rows(u)
            o_s[dst, :] = pv[:, :HEAD_DIM] / den
            l_s[dst, :] = m + jnp.log(den)
            return carry

        lax.fori_loop(0, n_tiles, att, 0, unroll=DIL_UNROLL)

    def mix(c, carry):
        rows = pl.ds(pl.multiple_of(c * blk, blk), blk)
        l0, l1, l2 = l0_s[rows, :], l1_s[rows, :], l2_s[rows, :]
        m = jnp.maximum(jnp.maximum(l0, l1), l2)
        w0, w1, w2 = jnp.exp(l0 - m), jnp.exp(l1 - m), jnp.exp(l2 - m)
        o = (w0 * o0_s[rows, :] + w1 * o1_s[rows, :] + w2 * o2_s[rows, :]) / (w0 + w1 + w2)
        o_ref[rows, :] = (o * _silu(z_ref[rows, :])).astype(o_ref.dtype)
        return carry

    lax.fori_loop(0, n_tiles, mix, 0, unroll=DIL_UNROLL)


def _dilated(h, q_g, k_g):
    b = h.shape[0]

    def col(off):
        return pl.BlockSpec((None, SEQ, HEAD_DIM), lambda bi, hi: (bi, 0, off + hi))

    gain = pl.BlockSpec((1, HEAD_DIM), lambda bi, hi: (0, 0))
    specs = []
    for gi in range(len(DIL_PATTERNS)):
        specs += [col(OD_DQ + gi * DIL_GROUP_HEADS), col(OD_DK + gi * DIL_GROUP_HEADS), col(OD_DV + gi * DIL_GROUP_HEADS)]
    return pl.pallas_call(
        _dil_body,
        grid=(b, DIL_GROUP_HEADS),
        in_specs=specs + [col(OD_DZ), gain, gain],
        out_specs=pl.BlockSpec((None, SEQ, HEAD_DIM), lambda bi, hi: (bi, 0, hi)),
        out_shape=jax.ShapeDtypeStruct((b, SEQ, DIL_GROUP_HEADS * HEAD_DIM), bf16),
        scratch_shapes=[pltpu.VMEM((SEQ, HEAD_DIM), bf16), pltpu.VMEM((SEQ + DIL_BLOCK, HEAD_DIM), bf16),
                        pltpu.VMEM((SEQ + DIL_BLOCK, 2 * HEAD_DIM), bf16)] + [pltpu.VMEM((SEQ, HEAD_DIM), f32)] * 6,
        compiler_params=_cparams("parallel", "parallel"),
        name="dilated",
    )(*([h] * 10), q_g.reshape(1, HEAD_DIM), k_g.reshape(1, HEAD_DIM))


def _even_w_in(w):
    ng0 = EV_W_NZ * LANE
    nz0 = ng0 + NSA_N_BRANCH * NSA_HEADS
    ng = w[:, ng0:nz0].reshape(-1, NSA_N_BRANCH, NSA_KV_HEADS, NSA_REP)
    ng = ng.transpose(0, 2, 1, 3).reshape(-1, NSA_KV_HEADS, NSA_N_BRANCH * NSA_REP)
    ng = jnp.pad(ng, ((0, 0), (0, 0), (0, LANE - NSA_N_BRANCH * NSA_REP))).reshape(-1, NSA_KV_HEADS * LANE)
    out = jnp.concatenate([w[:, :ng0], w[:, nz0:], ng], axis=1)
    return jnp.pad(out, ((0, 0), (0, EV_COLS - out.shape[1]))).astype(bf16)


def kernel(x, ev_norm, ev_w_in, ev_w_out, ev_ret_norm, ev_nsa_q_norm, ev_nsa_k_norm, ev_cmp_pos_k, ev_cmp_pos_v, ev_cmp_k_w1, ev_cmp_k_w2, ev_cmp_v_w1, ev_cmp_v_w2, od_norm, od_w_in, od_w_out, od_dw_w, od_dw_b, od_conv_norm_g, od_conv_norm_b, od_dil_q_norm, od_dil_k_norm):
    b, s, d = x.shape
    assert (s, d) == (SEQ, D_MODEL)
    x0 = x.reshape(b * s, d)

    h0, h0f = _proj_in(x0, ev_norm[0], _even_w_in(ev_w_in[0]), EV_F32_TILE, 1)
    h0, h0f = h0.reshape(b, s, -1), h0f.reshape(b, s, -1)
    a_out = _retention(h0, ev_ret_norm[0])
    b_out = _nsa(h0, h0f, ev_nsa_q_norm[0], ev_nsa_k_norm[0], ev_cmp_pos_k[0], ev_cmp_pos_v[0],
                 ev_cmp_k_w1[0], ev_cmp_k_w2[0], ev_cmp_v_w1[0], ev_cmp_v_w2[0])
    x1 = _proj_out(x0, a_out.reshape(b * s, -1), b_out.reshape(b * s, -1), ev_w_out[0].astype(bf16))

    _, h1 = _proj_in(x1, od_norm[0], od_w_in[0].astype(bf16), 0, OD_COLS // PROJ_TN)
    h1 = h1.reshape(b, s, -1)
    c_out = _conv(h1, od_dw_w[0], od_dw_b[0], od_conv_norm_g[0], od_conv_norm_b[0])
    d_out = _dilated(h1, od_dil_q_norm[0], od_dil_k_norm[0])
    x2 = _proj_out(x1, c_out.reshape(b * s, -1), d_out.reshape(b * s, -1), od_w_out[0].astype(bf16))
    return x2.reshape(b, s, d)
```

```python
import functools

import jax
import jax.numpy as jnp
import numpy as np
from jax import lax
from jax.experimental import pallas as pl
from jax.experimental.pallas import tpu as pltpu

f32 = jnp.float32
bf16 = jnp.bfloat16

D_MODEL = 2048
SEQ = 2048
HEAD_DIM = 128
SCALE = HEAD_DIM ** -0.5
NORM_EPS = 1e-6
NEG_INF = -1e30
LANE = 128
SUBLANE = 8

RET_HEADS = 8
RET_W = RET_HEADS * HEAD_DIM
RET_CHUNK = 256
RET_UNROLL = 8
ROPE_BASE = 10000.0
NSA_HEADS = 8
NSA_KV_HEADS = 2
NSA_REP = NSA_HEADS // NSA_KV_HEADS
NSA_N_BRANCH = 3
CMP_LEN = 32
CMP_STRIDE = 16
N_CMP = (SEQ - CMP_LEN) // CMP_STRIDE + 1
SEL_BLOCK = 64
N_SEL = SEQ // SEL_BLOCK
SEL_TOPK = 16
NSA_WINDOW = 512
NSA_TQ = 128
SEL_TK = 512
NSA_PREP_ROWS = 256
NSA_SELECT_TQ = 1024
WIN_SPAN = NSA_WINDOW + NSA_TQ
NSA_TILE_UNROLL = 2
CONV_W = 1024
CONV_K = 31
CONV_T = 512
CONV_HALO = 32
CONV_RB = 64
DIL_PATTERNS = ((128, 1), (512, 4), (2048, 16))
DIL_GROUP_HEADS = 4
DIL_HEADS = DIL_GROUP_HEADS * len(DIL_PATTERNS)
DIL_BLOCK = 128
DIL_UNROLL = 16

EV_RQ, EV_RK, EV_RV, EV_RZ = 0, 8, 16, 24
EV_NQ = 32
EV_W_NZ = 52
EV_COLS = 8192
EV_TAIL_TILES = 2
EV_F32_TILE = 5
EV_F32_KC, EV_F32_VC, EV_F32_KS, EV_F32_VS = 0, 2, 4, 6
EV_KW, EV_VW = 40, 42
EV_NZ = 44
EV_NG = 52
OD_CU, OD_CG, OD_CZ = 0, 8, 16
OD_DQ, OD_DK, OD_DV = 24, 36, 48
OD_DZ = 60
OD_COLS = 8192

VMEM_LIMIT = 56 * 1024 * 1024


def _cparams(*sem):
    return pltpu.CompilerParams(dimension_semantics=sem, vmem_limit_bytes=VMEM_LIMIT)


def _rms(x, gain):
    ms = jnp.mean(x * x, axis=-1, keepdims=True)
    return x * lax.rsqrt(ms + NORM_EPS) * gain


def _silu(z):
    return z * jax.nn.sigmoid(z)


def _dot(a, b):
    return jnp.dot(a, b, preferred_element_type=f32)


def _dot_nt(a, b):
    return lax.dot_general(a, b, (((1,), (1,)), ((), ())), preferred_element_type=f32)


def _dot_tn(a, b):
    return lax.dot_general(a, b, (((0,), (0,)), ((), ())), preferred_element_type=f32)


PROJ_TM = 1024
PROJ_TN = 1024
NORM_ROWS = 256


def _proj_in_body(x_ref, g_ref, w_ref, wt_ref, *refs, f32_tile, f32_tiles, n_tiles, main_tiles):
    xn_ref = refs[-1]
    j = pl.program_id(1)

    @pl.when(j == 0)
    def _():
        for c in range(PROJ_TM // NORM_ROWS):
            rows = slice(c * NORM_ROWS, (c + 1) * NORM_ROWS)
            xn_ref[rows, :] = _rms(x_ref[rows, :], g_ref[...]).astype(bf16)

    def emit(out_ref, weight_ref):
        out_ref[...] = _dot(xn_ref[...], weight_ref[...]).astype(out_ref.dtype)

    if f32_tiles == n_tiles:
        assert main_tiles == n_tiles
        emit(refs[0], w_ref)
    else:
        assert f32_tile + f32_tiles <= main_tiles
        in_f32 = (j >= f32_tile) & (j < f32_tile + f32_tiles)
        in_tail = j >= main_tiles
        pl.when(in_f32)(functools.partial(emit, refs[1], w_ref))
        pl.when(jnp.logical_not(in_f32) & jnp.logical_not(in_tail))(functools.partial(emit, refs[0], w_ref))
        pl.when(in_tail)(functools.partial(emit, refs[0], wt_ref))


def _proj_in(x2, gain, w, w_tail, main_tiles, f32_tile, f32_tiles):
    m, d = x2.shape
    assert main_tiles * PROJ_TN <= w.shape[1]
    if w_tail is None:
        w_tail = w
    else:
        assert w_tail.shape[1] % PROJ_TN == 0
    n_tiles = main_tiles + (0 if w_tail is w else w_tail.shape[1] // PROJ_TN)
    n = n_tiles * PROJ_TN
    tail_tiles = max(n_tiles - main_tiles, 1)
    all_f32 = f32_tiles == n_tiles
    assert all_f32 or f32_tile >= 1
    bf16_spec = pl.BlockSpec((PROJ_TM, PROJ_TN), lambda i, j: (i, j - jnp.clip(j - f32_tile + 1, 0, f32_tiles)))
    f32_spec = pl.BlockSpec((PROJ_TM, PROJ_TN), lambda i, j: (i, jnp.clip(j - f32_tile, 0, f32_tiles - 1)))
    bf16_shape = jax.ShapeDtypeStruct((m, n - f32_tiles * PROJ_TN), bf16)
    f32_shape = jax.ShapeDtypeStruct((m, f32_tiles * PROJ_TN), f32)
    out = pl.pallas_call(
        functools.partial(_proj_in_body, f32_tile=f32_tile, f32_tiles=f32_tiles, n_tiles=n_tiles, main_tiles=main_tiles),
        grid=(m // PROJ_TM, n_tiles),
        in_specs=[
            pl.BlockSpec((PROJ_TM, d), lambda i, j: (i, 0)),
            pl.BlockSpec((1, d), lambda i, j: (0, 0)),
            pl.BlockSpec((d, PROJ_TN), lambda i, j: (0, jnp.minimum(j, main_tiles - 1))),
            pl.BlockSpec((d, PROJ_TN), lambda i, j: (0, jnp.clip(j - main_tiles, 0, tail_tiles - 1))),
        ],
        out_specs=[f32_spec] if all_f32 else [bf16_spec, f32_spec],
        out_shape=[f32_shape] if all_f32 else [bf16_shape, f32_shape],
        scratch_shapes=[pltpu.VMEM((PROJ_TM, d), bf16)],
        compiler_params=_cparams("parallel", "arbitrary"),
        name="proj_in",
    )(x2, gain.reshape(1, d), w, w_tail)
    return (None, out[0]) if all_f32 else tuple(out)


OUT_TM = 512


def _proj_out_body(x_ref, m0_ref, m1_ref, w0_ref, w1_ref, o_ref):
    o_ref[...] = x_ref[...] + _dot(m0_ref[...], w0_ref[...]) + _dot(m1_ref[...], w1_ref[...])


def _proj_out(x2, m0, m1, w):
    m, d = x2.shape
    k0, k1 = m0.shape[1], m1.shape[1]
    assert w.shape == (k0 + k1, d) and k0 % k1 == 0
    return pl.pallas_call(
        _proj_out_body,
        grid=(m // OUT_TM,),
        in_specs=[
            pl.BlockSpec((OUT_TM, d), lambda i: (i, 0)),
            pl.BlockSpec((OUT_TM, k0), lambda i: (i, 0)),
            pl.BlockSpec((OUT_TM, k1), lambda i: (i, 0)),
            pl.BlockSpec((k0, d), lambda i: (0, 0)),
            pl.BlockSpec((k1, d), lambda i: (k0 // k1, 0)),
        ],
        out_specs=pl.BlockSpec((OUT_TM, d), lambda i: (i, 0)),
        out_shape=jax.ShapeDtypeStruct((m, d), f32),
        compiler_params=_cparams("parallel"),
        name="proj_out",
    )(x2, m0, m1, w, w)


def _ret_body(q_ref, k_ref, v_ref, z_ref, cos_ref, sin_ref, dmask_ref, kdec_ref, qdec_ref, cdec_ref, g_ref, o_ref):
    c_len = RET_CHUNK

    def rot(t, cs, sn):
        return t * cs + pltpu.roll(t, HEAD_DIM // 2, 1) * sn

    def body(n, state):
        rows = pl.ds(pl.multiple_of(n * c_len, c_len), c_len)
        cs = cos_ref[rows, :]
        sn = sin_ref[rows, :]
        q = rot(q_ref[rows, :].astype(f32), cs, sn)
        k = rot(k_ref[rows, :].astype(f32), cs, sn) * SCALE
        vb = v_ref[rows, :]
        s = _dot_nt(q.astype(bf16), k.astype(bf16)) * dmask_ref[...]
        o = _dot(s.astype(bf16), vb)
        o = o + _dot((q * qdec_ref[...]).astype(bf16), state.astype(bf16))
        kv = _dot_tn((k * kdec_ref[...]).astype(bf16), vb)
        state = cdec_ref[...] * state + kv
        y = _rms(o, g_ref[...])
        o_ref[rows, :] = (y * _silu(z_ref[rows, :].astype(f32))).astype(o_ref.dtype)
        return state

    lax.fori_loop(0, SEQ // c_len, body, jnp.zeros((HEAD_DIM, HEAD_DIM), f32), unroll=RET_UNROLL)


def _retention(h, ret_g):
    b = h.shape[0]
    c_len = RET_CHUNK
    half = HEAD_DIM // 2
    pos = jnp.arange(SEQ)
    inv = ROPE_BASE ** (-jnp.arange(half, dtype=f32) / half)
    ang = pos.astype(f32)[:, None] * inv[None, :]
    cos, sin = jnp.cos(ang), jnp.sin(ang)
    cos_t = jnp.concatenate([cos, cos], axis=-1)
    sin_t = jnp.concatenate([-sin, sin], axis=-1)
    log_g = jnp.log(1.0 - 2.0 ** (-5.0 - jnp.arange(RET_HEADS, dtype=f32)))
    idx = jnp.arange(c_len, dtype=f32)
    diff = idx[:, None] - idx[None, :]
    dmask = jnp.where(diff >= 0, jnp.exp(jnp.maximum(diff, 0.0)[None] * log_g[:, None, None]), 0.0)
    ones = jnp.ones((1, 1, HEAD_DIM), f32)
    k_dec = jnp.exp((c_len - 1 - idx)[None, :] * log_g[:, None])[:, :, None] * ones
    q_dec = jnp.exp((idx + 1.0)[None, :] * log_g[:, None])[:, :, None] * ones
    chunk_dec = jnp.exp(c_len * log_g)[:, None, None] * ones

    def col(off):
        return pl.BlockSpec((None, SEQ, HEAD_DIM), lambda bi, hi: (bi, 0, off + hi))

    table = pl.BlockSpec((SEQ, HEAD_DIM), lambda bi, hi: (0, 0))

    def per_head(rows, cols=HEAD_DIM):
        return pl.BlockSpec((None, rows, cols), lambda bi, hi: (hi, 0, 0))

    return pl.pallas_call(
        _ret_body,
        grid=(b, RET_HEADS),
        in_specs=[col(EV_RQ), col(EV_RK), col(EV_RV), col(EV_RZ), table, table,
                  per_head(c_len, c_len), per_head(c_len), per_head(c_len), per_head(1), per_head(1)],
        out_specs=pl.BlockSpec((None, SEQ, HEAD_DIM), lambda bi, hi: (bi, 0, hi)),
        out_shape=jax.ShapeDtypeStruct((b, SEQ, RET_W), bf16),
        compiler_params=_cparams("parallel", "parallel"),
        name="retention",
    )(h, h, h, h, cos_t, sin_t, dmask, k_dec, q_dec, chunk_dec, ret_g.reshape(RET_HEADS, 1, HEAD_DIM))


def _nsa_body(q_ref, kc_ref, vc_ref, ks_ref, vs_ref, kw_ref, vw_ref, nz_ref, ng_ref,
              qg_ref, kg_ref, posk_ref, posv_ref, w1k_ref, w2k_ref, w1v_ref, w2v_ref, ovl_ref, gsel_ref, eye_ref,
              o_ref, kcmp_s, vcmp_s, ksa_s, vsa_s, kwa_s, vwa_s, qn_s, not_sel_s):
    tq = NSA_TQ
    sq = NSA_SELECT_TQ
    rep = NSA_REP
    half_blk = CMP_LEN // 2
    d = HEAD_DIM
    exp2_scale = SCALE * float(np.log2(np.e))

    def compress(t_ref, pos_ref, w1_ref, w2_ref):
        first = jnp.zeros((SEQ // CMP_STRIDE, d), f32)
        second = jnp.zeros((SEQ // CMP_STRIDE, d), f32)
        for i in range(half_blk):
            ti = t_ref[pl.ds(i, SEQ // CMP_STRIDE, stride=CMP_STRIDE), :]
            first = first + _dot((ti + pos_ref[i:i + 1, :]).astype(bf16), w1_ref[i * d:(i + 1) * d, :])
            j = half_blk + i
            second = second + _dot((ti + pos_ref[j:j + 1, :]).astype(bf16), w1_ref[j * d:(j + 1) * d, :])
        pre = first + pltpu.roll(second, SEQ // CMP_STRIDE - 1, 0)
        return _dot(jax.nn.gelu(pre).astype(bf16), w2_ref[...])

    kcmp_s[...] = _rms(compress(kc_ref, posk_ref, w1k_ref, w2k_ref), kg_ref[0:1, :]).astype(bf16)
    vcmp_s[...] = compress(vc_ref, posv_ref, w1v_ref, w2v_ref).astype(bf16)
    lane0 = lax.broadcasted_iota(jnp.int32, (tq, d), 1) == 0
    pc_rows = NSA_PREP_ROWS
    ones = jnp.ones((pc_rows, d), bf16)
    zeros = jnp.zeros((pc_rows, d), bf16)
    for c in range(NSA_WINDOW // pc_rows):
        rows = slice(c * pc_rows, (c + 1) * pc_rows)
        kwa_s[rows, 0:d] = zeros
        kwa_s[rows, d:2 * d] = jnp.where(lax.broadcasted_iota(jnp.int32, (pc_rows, d), 1) == 0, NEG_INF, 0.0).astype(bf16)
        vwa_s[rows, 0:d] = zeros
        vwa_s[rows, d:2 * d] = zeros
    for c in range(SEQ // pc_rows):
        rows = slice(c * pc_rows, (c + 1) * pc_rows)
        wrows = slice(NSA_WINDOW + c * pc_rows, NSA_WINDOW + (c + 1) * pc_rows)
        ksa_s[rows, 0:d] = _rms(ks_ref[rows, :], kg_ref[1:2, :]).astype(bf16)
        ksa_s[rows, d:2 * d] = gsel_ref[rows, :]
        vsa_s[rows, 0:d] = vs_ref[rows, :].astype(bf16)
        vsa_s[rows, d:2 * d] = ones
        kwa_s[wrows, 0:d] = _rms(kw_ref[rows, :].astype(f32), kg_ref[2:3, :]).astype(bf16)
        kwa_s[wrows, d:2 * d] = zeros
        vwa_s[wrows, 0:d] = vw_ref[rows, :]
        vwa_s[wrows, d:2 * d] = ones

    row_i = lax.broadcasted_iota(jnp.int32, (tq, tq), 0)
    col_i = lax.broadcasted_iota(jnp.int32, (tq, tq), 1)
    band_first = jnp.where(col_i > row_i, 0.0, NEG_INF)
    band_last = jnp.where(col_i <= row_i, 0.0, NEG_INF)
    pad_flag = jnp.where(lane0, 1.0, 0.0).astype(bf16)

    def softmax_update(s, m, acc, v):
        n = s.shape[-1]
        m_new = jnp.maximum(m, jnp.max(s, axis=-1, keepdims=True))
        alpha = jnp.exp2((m - m_new) * exp2_scale)
        e = jnp.exp2((s - m_new) * exp2_scale)
        pv = _dot(e.reshape(rep * tq, n).astype(bf16), v).reshape(rep, tq, 2 * d)
        return m_new, alpha * acc + pv

    def select_tile(i, carry):
        t0 = i * sq
        rows = pl.ds(pl.multiple_of(t0, sq), sq)
        qs = [_rms(q_ref[rows, r * d:(r + 1) * d].astype(f32), qg_ref[...]).astype(bf16) for r in range(rep)]
        for r in range(rep):
            qn_s[r, rows, :] = qs[r]
        tpos = t0 + lax.broadcasted_iota(jnp.int32, (sq, 1), 0)
        qst = jnp.concatenate(qs, axis=0)
        n_idx = lax.broadcasted_iota(jnp.int32, (1, LANE), 1)
        valid = (n_idx * CMP_STRIDE + (CMP_LEN - 1) <= tpos) & (n_idx < N_CMP)
        cbias = jnp.where(valid, 0.0, NEG_INF)
        sc = (_dot_nt(qst, kcmp_s[...]) * SCALE).reshape(rep, sq, LANE) + cbias[None]
        mc = jnp.max(sc, axis=-1, keepdims=True)
        ec = jnp.exp(sc - mc)
        any_valid = jnp.where(tpos >= CMP_LEN - 1, 1.0, 0.0)
        pc = ec / jnp.sum(ec, axis=-1, keepdims=True) * any_valid[None]
        o_cmp = _dot(pc.reshape(rep * sq, LANE).astype(bf16), vcmp_s[...]).reshape(rep, sq, d)
        gates = jax.nn.sigmoid(ng_ref[rows, :].astype(f32))
        for r in range(rep):
            o_ref[rows, r * d:(r + 1) * d] = (gates[:, r:r + 1] * o_cmp[r]).astype(o_ref.dtype)
        psum = pc[0]
        for r in range(1, rep):
            psum = psum + pc[r]

        p_hi = psum.astype(bf16)
        rest = psum - p_hi.astype(f32)
        p_mid = rest.astype(bf16)
        p_lo = (rest - p_mid.astype(f32)).astype(bf16)
        imp = (_dot_nt(ovl_ref[...], p_hi) + _dot_nt(ovl_ref[...], p_mid) + _dot_nt(ovl_ref[...], p_lo))[:N_SEL, :]
        jj = lax.broadcasted_iota(jnp.int32, (N_SEL, 1), 0)
        cur = (t0 + lax.broadcasted_iota(jnp.int32, (1, sq), 1)) // SEL_BLOCK
        forced = (jj == 0) | (jj == cur) | (jj == cur - 1)
        imp = jnp.where(forced, -NEG_INF, jnp.where(jj > cur, NEG_INF, imp))
        rank = jnp.zeros((N_SEL, sq), f32)
        for j2 in range(N_SEL):
            row = imp[j2:j2 + 1, :]
            beats = (row > imp) | ((row == imp) & (j2 < jj))
            rank = rank + jnp.where(beats, 1.0, 0.0)
        not_sel_t = jnp.where(rank < SEL_TOPK, 0.0, 1.0).astype(bf16)
        not_sel_s[rows, :] = _dot_tn(not_sel_t, eye_ref[...]).astype(bf16)
        return carry

    lax.fori_loop(0, SEQ // sq, select_tile, 0)

    def make_tile(n_full):
        def tile(i, carry):
            t0 = i * tq
            rows = pl.ds(pl.multiple_of(t0, tq), tq)
            qs = [qn_s[r, rows, :] for r in range(rep)]
            tpos = t0 + lax.broadcasted_iota(jnp.int32, (tq, 1), 0)
            m_init = jnp.full((rep, tq, 1), NEG_INF, f32)
            acc_init = jnp.zeros((rep, tq, 2 * d), f32)

            qw = jnp.concatenate([jnp.concatenate([q, pad_flag], axis=1) for q in qs], axis=0)
            wrows = pl.ds(pl.multiple_of(t0, tq), WIN_SPAN)
            sw = _dot_nt(qw, kwa_s[wrows, :]).reshape(rep, tq, WIN_SPAN)
            sw = jnp.concatenate([sw[:, :, :tq] + band_first[None], sw[:, :, tq:NSA_WINDOW],
                                  sw[:, :, NSA_WINDOW:] + band_last[None]], axis=2)
            _, acc_w = softmax_update(sw, m_init, acc_init, vwa_s[wrows, :])
            o_win = acc_w[:, :, :d] / acc_w[:, :, d:]

            not_sel = not_sel_s[rows, :]
            qsel = jnp.concatenate([jnp.concatenate([q, not_sel], axis=1) for q in qs], axis=0)
            m_s, acc_s = m_init, acc_init
            for kt in range(n_full + 1):
                krows = slice(kt * SEL_TK, (kt + 1) * SEL_TK)
                s = _dot_nt(qsel, ksa_s[krows, :]).reshape(rep, tq, SEL_TK)
                if kt == n_full:
                    kpos = kt * SEL_TK + lax.broadcasted_iota(jnp.int32, (1, SEL_TK), 1)
                    s = s + jnp.where(kpos <= tpos, 0.0, NEG_INF)[None]
                m_s, acc_s = softmax_update(s, m_s, acc_s, vsa_s[krows, :])
            o_sel = acc_s[:, :, :d] / acc_s[:, :, d:]

            gates = jax.nn.sigmoid(ng_ref[rows, :].astype(f32))
            for r in range(rep):
                g_s = gates[:, rep + r:rep + r + 1]
                g_w = gates[:, 2 * rep + r:2 * rep + r + 1]
                cols = slice(r * d, (r + 1) * d)
                o = o_ref[rows, cols].astype(f32) + g_s * o_sel[r] + g_w * o_win[r]
                o_ref[rows, cols] = (o * _silu(nz_ref[rows, cols].astype(f32))).astype(o_ref.dtype)
            return carry
        return tile

    tiles_per_key_tile = SEL_TK // tq
    for n_full in range(SEQ // SEL_TK):
        lax.fori_loop(n_full * tiles_per_key_tile, (n_full + 1) * tiles_per_key_tile, make_tile(n_full), 0,
                      unroll=NSA_TILE_UNROLL)


def _nsa(h, hf, q_g, k_g, pos_k, pos_v, k_w1, k_w2, v_w1, v_w2):
    b = h.shape[0]
    grp_w = NSA_REP * HEAD_DIM
    cs = np.arange(LANE) * CMP_STRIDE
    ss = np.arange(LANE) * SEL_BLOCK
    ovl = ((cs[None, :] < ss[:, None] + SEL_BLOCK) & (cs[None, :] + CMP_LEN > ss[:, None])
           & (np.arange(LANE)[None, :] < N_CMP) & (np.arange(LANE)[:, None] < N_SEL))
    ovl_t = jnp.asarray(ovl, dtype=bf16)
    key_blk = np.arange(SEQ) // SEL_BLOCK
    gsel = jnp.asarray(np.where(key_blk[:, None] == np.arange(LANE)[None, :], NEG_INF, 0.0), dtype=bf16)
    eye = jnp.asarray(np.eye(N_SEL, LANE), dtype=bf16)

    def wide(off):
        return pl.BlockSpec((None, SEQ, grp_w), lambda bi, gi: (bi, 0, off // NSA_REP + gi))

    def col(off):
        return pl.BlockSpec((None, SEQ, HEAD_DIM), lambda bi, gi: (bi, 0, off + gi))

    def full(shape):
        return pl.BlockSpec(shape, lambda bi, gi: (0,) * len(shape))

    flat = CMP_LEN * HEAD_DIM
    return pl.pallas_call(
        _nsa_body,
        grid=(b, NSA_KV_HEADS),
        in_specs=[wide(EV_NQ), col(EV_F32_KC), col(EV_F32_VC), col(EV_F32_KS), col(EV_F32_VS), col(EV_KW), col(EV_VW),
                  wide(EV_NZ), col(EV_NG),
                  full((1, HEAD_DIM)), full((NSA_N_BRANCH, HEAD_DIM)), full((CMP_LEN, HEAD_DIM)), full((CMP_LEN, HEAD_DIM)),
                  full((flat, HEAD_DIM)), full((HEAD_DIM, HEAD_DIM)), full((flat, HEAD_DIM)), full((HEAD_DIM, HEAD_DIM)),
                  full((LANE, LANE)), full((SEQ, LANE)), full((N_SEL, LANE))],
        out_specs=pl.BlockSpec((None, SEQ, grp_w), lambda bi, gi: (bi, 0, gi)),
        out_shape=jax.ShapeDtypeStruct((b, SEQ, NSA_HEADS * HEAD_DIM), bf16),
        scratch_shapes=[pltpu.VMEM((SEQ // CMP_STRIDE, HEAD_DIM), bf16), pltpu.VMEM((SEQ // CMP_STRIDE, HEAD_DIM), bf16),
                        pltpu.VMEM((SEQ, 2 * HEAD_DIM), bf16), pltpu.VMEM((SEQ, 2 * HEAD_DIM), bf16),
                        pltpu.VMEM((SEQ + NSA_WINDOW, 2 * HEAD_DIM), bf16), pltpu.VMEM((SEQ + NSA_WINDOW, 2 * HEAD_DIM), bf16),
                        pltpu.VMEM((NSA_REP, SEQ, HEAD_DIM), bf16), pltpu.VMEM((SEQ, LANE), bf16)],
        compiler_params=_cparams("parallel", "parallel"),
        name="nsa",
    )(h, hf, hf, hf, hf, h, h, h, h,
      q_g.reshape(1, HEAD_DIM), k_g, pos_k, pos_v,
      k_w1.astype(bf16), k_w2.astype(bf16), v_w1.astype(bf16), v_w2.astype(bf16), ovl_t, gsel, eye)


def _conv_body(u_ref, g_ref, z_ref, w_ref, b_ref, lg_ref, lb_ref, o_ref, a_ext, a_sh, y_s):
    ext = CONV_HALO + CONV_T
    n_cb = CONV_W // LANE
    lead = CONV_HALO - (CONV_K - 1)

    @pl.when(pl.program_id(1) == 0)
    def _():
        for cb in range(n_cb):
            a_ext[cb, 0:CONV_HALO, :] = jnp.zeros((CONV_HALO, LANE), f32)
            a_ext[cb, ext:ext + SUBLANE, :] = jnp.zeros((SUBLANE, LANE), f32)

    for cb in range(n_cb):
        lanes = slice(cb * LANE, (cb + 1) * LANE)
        for c in range(CONV_T // 128):
            rows = slice(c * 128, (c + 1) * 128)
            a_ext[cb, CONV_HALO + c * 128:CONV_HALO + (c + 1) * 128, :] = u_ref[rows, lanes] * jax.nn.sigmoid(g_ref[rows, lanes])
        for r in range(1, SUBLANE):
            for c in range(ext // CONV_HALO):
                a_sh[r - 1, cb, c * CONV_HALO:(c + 1) * CONV_HALO, :] = a_ext[cb, c * CONV_HALO + r:(c + 1) * CONV_HALO + r, :]

        taps_w = [jnp.broadcast_to(w_ref[k:k + 1, lanes], (CONV_RB, LANE)) for k in range(CONV_K)]
        bias = jnp.broadcast_to(b_ref[:, lanes], (CONV_RB, LANE))

        def conv_rows(rb, carry, cb=cb, taps_w=taps_w, bias=bias):
            r0 = pl.multiple_of(rb * CONV_RB, CONV_RB)
            parts = [bias, None]
            for k in range(CONV_K):
                shift, base = (lead + k) % SUBLANE, (lead + k) // SUBLANE * SUBLANE
                rows = pl.ds(r0 + base, CONV_RB)
                tap = a_ext[cb, rows, :] if shift == 0 else a_sh[shift - 1, cb, rows, :]
                term = tap * taps_w[k]
                parts[k % 2] = term if parts[k % 2] is None else parts[k % 2] + term
            y_s[cb, pl.ds(r0, CONV_RB), :] = parts[0] + parts[1]
            return carry

        lax.fori_loop(0, CONV_T // CONV_RB, conv_rows, 0, unroll=2)
        a_ext[cb, 0:CONV_HALO, :] = a_ext[cb, CONV_T:CONV_T + CONV_HALO, :]

    def norm_rows(rb, carry):
        rows = pl.ds(pl.multiple_of(rb * CONV_RB, CONV_RB), CONV_RB)
        y = jnp.concatenate([y_s[cb, rows, :] for cb in range(n_cb)], axis=1)
        mu = jnp.mean(y, axis=-1, keepdims=True)
        var = jnp.mean(jnp.square(y - mu), axis=-1, keepdims=True)
        yn = (y - mu) * lax.rsqrt(var + NORM_EPS) * lg_ref[...] + lb_ref[...]
        o_ref[rows, :] = (_silu(yn) * _silu(z_ref[rows, :])).astype(o_ref.dtype)
        return carry

    lax.fori_loop(0, CONV_T // CONV_RB, norm_rows, 0, unroll=4)


def _conv(h, dw_w, dw_b, ln_g, ln_b):
    b = h.shape[0]

    def col(off):
        return pl.BlockSpec((None, CONV_T, CONV_W), lambda bi, ti: (bi, ti, off * LANE // CONV_W))

    def full(rows):
        return pl.BlockSpec((rows, CONV_W), lambda bi, ti: (0, 0))

    return pl.pallas_call(
        _conv_body,
        grid=(b, SEQ // CONV_T),
        in_specs=[col(OD_CU), col(OD_CG), col(OD_CZ), full(CONV_K), full(1), full(1), full(1)],
        out_specs=pl.BlockSpec((None, CONV_T, CONV_W), lambda bi, ti: (bi, ti, 0)),
        out_shape=jax.ShapeDtypeStruct((b, SEQ, CONV_W), bf16),
        scratch_shapes=[pltpu.VMEM((CONV_W // LANE, CONV_HALO + CONV_T + SUBLANE, LANE), f32),
                        pltpu.VMEM((SUBLANE - 1, CONV_W // LANE, CONV_HALO + CONV_T, LANE), f32),
                        pltpu.VMEM((CONV_W // LANE, CONV_T, LANE), f32)],
        compiler_params=_cparams("parallel", "arbitrary"),
        name="conformer_conv",
    )(h, h, h, dw_w, dw_b.reshape(1, CONV_W), ln_g.reshape(1, CONV_W), ln_b.reshape(1, CONV_W))


def _dil_body(q0_ref, k0_ref, v0_ref, q1_ref, k1_ref, v1_ref, q2_ref, k2_ref, v2_ref, z_ref, qg_ref, kg_ref,
              o_ref, qn_s, kn_s, va_s, o0_s, o1_s, o2_s, l0_s, l1_s, l2_s):
    blk = DIL_BLOCK
    n_tiles = SEQ // blk
    groups = ((q0_ref, k0_ref, v0_ref, o0_s, l0_s), (q1_ref, k1_ref, v1_ref, o1_s, l1_s), (q2_ref, k2_ref, v2_ref, o2_s, l2_s))
    kn_s[0:blk, :] = jnp.zeros((blk, HEAD_DIM), bf16)
    va_s[0:blk, :] = jnp.zeros((blk, 2 * HEAD_DIM), bf16)
    ones = jnp.ones((blk, HEAD_DIM), bf16)
    row2 = lax.broadcasted_iota(jnp.int32, (blk, 2 * blk), 0)
    col2 = lax.broadcasted_iota(jnp.int32, (blk, 2 * blk), 1)
    band_prev = jnp.where((col2 >= row2) & (col2 <= row2 + blk), 0.0, NEG_INF)
    band_first = jnp.where((col2 >= blk) & (col2 <= row2 + blk), 0.0, NEG_INF)

    for (q_ref, k_ref, v_ref, o_s, l_s), (window, dil) in zip(groups, DIL_PATTERNS):
        assert window // dil == blk
        nb = SEQ // dil // blk

        def natural_rows(u, dil=dil, nb=nb):
            start = u // nb + (u % nb) * (blk * dil)
            return pl.ds(start, blk, stride=dil) if dil > 1 else pl.ds(pl.multiple_of(start, blk), blk)

        def prep(u, carry, q_ref=q_ref, k_ref=k_ref, v_ref=v_ref, natural_rows=natural_rows):
            src = natural_rows(u)
            qn_s[pl.ds(pl.multiple_of(u * blk, blk), blk), :] = _rms(q_ref[src, :], qg_ref[...]).astype(bf16)
            dst = pl.ds(pl.multiple_of((u + 1) * blk, blk), blk)
            kn_s[dst, :] = _rms(k_ref[src, :], kg_ref[...]).astype(bf16)
            va_s[dst, :] = jnp.concatenate([v_ref[src, :].astype(bf16), ones], axis=1)
            return carry

        lax.fori_loop(0, n_tiles, prep, 0, unroll=DIL_UNROLL)

        def att(u, carry, o_s=o_s, l_s=l_s, nb=nb, natural_rows=natural_rows):
            q = qn_s[pl.ds(pl.multiple_of(u * blk, blk), blk), :]
            keys = pl.ds(pl.multiple_of(u * blk, blk), 2 * blk)
            band = jnp.where((u % nb) > 0, band_prev, band_first)
            s = _dot_nt(q, kn_s[keys, :]) * SCALE + band
            m = jnp.max(s, axis=-1, keepdims=True)
            pv = _dot(jnp.exp(s - m).astype(bf16), va_s[keys, :])
            den = pv[:, HEAD_DIM:]
            dst = natural_rows(u)
            o_s[dst, :] = pv[:, :HEAD_DIM] / den
            l_s[dst, :] = m + jnp.log(den)
            return carry

        lax.fori_loop(0, n_tiles, att, 0, unroll=DIL_UNROLL)

    def mix(c, carry):
        rows = pl.ds(pl.multiple_of(c * blk, blk), blk)
        l0, l1, l2 = l0_s[rows, :], l1_s[rows, :], l2_s[rows, :]
        m = jnp.maximum(jnp.maximum(l0, l1), l2)
        w0, w1, w2 = jnp.exp(l0 - m), jnp.exp(l1 - m), jnp.exp(l2 - m)
        o = (w0 * o0_s[rows, :] + w1 * o1_s[rows, :] + w2 * o2_s[rows, :]) / (w0 + w1 + w2)
        o_ref[rows, :] = (o * _silu(z_ref[rows, :])).astype(o_ref.dtype)
        return carry

    lax.fori_loop(0, n_tiles, mix, 0, unroll=DIL_UNROLL)


def _dilated(h, q_g, k_g):
    b = h.shape[0]

    def col(off):
        return pl.BlockSpec((None, SEQ, HEAD_DIM), lambda bi, hi: (bi, 0, off + hi))

    gain = pl.BlockSpec((1, HEAD_DIM), lambda bi, hi: (0, 0))
    specs = []
    for gi in range(len(DIL_PATTERNS)):
        specs += [col(OD_DQ + gi * DIL_GROUP_HEADS), col(OD_DK + gi * DIL_GROUP_HEADS), col(OD_DV + gi * DIL_GROUP_HEADS)]
    return pl.pallas_call(
        _dil_body,
        grid=(b, DIL_GROUP_HEADS),
        in_specs=specs + [col(OD_DZ), gain, gain],
        out_specs=pl.BlockSpec((None, SEQ, HEAD_DIM), lambda bi, hi: (bi, 0, hi)),
        out_shape=jax.ShapeDtypeStruct((b, SEQ, DIL_GROUP_HEADS * HEAD_DIM), bf16),
        scratch_shapes=[pltpu.VMEM((SEQ, HEAD_DIM), bf16), pltpu.VMEM((SEQ + DIL_BLOCK, HEAD_DIM), bf16),
                        pltpu.VMEM((SEQ + DIL_BLOCK, 2 * HEAD_DIM), bf16)] + [pltpu.VMEM((SEQ, HEAD_DIM), f32)] * 6,
        compiler_params=_cparams("parallel", "parallel"),
        name="dilated",
    )(*([h] * 10), q_g.reshape(1, HEAD_DIM), k_g.reshape(1, HEAD_DIM))


def _even_w_in(w):
    split = EV_COLS - EV_TAIL_TILES * PROJ_TN
    ng0 = EV_W_NZ * LANE
    nz0 = ng0 + NSA_N_BRANCH * NSA_HEADS
    assert split <= ng0
    ng = w[:, ng0:nz0].reshape(-1, NSA_N_BRANCH, NSA_KV_HEADS, NSA_REP)
    ng = ng.transpose(0, 2, 1, 3).reshape(-1, NSA_KV_HEADS, NSA_N_BRANCH * NSA_REP)
    ng = jnp.pad(ng, ((0, 0), (0, 0), (0, LANE - NSA_N_BRANCH * NSA_REP))).reshape(-1, NSA_KV_HEADS * LANE)
    tail = jnp.concatenate([w[:, split:ng0], w[:, nz0:], ng], axis=1)
    tail = jnp.pad(tail, ((0, 0), (0, EV_TAIL_TILES * PROJ_TN - tail.shape[1])))
    return w.astype(bf16), tail.astype(bf16), split // PROJ_TN


def kernel(x, ev_norm, ev_w_in, ev_w_out, ev_ret_norm, ev_nsa_q_norm, ev_nsa_k_norm, ev_cmp_pos_k, ev_cmp_pos_v, ev_cmp_k_w1, ev_cmp_k_w2, ev_cmp_v_w1, ev_cmp_v_w2, od_norm, od_w_in, od_w_out, od_dw_w, od_dw_b, od_conv_norm_g, od_conv_norm_b, od_dil_q_norm, od_dil_k_norm):
    b, s, d = x.shape
    assert (s, d) == (SEQ, D_MODEL)
    x0 = x.reshape(b * s, d)

    h0, h0f = _proj_in(x0, ev_norm[0], *_even_w_in(ev_w_in[0]), EV_F32_TILE, 1)
    h0, h0f = h0.reshape(b, s, -1), h0f.reshape(b, s, -1)
    a_out = _retention(h0, ev_ret_norm[0])
    b_out = _nsa(h0, h0f, ev_nsa_q_norm[0], ev_nsa_k_norm[0], ev_cmp_pos_k[0], ev_cmp_pos_v[0],
                 ev_cmp_k_w1[0], ev_cmp_k_w2[0], ev_cmp_v_w1[0], ev_cmp_v_w2[0])
    x1 = _proj_out(x0, a_out.reshape(b * s, -1), b_out.reshape(b * s, -1), ev_w_out[0].astype(bf16))

    _, h1 = _proj_in(x1, od_norm[0], od_w_in[0].astype(bf16), None, OD_COLS // PROJ_TN, 0, OD_COLS // PROJ_TN)
    h1 = h1.reshape(b, s, -1)
    c_out = _conv(h1, od_dw_w[0], od_dw_b[0], od_conv_norm_g[0], od_conv_norm_b[0])
    d_out = _dilated(h1, od_dil_q_norm[0], od_dil_k_norm[0])
    x2 = _proj_out(x1, c_out.reshape(b * s, -1), d_out.reshape(b * s, -1), od_w_out[0].astype(bf16))
    return x2.reshape(b, s, d)
```

```python
import functools

import jax
import jax.numpy as jnp
import numpy as np
from jax import lax
from jax.experimental import pallas as pl
from jax.experimental.pallas import tpu as pltpu

f32 = jnp.float32
bf16 = jnp.bfloat16

D_MODEL = 2048
SEQ = 2048
HEAD_DIM = 128
SCALE = HEAD_DIM ** -0.5
NORM_EPS = 1e-6
NEG_INF = -1e30
LANE = 128
SUBLANE = 8

RET_HEADS = 8
RET_W = RET_HEADS * HEAD_DIM
RET_CHUNK = 256
RET_UNROLL = 8
ROPE_BASE = 10000.0
NSA_HEADS = 8
NSA_KV_HEADS = 2
NSA_REP = NSA_HEADS // NSA_KV_HEADS
NSA_N_BRANCH = 3
CMP_LEN = 32
CMP_STRIDE = 16
N_CMP = (SEQ - CMP_LEN) // CMP_STRIDE + 1
SEL_BLOCK = 64
N_SEL = SEQ // SEL_BLOCK
SEL_TOPK = 16
NSA_WINDOW = 512
NSA_TQ = 128
SEL_TK = 512
NSA_PREP_ROWS = 256
NSA_SELECT_TQ = 1024
WIN_SPAN = NSA_WINDOW + NSA_TQ
NSA_TILE_UNROLL = 2
CONV_W = 1024
CONV_K = 31
CONV_T = 512
CONV_HALO = 32
CONV_RB = 64
DIL_PATTERNS = ((128, 1), (512, 4), (2048, 16))
DIL_GROUP_HEADS = 4
DIL_HEADS = DIL_GROUP_HEADS * len(DIL_PATTERNS)
DIL_BLOCK = 128
DIL_UNROLL = 16

EV_RQ, EV_RK, EV_RV, EV_RZ = 0, 8, 16, 24
EV_NQ = 32
EV_W_NZ = 52
EV_COLS = 8192
EV_F32_TILE = 5
EV_F32_KC, EV_F32_VC, EV_F32_KS, EV_F32_VS = 0, 2, 4, 6
EV_KW, EV_VW = 40, 42
EV_NZ = 44
EV_NG = 52
OD_CU, OD_CG, OD_CZ = 0, 8, 16
OD_DQ, OD_DK, OD_DV = 24, 36, 48
OD_DZ = 60
OD_COLS = 8192

VMEM_LIMIT = 56 * 1024 * 1024


def _cparams(*sem):
    return pltpu.CompilerParams(dimension_semantics=sem, vmem_limit_bytes=VMEM_LIMIT)


def _rms(x, gain):
    ms = jnp.mean(x * x, axis=-1, keepdims=True)
    return x * lax.rsqrt(ms + NORM_EPS) * gain


def _silu(z):
    return z * jax.nn.sigmoid(z)


def _dot(a, b):
    return jnp.dot(a, b, preferred_element_type=f32)


def _dot_nt(a, b):
    return lax.dot_general(a, b, (((1,), (1,)), ((), ())), preferred_element_type=f32)


def _dot_tn(a, b):
    return lax.dot_general(a, b, (((0,), (0,)), ((), ())), preferred_element_type=f32)


PROJ_TM = 1024
PROJ_TN = 1024
NORM_ROWS = 256


def _proj_in_body(x_ref, g_ref, w_ref, *refs, f32_tile, f32_tiles, n_tiles):
    xn_ref = refs[-1]
    j = pl.program_id(1)

    @pl.when(j == 0)
    def _():
        for c in range(PROJ_TM // NORM_ROWS):
            rows = slice(c * NORM_ROWS, (c + 1) * NORM_ROWS)
            xn_ref[rows, :] = _rms(x_ref[rows, :], g_ref[...]).astype(bf16)

    def emit(out_ref):
        out_ref[...] = _dot(xn_ref[...], w_ref[...]).astype(out_ref.dtype)

    if f32_tiles == n_tiles:
        emit(refs[0])
    else:
        in_f32 = (j >= f32_tile) & (j < f32_tile + f32_tiles)
        pl.when(in_f32)(functools.partial(emit, refs[1]))
        pl.when(jnp.logical_not(in_f32))(functools.partial(emit, refs[0]))


def _proj_in(x2, gain, w, f32_tile, f32_tiles):
    m, d = x2.shape
    n = w.shape[1]
    n_tiles = n // PROJ_TN
    all_f32 = f32_tiles == n_tiles
    assert all_f32 or f32_tile >= 1
    bf16_spec = pl.BlockSpec((PROJ_TM, PROJ_TN), lambda i, j: (i, j - jnp.clip(j - f32_tile + 1, 0, f32_tiles)))
    f32_spec = pl.BlockSpec((PROJ_TM, PROJ_TN), lambda i, j: (i, jnp.clip(j - f32_tile, 0, f32_tiles - 1)))
    bf16_shape = jax.ShapeDtypeStruct((m, n - f32_tiles * PROJ_TN), bf16)
    f32_shape = jax.ShapeDtypeStruct((m, f32_tiles * PROJ_TN), f32)
    out = pl.pallas_call(
        functools.partial(_proj_in_body, f32_tile=f32_tile, f32_tiles=f32_tiles, n_tiles=n_tiles),
        grid=(m // PROJ_TM, n_tiles),
        in_specs=[
            pl.BlockSpec((PROJ_TM, d), lambda i, j: (i, 0)),
            pl.BlockSpec((1, d), lambda i, j: (0, 0)),
            pl.BlockSpec((d, PROJ_TN), lambda i, j: (0, j)),
        ],
        out_specs=[f32_spec] if all_f32 else [bf16_spec, f32_spec],
        out_shape=[f32_shape] if all_f32 else [bf16_shape, f32_shape],
        scratch_shapes=[pltpu.VMEM((PROJ_TM, d), bf16)],
        compiler_params=_cparams("parallel", "arbitrary"),
        name="proj_in",
    )(x2, gain.reshape(1, d), w)
    return (None, out[0]) if all_f32 else tuple(out)


OUT_TM = 512


def _proj_out_body(x_ref, m0_ref, m1_ref, w0_ref, w1_ref, o_ref):
    o_ref[...] = x_ref[...] + _dot(m0_ref[...], w0_ref[...]) + _dot(m1_ref[...], w1_ref[...])


def _proj_out(x2, m0, m1, w):
    m, d = x2.shape
    k0, k1 = m0.shape[1], m1.shape[1]
    assert w.shape == (k0 + k1, d) and k0 % k1 == 0
    return pl.pallas_call(
        _proj_out_body,
        grid=(m // OUT_TM,),
        in_specs=[
            pl.BlockSpec((OUT_TM, d), lambda i: (i, 0)),
            pl.BlockSpec((OUT_TM, k0), lambda i: (i, 0)),
            pl.BlockSpec((OUT_TM, k1), lambda i: (i, 0)),
            pl.BlockSpec((k0, d), lambda i: (0, 0)),
            pl.BlockSpec((k1, d), lambda i: (k0 // k1, 0)),
        ],
        out_specs=pl.BlockSpec((OUT_TM, d), lambda i: (i, 0)),
        out_shape=jax.ShapeDtypeStruct((m, d), f32),
        compiler_params=_cparams("parallel"),
        name="proj_out",
    )(x2, m0, m1, w, w)


def _ret_body(q_ref, k_ref, v_ref, z_ref, cos_ref, sin_ref, dmask_ref, kdec_ref, qdec_ref, cdec_ref, g_ref, o_ref):
    c_len = RET_CHUNK

    def rot(t, cs, sn):
        return t * cs + pltpu.roll(t, HEAD_DIM // 2, 1) * sn

    def body(n, state):
        rows = pl.ds(pl.multiple_of(n * c_len, c_len), c_len)
        cs = cos_ref[rows, :]
        sn = sin_ref[rows, :]
        q = rot(q_ref[rows, :].astype(f32), cs, sn)
        k = rot(k_ref[rows, :].astype(f32), cs, sn) * SCALE
        vb = v_ref[rows, :]
        s = _dot_nt(q.astype(bf16), k.astype(bf16)) * dmask_ref[...]
        o = _dot(s.astype(bf16), vb)
        o = o + _dot((q * qdec_ref[...]).astype(bf16), state.astype(bf16))
        kv = _dot_tn((k * kdec_ref[...]).astype(bf16), vb)
        state = cdec_ref[...] * state + kv
        y = _rms(o, g_ref[...])
        o_ref[rows, :] = (y * _silu(z_ref[rows, :].astype(f32))).astype(o_ref.dtype)
        return state

    lax.fori_loop(0, SEQ // c_len, body, jnp.zeros((HEAD_DIM, HEAD_DIM), f32), unroll=RET_UNROLL)


def _retention(h, ret_g):
    b = h.shape[0]
    c_len = RET_CHUNK
    half = HEAD_DIM // 2
    pos = jnp.arange(SEQ)
    inv = ROPE_BASE ** (-jnp.arange(half, dtype=f32) / half)
    ang = pos.astype(f32)[:, None] * inv[None, :]
    cos, sin = jnp.cos(ang), jnp.sin(ang)
    cos_t = jnp.concatenate([cos, cos], axis=-1)
    sin_t = jnp.concatenate([-sin, sin], axis=-1)
    log_g = jnp.log(1.0 - 2.0 ** (-5.0 - jnp.arange(RET_HEADS, dtype=f32)))
    idx = jnp.arange(c_len, dtype=f32)
    diff = idx[:, None] - idx[None, :]
    dmask = jnp.where(diff >= 0, jnp.exp(jnp.maximum(diff, 0.0)[None] * log_g[:, None, None]), 0.0)
    ones = jnp.ones((1, 1, HEAD_DIM), f32)
    k_dec = jnp.exp((c_len - 1 - idx)[None, :] * log_g[:, None])[:, :, None] * ones
    q_dec = jnp.exp((idx + 1.0)[None, :] * log_g[:, None])[:, :, None] * ones
    chunk_dec = jnp.exp(c_len * log_g)[:, None, None] * ones

    def col(off):
        return pl.BlockSpec((None, SEQ, HEAD_DIM), lambda bi, hi: (bi, 0, off + hi))

    table = pl.BlockSpec((SEQ, HEAD_DIM), lambda bi, hi: (0, 0))

    def per_head(rows, cols=HEAD_DIM):
        return pl.BlockSpec((None, rows, cols), lambda bi, hi: (hi, 0, 0))

    return pl.pallas_call(
        _ret_body,
        grid=(b, RET_HEADS),
        in_specs=[col(EV_RQ), col(EV_RK), col(EV_RV), col(EV_RZ), table, table,
                  per_head(c_len, c_len), per_head(c_len), per_head(c_len), per_head(1), per_head(1)],
        out_specs=pl.BlockSpec((None, SEQ, HEAD_DIM), lambda bi, hi: (bi, 0, hi)),
        out_shape=jax.ShapeDtypeStruct((b, SEQ, RET_W), bf16),
        compiler_params=_cparams("parallel", "parallel"),
        name="retention",
    )(h, h, h, h, cos_t, sin_t, dmask, k_dec, q_dec, chunk_dec, ret_g.reshape(RET_HEADS, 1, HEAD_DIM))


def _nsa_body(q_ref, kc_ref, vc_ref, ks_ref, vs_ref, kw_ref, vw_ref, nz_ref, ng_ref,
              qg_ref, kg_ref, posk_ref, posv_ref, w1k_ref, w2k_ref, w1v_ref, w2v_ref, ovl_ref, gsel_ref, eye_ref,
              o_ref, kcmp_s, vcmp_s, ksa_s, vsa_s, kwa_s, vwa_s, qn_s, not_sel_s):
    tq = NSA_TQ
    sq = NSA_SELECT_TQ
    rep = NSA_REP
    half_blk = CMP_LEN // 2
    d = HEAD_DIM
    exp2_scale = SCALE * float(np.log2(np.e))

    def compress(t_ref, pos_ref, w1_ref, w2_ref):
        first = jnp.zeros((SEQ // CMP_STRIDE, d), f32)
        second = jnp.zeros((SEQ // CMP_STRIDE, d), f32)
        for i in range(half_blk):
            ti = t_ref[pl.ds(i, SEQ // CMP_STRIDE, stride=CMP_STRIDE), :]
            first = first + _dot((ti + pos_ref[i:i + 1, :]).astype(bf16), w1_ref[i * d:(i + 1) * d, :])
            j = half_blk + i
            second = second + _dot((ti + pos_ref[j:j + 1, :]).astype(bf16), w1_ref[j * d:(j + 1) * d, :])
        pre = first + pltpu.roll(second, SEQ // CMP_STRIDE - 1, 0)
        return _dot(jax.nn.gelu(pre).astype(bf16), w2_ref[...])

    kcmp_s[...] = _rms(compress(kc_ref, posk_ref, w1k_ref, w2k_ref), kg_ref[0:1, :]).astype(bf16)
    vcmp_s[...] = compress(vc_ref, posv_ref, w1v_ref, w2v_ref).astype(bf16)
    lane0 = lax.broadcasted_iota(jnp.int32, (tq, d), 1) == 0
    pc_rows = NSA_PREP_ROWS
    ones = jnp.ones((pc_rows, d), bf16)
    zeros = jnp.zeros((pc_rows, d), bf16)
    for c in range(NSA_WINDOW // pc_rows):
        rows = slice(c * pc_rows, (c + 1) * pc_rows)
        kwa_s[rows, 0:d] = zeros
        kwa_s[rows, d:2 * d] = jnp.where(lax.broadcasted_iota(jnp.int32, (pc_rows, d), 1) == 0, NEG_INF, 0.0).astype(bf16)
        vwa_s[rows, 0:d] = zeros
        vwa_s[rows, d:2 * d] = zeros
    for c in range(SEQ // pc_rows):
        rows = slice(c * pc_rows, (c + 1) * pc_rows)
        wrows = slice(NSA_WINDOW + c * pc_rows, NSA_WINDOW + (c + 1) * pc_rows)
        ksa_s[rows, 0:d] = _rms(ks_ref[rows, :], kg_ref[1:2, :]).astype(bf16)
        ksa_s[rows, d:2 * d] = gsel_ref[rows, :]
        vsa_s[rows, 0:d] = vs_ref[rows, :].astype(bf16)
        vsa_s[rows, d:2 * d] = ones
        kwa_s[wrows, 0:d] = _rms(kw_ref[rows, :].astype(f32), kg_ref[2:3, :]).astype(bf16)
        kwa_s[wrows, d:2 * d] = zeros
        vwa_s[wrows, 0:d] = vw_ref[rows, :]
        vwa_s[wrows, d:2 * d] = ones

    row_i = lax.broadcasted_iota(jnp.int32, (tq, tq), 0)
    col_i = lax.broadcasted_iota(jnp.int32, (tq, tq), 1)
    band_first = jnp.where(col_i > row_i, 0.0, NEG_INF)
    band_last = jnp.where(col_i <= row_i, 0.0, NEG_INF)
    pad_flag = jnp.where(lane0, 1.0, 0.0).astype(bf16)

    def softmax_update(s, m, acc, v):
        n = s.shape[-1]
        m_new = jnp.maximum(m, jnp.max(s, axis=-1, keepdims=True))
        alpha = jnp.exp2((m - m_new) * exp2_scale)
        e = jnp.exp2((s - m_new) * exp2_scale)
        pv = _dot(e.reshape(rep * tq, n).astype(bf16), v).reshape(rep, tq, 2 * d)
        return m_new, alpha * acc + pv

    def select_tile(i, carry):
        t0 = i * sq
        rows = pl.ds(pl.multiple_of(t0, sq), sq)
        qs = [_rms(q_ref[rows, r * d:(r + 1) * d].astype(f32), qg_ref[...]).astype(bf16) for r in range(rep)]
        for r in range(rep):
            qn_s[r, rows, :] = qs[r]
        tpos = t0 + lax.broadcasted_iota(jnp.int32, (sq, 1), 0)
        qst = jnp.concatenate(qs, axis=0)
        n_idx = lax.broadcasted_iota(jnp.int32, (1, LANE), 1)
        valid = (n_idx * CMP_STRIDE + (CMP_LEN - 1) <= tpos) & (n_idx < N_CMP)
        cbias = jnp.where(valid, 0.0, NEG_INF)
        sc = (_dot_nt(qst, kcmp_s[...]) * SCALE).reshape(rep, sq, LANE) + cbias[None]
        mc = jnp.max(sc, axis=-1, keepdims=True)
        ec = jnp.exp(sc - mc)
        any_valid = jnp.where(tpos >= CMP_LEN - 1, 1.0, 0.0)
        pc = ec / jnp.sum(ec, axis=-1, keepdims=True) * any_valid[None]
        o_cmp = _dot(pc.reshape(rep * sq, LANE).astype(bf16), vcmp_s[...]).reshape(rep, sq, d)
        gates = jax.nn.sigmoid(ng_ref[rows, :].astype(f32))
        for r in range(rep):
            o_ref[rows, r * d:(r + 1) * d] = (gates[:, r:r + 1] * o_cmp[r]).astype(o_ref.dtype)
        psum = pc[0]
        for r in range(1, rep):
            psum = psum + pc[r]

        p_hi = psum.astype(bf16)
        rest = psum - p_hi.astype(f32)
        p_mid = rest.astype(bf16)
        p_lo = (rest - p_mid.astype(f32)).astype(bf16)
        imp = (_dot_nt(ovl_ref[...], p_hi) + _dot_nt(ovl_ref[...], p_mid) + _dot_nt(ovl_ref[...], p_lo))[:N_SEL, :]
        jj = lax.broadcasted_iota(jnp.int32, (N_SEL, 1), 0)
        cur = (t0 + lax.broadcasted_iota(jnp.int32, (1, sq), 1)) // SEL_BLOCK
        forced = (jj == 0) | (jj == cur) | (jj == cur - 1)
        imp = jnp.where(forced, -NEG_INF, jnp.where(jj > cur, NEG_INF, imp))
        rank = jnp.zeros((N_SEL, sq), f32)
        for j2 in range(N_SEL):
            row = imp[j2:j2 + 1, :]
            beats = (row > imp) | ((row == imp) & (j2 < jj))
            rank = rank + jnp.where(beats, 1.0, 0.0)
        not_sel_t = jnp.where(rank < SEL_TOPK, 0.0, 1.0).astype(bf16)
        not_sel_s[rows, :] = _dot_tn(not_sel_t, eye_ref[...]).astype(bf16)
        return carry

    lax.fori_loop(0, SEQ // sq, select_tile, 0)

    def make_tile(n_full):
        def tile(i, carry):
            t0 = i * tq
            rows = pl.ds(pl.multiple_of(t0, tq), tq)
            qs = [qn_s[r, rows, :] for r in range(rep)]
            tpos = t0 + lax.broadcasted_iota(jnp.int32, (tq, 1), 0)
            m_init = jnp.full((rep, tq, 1), NEG_INF, f32)
            acc_init = jnp.zeros((rep, tq, 2 * d), f32)

            qw = jnp.concatenate([jnp.concatenate([q, pad_flag], axis=1) for q in qs], axis=0)
            wrows = pl.ds(pl.multiple_of(t0, tq), WIN_SPAN)
            sw = _dot_nt(qw, kwa_s[wrows, :]).reshape(rep, tq, WIN_SPAN)
            sw = jnp.concatenate([sw[:, :, :tq] + band_first[None], sw[:, :, tq:NSA_WINDOW],
                                  sw[:, :, NSA_WINDOW:] + band_last[None]], axis=2)
            _, acc_w = softmax_update(sw, m_init, acc_init, vwa_s[wrows, :])
            o_win = acc_w[:, :, :d] / acc_w[:, :, d:]

            not_sel = not_sel_s[rows, :]
            qsel = jnp.concatenate([jnp.concatenate([q, not_sel], axis=1) for q in qs], axis=0)
            m_s, acc_s = m_init, acc_init
            for kt in range(n_full + 1):
                krows = slice(kt * SEL_TK, (kt + 1) * SEL_TK)
                s = _dot_nt(qsel, ksa_s[krows, :]).reshape(rep, tq, SEL_TK)
                if kt == n_full:
                    kpos = kt * SEL_TK + lax.broadcasted_iota(jnp.int32, (1, SEL_TK), 1)
                    s = s + jnp.where(kpos <= tpos, 0.0, NEG_INF)[None]
                m_s, acc_s = softmax_update(s, m_s, acc_s, vsa_s[krows, :])
            o_sel = acc_s[:, :, :d] / acc_s[:, :, d:]

            gates = jax.nn.sigmoid(ng_ref[rows, :].astype(f32))
            for r in range(rep):
                g_s = gates[:, rep + r:rep + r + 1]
                g_w = gates[:, 2 * rep + r:2 * rep + r + 1]
                cols = slice(r * d, (r + 1) * d)
                o = o_ref[rows, cols].astype(f32) + g_s * o_sel[r] + g_w * o_win[r]
                o_ref[rows, cols] = (o * _silu(nz_ref[rows, cols].astype(f32))).astype(o_ref.dtype)
            return carry
        return tile

    tiles_per_key_tile = SEL_TK // tq
    for n_full in range(SEQ // SEL_TK):
        lax.fori_loop(n_full * tiles_per_key_tile, (n_full + 1) * tiles_per_key_tile, make_tile(n_full), 0,
                      unroll=NSA_TILE_UNROLL)


def _nsa(h, hf, q_g, k_g, pos_k, pos_v, k_w1, k_w2, v_w1, v_w2):
    b = h.shape[0]
    grp_w = NSA_REP * HEAD_DIM
    cs = np.arange(LANE) * CMP_STRIDE
    ss = np.arange(LANE) * SEL_BLOCK
    ovl = ((cs[None, :] < ss[:, None] + SEL_BLOCK) & (cs[None, :] + CMP_LEN > ss[:, None])
           & (np.arange(LANE)[None, :] < N_CMP) & (np.arange(LANE)[:, None] < N_SEL))
    ovl_t = jnp.asarray(ovl, dtype=bf16)
    key_blk = np.arange(SEQ) // SEL_BLOCK
    gsel = jnp.asarray(np.where(key_blk[:, None] == np.arange(LANE)[None, :], NEG_INF, 0.0), dtype=bf16)
    eye = jnp.asarray(np.eye(N_SEL, LANE), dtype=bf16)

    def wide(off):
        return pl.BlockSpec((None, SEQ, grp_w), lambda bi, gi: (bi, 0, off // NSA_REP + gi))

    def col(off):
        return pl.BlockSpec((None, SEQ, HEAD_DIM), lambda bi, gi: (bi, 0, off + gi))

    def full(shape):
        return pl.BlockSpec(shape, lambda bi, gi: (0,) * len(shape))

    flat = CMP_LEN * HEAD_DIM
    return pl.pallas_call(
        _nsa_body,
        grid=(b, NSA_KV_HEADS),
        in_specs=[wide(EV_NQ), col(EV_F32_KC), col(EV_F32_VC), col(EV_F32_KS), col(EV_F32_VS), col(EV_KW), col(EV_VW),
                  wide(EV_NZ), col(EV_NG),
                  full((1, HEAD_DIM)), full((NSA_N_BRANCH, HEAD_DIM)), full((CMP_LEN, HEAD_DIM)), full((CMP_LEN, HEAD_DIM)),
                  full((flat, HEAD_DIM)), full((HEAD_DIM, HEAD_DIM)), full((flat, HEAD_DIM)), full((HEAD_DIM, HEAD_DIM)),
                  full((LANE, LANE)), full((SEQ, LANE)), full((N_SEL, LANE))],
        out_specs=pl.BlockSpec((None, SEQ, grp_w), lambda bi, gi: (bi, 0, gi)),
        out_shape=jax.ShapeDtypeStruct((b, SEQ, NSA_HEADS * HEAD_DIM), bf16),
        scratch_shapes=[pltpu.VMEM((SEQ // CMP_STRIDE, HEAD_DIM), bf16), pltpu.VMEM((SEQ // CMP_STRIDE, HEAD_DIM), bf16),
                        pltpu.VMEM((SEQ, 2 * HEAD_DIM), bf16), pltpu.VMEM((SEQ, 2 * HEAD_DIM), bf16),
                        pltpu.VMEM((SEQ + NSA_WINDOW, 2 * HEAD_DIM), bf16), pltpu.VMEM((SEQ + NSA_WINDOW, 2 * HEAD_DIM), bf16),
                        pltpu.VMEM((NSA_REP, SEQ, HEAD_DIM), bf16), pltpu.VMEM((SEQ, LANE), bf16)],
        compiler_params=_cparams("parallel", "parallel"),
        name="nsa",
    )(h, hf, hf, hf, hf, h, h, h, h,
      q_g.reshape(1, HEAD_DIM), k_g, pos_k, pos_v,
      k_w1.astype(bf16), k_w2.astype(bf16), v_w1.astype(bf16), v_w2.astype(bf16), ovl_t, gsel, eye)


def _conv_body(u_ref, g_ref, z_ref, w_ref, b_ref, lg_ref, lb_ref, o_ref, a_ext, a_sh, y_s):
    ext = CONV_HALO + CONV_T
    n_cb = CONV_W // LANE
    lead = CONV_HALO - (CONV_K - 1)

    @pl.when(pl.program_id(1) == 0)
    def _():
        for cb in range(n_cb):
            a_ext[cb, 0:CONV_HALO, :] = jnp.zeros((CONV_HALO, LANE), f32)
            a_ext[cb, ext:ext + SUBLANE, :] = jnp.zeros((SUBLANE, LANE), f32)

    for cb in range(n_cb):
        lanes = slice(cb * LANE, (cb + 1) * LANE)
        for c in range(CONV_T // 128):
            rows = slice(c * 128, (c + 1) * 128)
            a_ext[cb, CONV_HALO + c * 128:CONV_HALO + (c + 1) * 128, :] = u_ref[rows, lanes] * jax.nn.sigmoid(g_ref[rows, lanes])
        for r in range(1, SUBLANE):
            for c in range(ext // CONV_HALO):
                a_sh[r - 1, cb, c * CONV_HALO:(c + 1) * CONV_HALO, :] = a_ext[cb, c * CONV_HALO + r:(c + 1) * CONV_HALO + r, :]

        taps_w = [jnp.broadcast_to(w_ref[k:k + 1, lanes], (CONV_RB, LANE)) for k in range(CONV_K)]
        bias = jnp.broadcast_to(b_ref[:, lanes], (CONV_RB, LANE))

        def conv_rows(rb, carry, cb=cb, taps_w=taps_w, bias=bias):
            r0 = pl.multiple_of(rb * CONV_RB, CONV_RB)
            parts = [bias, None]
            for k in range(CONV_K):
                shift, base = (lead + k) % SUBLANE, (lead + k) // SUBLANE * SUBLANE
                rows = pl.ds(r0 + base, CONV_RB)
                tap = a_ext[cb, rows, :] if shift == 0 else a_sh[shift - 1, cb, rows, :]
                term = tap * taps_w[k]
                parts[k % 2] = term if parts[k % 2] is None else parts[k % 2] + term
            y_s[cb, pl.ds(r0, CONV_RB), :] = parts[0] + parts[1]
            return carry

        lax.fori_loop(0, CONV_T // CONV_RB, conv_rows, 0, unroll=2)
        a_ext[cb, 0:CONV_HALO, :] = a_ext[cb, CONV_T:CONV_T + CONV_HALO, :]

    def norm_rows(rb, carry):
        rows = pl.ds(pl.multiple_of(rb * CONV_RB, CONV_RB), CONV_RB)
        y = jnp.concatenate([y_s[cb, rows, :] for cb in range(n_cb)], axis=1)
        mu = jnp.mean(y, axis=-1, keepdims=True)
        var = jnp.mean(jnp.square(y - mu), axis=-1, keepdims=True)
        yn = (y - mu) * lax.rsqrt(var + NORM_EPS) * lg_ref[...] + lb_ref[...]
        o_ref[rows, :] = (_silu(yn) * _silu(z_ref[rows, :])).astype(o_ref.dtype)
        return carry

    lax.fori_loop(0, CONV_T // CONV_RB, norm_rows, 0, unroll=4)


def _conv(h, dw_w, dw_b, ln_g, ln_b):
    b = h.shape[0]

    def col(off):
        return pl.BlockSpec((None, CONV_T, CONV_W), lambda bi, ti: (bi, ti, off * LANE // CONV_W))

    def full(rows):
        return pl.BlockSpec((rows, CONV_W), lambda bi, ti: (0, 0))

    return pl.pallas_call(
        _conv_body,
        grid=(b, SEQ // CONV_T),
        in_specs=[col(OD_CU), col(OD_CG), col(OD_CZ), full(CONV_K), full(1), full(1), full(1)],
        out_specs=pl.BlockSpec((None, CONV_T, CONV_W), lambda bi, ti: (bi, ti, 0)),
        out_shape=jax.ShapeDtypeStruct((b, SEQ, CONV_W), bf16),
        scratch_shapes=[pltpu.VMEM((CONV_W // LANE, CONV_HALO + CONV_T + SUBLANE, LANE), f32),
                        pltpu.VMEM((SUBLANE - 1, CONV_W // LANE, CONV_HALO + CONV_T, LANE), f32),
                        pltpu.VMEM((CONV_W // LANE, CONV_T, LANE), f32)],
        compiler_params=_cparams("parallel", "arbitrary"),
        name="conformer_conv",
    )(h, h, h, dw_w, dw_b.reshape(1, CONV_W), ln_g.reshape(1, CONV_W), ln_b.reshape(1, CONV_W))


def _dil_body(q0_ref, k0_ref, v0_ref, q1_ref, k1_ref, v1_ref, q2_ref, k2_ref, v2_ref, z_ref, qg_ref, kg_ref,
              o_ref, qn_s, kn_s, va_s, o0_s, o1_s, o2_s, l0_s, l1_s, l2_s):
    blk = DIL_BLOCK
    n_tiles = SEQ // blk
    groups = ((q0_ref, k0_ref, v0_ref, o0_s, l0_s), (q1_ref, k1_ref, v1_ref, o1_s, l1_s), (q2_ref, k2_ref, v2_ref, o2_s, l2_s))
    kn_s[0:blk, :] = jnp.zeros((blk, HEAD_DIM), bf16)
    va_s[0:blk, :] = jnp.zeros((blk, 2 * HEAD_DIM), bf16)
    ones = jnp.ones((blk, HEAD_DIM), bf16)
    row2 = lax.broadcasted_iota(jnp.int32, (blk, 2 * blk), 0)
    col2 = lax.broadcasted_iota(jnp.int32, (blk, 2 * blk), 1)
    band_prev = jnp.where((col2 >= row2) & (col2 <= row2 + blk), 0.0, NEG_INF)
    band_first = jnp.where((col2 >= blk) & (col2 <= row2 + blk), 0.0, NEG_INF)

    for (q_ref, k_ref, v_ref, o_s, l_s), (window, dil) in zip(groups, DIL_PATTERNS):
        assert window // dil == blk
        nb = SEQ // dil // blk

        def natural_rows(u, dil=dil, nb=nb):
            start = u // nb + (u % nb) * (blk * dil)
            return pl.ds(start, blk, stride=dil) if dil > 1 else pl.ds(pl.multiple_of(start, blk), blk)

        def prep(u, carry, q_ref=q_ref, k_ref=k_ref, v_ref=v_ref, natural_rows=natural_rows):
            src = natural_rows(u)
            qn_s[pl.ds(pl.multiple_of(u * blk, blk), blk), :] = _rms(q_ref[src, :], qg_ref[...]).astype(bf16)
            dst = pl.ds(pl.multiple_of((u + 1) * blk, blk), blk)
            kn_s[dst, :] = _rms(k_ref[src, :], kg_ref[...]).astype(bf16)
            va_s[dst, :] = jnp.concatenate([v_ref[src, :].astype(bf16), ones], axis=1)
            return carry

        lax.fori_loop(0, n_tiles, prep, 0, unroll=DIL_UNROLL)

        def att(u, carry, o_s=o_s, l_s=l_s, nb=nb, natural_rows=natural_rows):
            q = qn_s[pl.ds(pl.multiple_of(u * blk, blk), blk), :]
            keys = pl.ds(pl.multiple_of(u * blk, blk), 2 * blk)
            band = jnp.where((u % nb) > 0, band_prev, band_first)
            s = _dot_nt(q, kn_s[keys, :]) * SCALE + band
            m = jnp.max(s, axis=-1, keepdims=True)
            pv = _dot(jnp.exp(s - m).astype(bf16), va_s[keys, :])
            den = pv[:, HEAD_DIM:]
            dst = natural_rows(u)
            o_s[dst, :] = pv[:, :HEAD_DIM] / den
            l_s[dst, :] = m + jnp.log(den)
            return carry

        lax.fori_loop(0, n_tiles, att, 0, unroll=DIL_UNROLL)

    def mix(c, carry):
        rows = pl.ds(pl.multiple_of(c * blk, blk), blk)
        l0, l1, l2 = l0_s[rows, :], l1_s[rows, :], l2_s[rows, :]
        m = jnp.maximum(jnp.maximum(l0, l1), l2)
        w0, w1, w2 = jnp.exp(l0 - m), jnp.exp(l1 - m), jnp.exp(l2 - m)
        o = (w0 * o0_s[rows, :] + w1 * o1_s[rows, :] + w2 * o2_s[rows, :]) / (w0 + w1 + w2)
        o_ref[rows, :] = (o * _silu(z_ref[rows, :])).astype(o_ref.dtype)
        return carry

    lax.fori_loop(0, n_tiles, mix, 0, unroll=DIL_UNROLL)


def _dilated(h, q_g, k_g):
    b = h.shape[0]

    def col(off):
        return pl.BlockSpec((None, SEQ, HEAD_DIM), lambda bi, hi: (bi, 0, off + hi))

    gain = pl.BlockSpec((1, HEAD_DIM), lambda bi, hi: (0, 0))
    specs = []
    for gi in range(len(DIL_PATTERNS)):
        specs += [col(OD_DQ + gi * DIL_GROUP_HEADS), col(OD_DK + gi * DIL_GROUP_HEADS), col(OD_DV + gi * DIL_GROUP_HEADS)]
    return pl.pallas_call(
        _dil_body,
        grid=(b, DIL_GROUP_HEADS),
        in_specs=specs + [col(OD_DZ), gain, gain],
        out_specs=pl.BlockSpec((None, SEQ, HEAD_DIM), lambda bi, hi: (bi, 0, hi)),
        out_shape=jax.ShapeDtypeStruct((b, SEQ, DIL_GROUP_HEADS * HEAD_DIM), bf16),
        scratch_shapes=[pltpu.VMEM((SEQ, HEAD_DIM), bf16), pltpu.VMEM((SEQ + DIL_BLOCK, HEAD_DIM), bf16),
                        pltpu.VMEM((SEQ + DIL_BLOCK, 2 * HEAD_DIM), bf16)] + [pltpu.VMEM((SEQ, HEAD_DIM), f32)] * 6,
        compiler_params=_cparams("parallel", "parallel"),
        name="dilated",
    )(*([h] * 10), q_g.reshape(1, HEAD_DIM), k_g.reshape(1, HEAD_DIM))


def _even_w_in(w):
    ng0 = EV_W_NZ * LANE
    nz0 = ng0 + NSA_N_BRANCH * NSA_HEADS
    ng = w[:, ng0:nz0].reshape(-1, NSA_N_BRANCH, NSA_KV_HEADS, NSA_REP)
    ng = ng.transpose(0, 2, 1, 3).reshape(-1, NSA_KV_HEADS, NSA_N_BRANCH * NSA_REP)
    ng = jnp.pad(ng, ((0, 0), (0, 0), (0, LANE - NSA_N_BRANCH * NSA_REP))).reshape(-1, NSA_KV_HEADS * LANE)
    out = jnp.concatenate([w[:, :ng0], w[:, nz0:], ng], axis=1)
    return jnp.pad(out, ((0, 0), (0, EV_COLS - out.shape[1]))).astype(bf16)


def kernel(x, ev_norm, ev_w_in, ev_w_out, ev_ret_norm, ev_nsa_q_norm, ev_nsa_k_norm, ev_cmp_pos_k, ev_cmp_pos_v, ev_cmp_k_w1, ev_cmp_k_w2, ev_cmp_v_w1, ev_cmp_v_w2, od_norm, od_w_in, od_w_out, od_dw_w, od_dw_b, od_conv_norm_g, od_conv_norm_b, od_dil_q_norm, od_dil_k_norm):
    b, s, d = x.shape
    assert (s, d) == (SEQ, D_MODEL)
    x0 = x.reshape(b * s, d)

    h0, h0f = _proj_in(x0, ev_norm[0], _even_w_in(ev_w_in[0]), EV_F32_TILE, 1)
    h0, h0f = h0.reshape(b, s, -1), h0f.reshape(b, s, -1)
    a_out = _retention(h0, ev_ret_norm[0])
    b_out = _nsa(h0, h0f, ev_nsa_q_norm[0], ev_nsa_k_norm[0], ev_cmp_pos_k[0], ev_cmp_pos_v[0],
                 ev_cmp_k_w1[0], ev_cmp_k_w2[0], ev_cmp_v_w1[0], ev_cmp_v_w2[0])
    x1 = _proj_out(x0, a_out.reshape(b * s, -1), b_out.reshape(b * s, -1), ev_w_out[0].astype(bf16))

    _, h1 = _proj_in(x1, od_norm[0], od_w_in[0].astype(bf16), 0, OD_COLS // PROJ_TN)
    h1 = h1.reshape(b, s, -1)
    c_out = _conv(h1, od_dw_w[0], od_dw_b[0], od_conv_norm_g[0], od_conv_norm_b[0])
    d_out = _dilated(h1, od_dil_q_norm[0], od_dil_k_norm[0])
    x2 = _proj_out(x1, c_out.reshape(b * s, -1), d_out.reshape(b * s, -1), od_w_out[0].astype(bf16))
    return x2.reshape(b, s, d)
```
